```python
import jax, jax.numpy as jnp
from jax import lax
import numpy as np

D_MODEL = 1024
BATCH = 4
SEQ = 4096
DEPTH = 4

GRID_W = 64
CTX_LEN = 256
EPS = 1e-6
F_FLOOR = 1e-20
N_MOD = 6
HG_WIDTH = 512
HG_HEADS = 4
HG_DIM = HG_WIDTH // HG_HEADS
CHUNK = 16
SC_WIDTH = 512
SC_HALF = SC_WIDTH // 2
CONV_W = 3
IN_COLS = 5 * HG_WIDTH + 3 * SC_WIDTH
SPLITS = tuple(HG_WIDTH * i for i in range(1, 6)) + (5 * HG_WIDTH + SC_WIDTH, 5 * HG_WIDTH + 2 * SC_WIDTH)
PEER_HEADS = 8
PEER_QDIM = 256
PEER_HALF = PEER_QDIM // 2
N_KEYS = 128
N_EXPERTS = N_KEYS * N_KEYS
PEER_TOPK = 16
PEER_BLOCK = 128

kernel_name = 'hybrid_hgrn2_shortconv_peer_dit'


def rms_norm(x, g):
    xf = x.astype(jnp.float32)
    y = xf * lax.rsqrt(jnp.mean(xf * xf, axis=-1, keepdims=True) + EPS)
    return (y * g.astype(jnp.float32)).astype(x.dtype)


def modulate(h, shift, scale):
    return h * (1 + scale) + shift


def to_heads(a):
    return a.reshape(a.shape[0], a.shape[1], HG_HEADS, HG_DIM).astype(jnp.float32)


def flip(a):
    return a[:, ::-1]


def forget_terms(z, lb):
    f = lb + (1.0 - lb) * jax.nn.sigmoid(z)
    logf = jnp.log(jnp.maximum(f, F_FLOOR))
    key = (1.0 - lb) * jax.nn.sigmoid(-z)
    return logf, key


def gla_chunk(q, k, v, logf, s0):
    bsz, t, h, _ = q.shape
    dv = v.shape[-1]
    n = t // CHUNK
    q, k, v, logf = (a.reshape(bsz, n, CHUNK, h, a.shape[-1]) for a in (q, k, v, logf))
    b = jnp.cumsum(logf, axis=2)
    mask = jnp.tril(jnp.ones((CHUNK, CHUNK), bool))[None, None, :, :, None, None]
    diff = b[:, :, :, None] - b[:, :, None, :]
    decay = jnp.where(mask, jnp.exp(jnp.where(mask, diff, 0.0)), 0.0)
    att = jnp.einsum('bnthk,bntshk,bnshk->bnhts', q, decay, k)
    o_intra = jnp.einsum('bnhts,bnshv->bnthv', att, v)
    b_last = b[:, :, -1]
    q_in = q * jnp.exp(b)
    k_out = k * jnp.exp(b_last[:, :, None] - b)

    def step(s, xs):
        qc, kc, vc, dc = xs
        o = jnp.einsum('bchk,bhkv->bchv', qc, s)
        s = dc[..., None] * s + jnp.einsum('bchk,bchv->bhkv', kc, vc)
        return s, o

    xs = tuple(jnp.moveaxis(a, 1, 0) for a in (q_in, k_out, v, jnp.exp(b_last)))
    s_final, o_inter = lax.scan(step, s0, xs)
    o = o_intra + jnp.moveaxis(o_inter, 0, 1)
    return o.reshape(bsz, t, h, dv), s_final


def gla_final_state(k, v, logf):
    b = jnp.cumsum(logf, axis=1)
    kd = k * jnp.exp(b[:, -1:] - b)
    return jnp.einsum('bthk,bthv->bhkv', kd, v)


def hgrn2_mix(q, iv, zf, zb, g, lbf, lbb, s0f, s0b):
    q, iv, zf, zb = (to_heads(a) for a in (q, iv, zf, zb))
    logf_f, k_f = forget_terms(zf, lbf)
    o_f, s_f = gla_chunk(q, k_f, iv, logf_f, s0f)
    logf_b, k_b = forget_terms(zb, lbb)
    o_b, s_b = gla_chunk(flip(q), flip(k_b), flip(iv), flip(logf_b), s0b)
    o = o_f + flip(o_b)
    o = o * lax.rsqrt(jnp.mean(o * o, axis=-1, keepdims=True) + EPS)
    o = o.reshape(g.shape).astype(g.dtype) * jax.nn.silu(g)
    return o, s_f, s_b


def hgrn2_final_states(iv, zf, zb, lbf, lbb):
    iv, zf, zb = (to_heads(a) for a in (iv, zf, zb))
    logf_f, k_f = forget_terms(zf, lbf)
    s_f = gla_final_state(k_f, iv, logf_f)
    logf_b, k_b = forget_terms(zb, lbb)
    s_b = gla_final_state(flip(k_b), flip(iv), flip(logf_b))
    return s_f, s_b


def conv3(u, w, axis):
    n = u.shape[axis]
    pad = [(0, 0)] * u.ndim
    pad[axis] = (1, 1)
    up = jnp.pad(u, pad)
    return (w[0] * lax.slice_in_dim(up, 0, n, axis=axis)
            + w[1] * lax.slice_in_dim(up, 1, n + 1, axis=axis)
            + w[2] * lax.slice_in_dim(up, 2, n + 2, axis=axis))


def grid_conv(u, w, rows):
    bsz, t, ch = u.shape
    grid = u.reshape(bsz, rows, GRID_W, ch)
    horiz = conv3(grid[..., :SC_HALF], w[:, :SC_HALF], axis=2)
    vert = conv3(grid[..., SC_HALF:], w[:, SC_HALF:], axis=1)
    return jnp.concatenate([horiz, vert], axis=-1).reshape(bsz, t, ch)


def seq_conv(u, w):
    return conv3(u, w, axis=1)


def token_mixer(h, w_in_l, w_out_l, conv_w_l, lbf, lbb, s0f, s0b, conv_fn):
    iv, zf, zb, q, g, cg, bg, hv = jnp.split(h @ w_in_l, SPLITS, axis=-1)
    o_rec, s_f, s_b = hgrn2_mix(q, iv, zf, zb, g, lbf, lbb, s0f, s0b)
    o_conv = bg * conv_fn(cg * hv, conv_w_l)
    y = jnp.concatenate([o_rec, o_conv], axis=-1) @ w_out_l
    return y, s_f, s_b


def peer_ffn(h, wq, subkeys, u, v):
    bsz, t, d = h.shape
    qr = (h @ wq).astype(jnp.float32).reshape(bsz, t, PEER_HEADS, 2, PEER_HALF)
    scores = jnp.einsum('bthpd,hpkd->bthpk', qr, subkeys.astype(jnp.float32))
    s1, i1 = lax.top_k(scores[..., 0, :], PEER_TOPK)
    s2, i2 = lax.top_k(scores[..., 1, :], PEER_TOPK)
    n_cand = PEER_TOPK * PEER_TOPK
    cand = (s1[..., :, None] + s2[..., None, :]).reshape(bsz, t, PEER_HEADS, n_cand)
    cidx = (i1[..., :, None] * N_KEYS + i2[..., None, :]).reshape(bsz, t, PEER_HEADS, n_cand)
    top, pos = lax.top_k(cand, PEER_TOPK)
    eidx = jnp.take_along_axis(cidx, pos, axis=-1)
    gate = jax.nn.softmax(top, axis=-1).astype(h.dtype)
    n_blk = bsz * t // PEER_BLOCK
    n_sel = PEER_HEADS * PEER_TOPK
    hb = h.reshape(n_blk, PEER_BLOCK, d)
    eb = eidx.reshape(n_blk, PEER_BLOCK, n_sel)
    gb = gate.reshape(n_blk, PEER_BLOCK, n_sel)

    def block(args):
        hk, ek, gk = args
        act = jax.nn.gelu(jnp.einsum('ted,td->te', u[ek], hk), approximate=False)
        return jnp.einsum('te,ted->td', gk * act, v[ek])

    return lax.map(block, (hb, eb, gb)).reshape(bsz, t, d)


def setup_inputs(seed: int = 0) -> dict:
    key = jax.random.key(seed)
    ks = jax.random.split(key, 17)
    nrm = lambda k, shape, scale: jax.random.normal(k, shape, jnp.float32) * scale
    mix_w = HG_WIDTH + SC_WIDTH
    return {
        'x': nrm(ks[0], (BATCH, SEQ, D_MODEL), 1.0),
        'c': nrm(ks[1], (BATCH, D_MODEL), 1.0),
        'ctx': nrm(ks[2], (BATCH, CTX_LEN, D_MODEL), 1.0),
        'c_ctx': nrm(ks[3], (D_MODEL,), 1.0),
        'w_mod': nrm(ks[4], (DEPTH, D_MODEL, N_MOD * D_MODEL), 0.5 * D_MODEL ** -0.5),
        'b_mod': nrm(ks[5], (DEPTH, N_MOD * D_MODEL), 0.01),
        'norm1_g': 1.0 + nrm(ks[6], (DEPTH, D_MODEL), 0.1),
        'norm2_g': 1.0 + nrm(ks[7], (DEPTH, D_MODEL), 0.1),
        'w_in': nrm(ks[8], (DEPTH, D_MODEL, IN_COLS), D_MODEL ** -0.5),
        'conv_w': nrm(ks[9], (DEPTH, CONV_W, SC_WIDTH), CONV_W ** -0.5),
        'w_out': nrm(ks[10], (DEPTH, mix_w, D_MODEL), mix_w ** -0.5),
        'lb_logits': nrm(ks[11], (DEPTH, 2, HG_WIDTH), 0.5),
        'peer_wq': nrm(ks[12], (DEPTH, D_MODEL, PEER_HEADS * PEER_QDIM), D_MODEL ** -0.5),
        'peer_subkeys': nrm(ks[13], (DEPTH, PEER_HEADS, 2, N_KEYS, PEER_HALF), PEER_HALF ** -0.5),
        'peer_u': nrm(ks[14], (DEPTH, N_EXPERTS, D_MODEL), D_MODEL ** -0.5),
        'peer_v': nrm(ks[15], (DEPTH, N_EXPERTS, D_MODEL), PEER_HEADS ** -0.5),
        'final_g': 1.0 + nrm(ks[16], (D_MODEL,), 0.1),
    }


def reference(x, c, ctx, c_ctx, w_mod, b_mod, norm1_g, norm2_g, w_in, conv_w, w_out,
              lb_logits, peer_wq, peer_subkeys, peer_u, peer_v, final_g):
    rows = x.shape[1] // GRID_W
    p_lb = jax.nn.softmax(lb_logits.astype(jnp.float32), axis=0)
    lower = jnp.cumsum(p_lb, axis=0) - p_lb[0]
    cond_x = jax.nn.silu(c)
    cond_c = jax.nn.silu(c_ctx)
    latent_conv = lambda u, w: grid_conv(u, w, rows)
    zero_state = jnp.zeros((ctx.shape[0], HG_HEADS, HG_DIM, HG_DIM), jnp.float32)
    xc = ctx
    for l in range(DEPTH):
        lbf = lower[l, 0].reshape(HG_HEADS, HG_DIM)
        lbb = lower[l, 1].reshape(HG_HEADS, HG_DIM)
        mod_x = (cond_x @ w_mod[l] + b_mod[l])[:, None, :]
        mod_c = (cond_c @ w_mod[l] + b_mod[l])[None, None, :]
        sh1x, sc1x, g1x, sh2x, sc2x, g2x = jnp.split(mod_x, N_MOD, axis=-1)
        sh1c, sc1c, g1c, sh2c, sc2c, g2c = jnp.split(mod_c, N_MOD, axis=-1)
        hc = modulate(rms_norm(xc, norm1_g[l]), sh1c, sc1c)
        if l < DEPTH - 1:
            yc, s_f, s_b = token_mixer(hc, w_in[l], w_out[l], conv_w[l], lbf, lbb,
                                       zero_state, zero_state, seq_conv)
            xc = xc + g1c * yc
            hc2 = modulate(rms_norm(xc, norm2_g[l]), sh2c, sc2c)
            xc = xc + g2c * peer_ffn(hc2, peer_wq[l], peer_subkeys[l], peer_u[l], peer_v[l])
        else:
            iv_c, zf_c, zb_c = jnp.split(hc @ w_in[l][:, :3 * HG_WIDTH], 3, axis=-1)
            s_f, s_b = hgrn2_final_states(iv_c, zf_c, zb_c, lbf, lbb)
        hx = modulate(rms_norm(x, norm1_g[l]), sh1x, sc1x)
        yx, _, _ = token_mixer(hx, w_in[l], w_out[l], conv_w[l], lbf, lbb, s_f, s_b, latent_conv)
        x = x + g1x * yx
        hx2 = modulate(rms_norm(x, norm2_g[l]), sh2x, sc2x)
        x = x + g2x * peer_ffn(hx2, peer_wq[l], peer_subkeys[l], peer_u[l], peer_v[l])
    return rms_norm(x, final_g)
```

```python
import functools
import math

import numpy as np
import jax
import jax.numpy as jnp
from jax import lax
from jax.experimental import pallas as pl
from jax.experimental.pallas import tpu as pltpu

F32 = jnp.float32
BF16 = jnp.bfloat16

D_MODEL = 1024
GRID_W = 64
EPS = 1e-6
F_FLOOR = 1e-20
N_MOD = 6
HG_WIDTH = 512
HG_HEADS = 4
HG_DIM = HG_WIDTH // HG_HEADS
SC_WIDTH = 512
SC_HALF = SC_WIDTH // 2
IN_COLS = 5 * HG_WIDTH + 3 * SC_WIDTH
PEER_HEADS = 8
PEER_QDIM = 256
PEER_HALF = PEER_QDIM // 2
N_KEYS = 128
N_EXPERTS = N_KEYS * N_KEYS
PEER_TOPK = 16

SUBLANES = 8
LANES = 128
VMEM_LIMIT = 48 * 1024 * 1024

GLA_CHUNK = 128
GLA_LEVELS = (64, 32, 16, 8)
COND_ROWS = 8

COL_IV, COL_ZF, COL_ZB, COL_Q, COL_G, COL_CG, COL_BG, COL_HV = range(8)


def _params(*sem):
    return pltpu.CompilerParams(dimension_semantics=sem, vmem_limit_bytes=VMEM_LIMIT)


def _dot(a, b):
    return jnp.dot(a.astype(BF16), b.astype(BF16), preferred_element_type=F32)


def _dot_nt(a, b):
    return lax.dot_general(a.astype(BF16), b.astype(BF16), (((1,), (1,)), ((), ())),
                           preferred_element_type=F32)


def _dot_tn(a, b):
    return lax.dot_general(a.astype(BF16), b.astype(BF16), (((0,), (0,)), ((), ())),
                           preferred_element_type=F32)


def _mod_kernel(cond_ref, w_ref, b_ref, o_ref):
    c = cond_ref[...]
    s = c * jax.nn.sigmoid(c)
    o_ref[0] = _dot(s, w_ref[0]) + b_ref[0]


def _mod_call(cond, w_mod, b_mod):
    depth, d, n = w_mod.shape
    tn = 1536
    return pl.pallas_call(
        _mod_kernel,
        grid=(depth, n // tn),
        in_specs=[
            pl.BlockSpec((COND_ROWS, d), lambda l, j: (0, 0)),
            pl.BlockSpec((1, d, tn), lambda l, j: (l, 0, j)),
            pl.BlockSpec((1, 1, tn), lambda l, j: (l, 0, j)),
        ],
        out_specs=pl.BlockSpec((1, COND_ROWS, tn), lambda l, j: (l, 0, j)),
        out_shape=jax.ShapeDtypeStruct((depth, COND_ROWS, n), F32),
        compiler_params=_params("parallel", "parallel"),
        name="mod",
    )(cond, w_mod, b_mod.reshape(depth, 1, n))


def _norm_mod(x, g, scale, shift):
    ms = jnp.mean(x * x, axis=-1, keepdims=True)
    y = x * lax.rsqrt(ms + EPS) * g
    return y * (1.0 + scale) + shift


def _proj_kernel(x_ref, g_ref, sc_ref, sh_ref, w_ref, o_ref):
    h = _norm_mod(x_ref[0], g_ref[...], sc_ref[0], sh_ref[0])
    o_ref[0] = _dot(h, w_ref[...])


def _proj_call(x, g, scale, shift, w):
    bsz, t, d = x.shape
    n = w.shape[1]
    tm = min(256, t)
    return pl.pallas_call(
        _proj_kernel,
        grid=(bsz, t // tm),
        in_specs=[
            pl.BlockSpec((1, tm, d), lambda b, i: (b, i, 0)),
            pl.BlockSpec((1, d), lambda b, i: (0, 0)),
            pl.BlockSpec((1, 1, d), lambda b, i: (b, 0, 0)),
            pl.BlockSpec((1, 1, d), lambda b, i: (b, 0, 0)),
            pl.BlockSpec((d, n), lambda b, i: (0, 0)),
        ],
        out_specs=pl.BlockSpec((1, tm, n), lambda b, i: (b, i, 0)),
        out_shape=jax.ShapeDtypeStruct((bsz, t, n), F32),
        compiler_params=_params("parallel", "parallel"),
        name="proj",
    )(x, g, scale, shift, w)


def _gla_sum_matrix(backward):
    c = GLA_CHUNK
    t = np.arange(c)[:, None]
    s = np.arange(c)[None, :]
    mats = []
    if not backward:
        mats.append(s <= t)
        mats.append(s > t)
    else:
        mats.append(s >= t)
        mats.append(s < t)
    for n in GLA_LEVELS:
        mid = (t // (2 * n)) * (2 * n) + n
        if not backward:
            m = np.where(t >= mid, (s >= mid) & (s <= t), (s > t) & (s < mid))
        else:
            m = np.where(t < mid, (s >= t) & (s < mid), (s >= mid) & (s < t))
        mats.append(m)
    return np.concatenate([m.astype(np.float32) for m in mats], axis=0)


def _gla_masks(backward):
    c = GLA_CHUNK
    row = lax.broadcasted_iota(jnp.int32, (c, c), 0)
    col = lax.broadcasted_iota(jnp.int32, (c, c), 1)
    qrow, lvl = [], []
    for n in GLA_LEVELS:
        r_hi = (row & (2 * n - 1)) >= n
        c_hi = (col & (2 * n - 1)) >= n
        same = (row & ~(2 * n - 1)) == (col & ~(2 * n - 1))
        if not backward:
            qrow.append(r_hi)
            lvl.append(same & r_hi & jnp.logical_not(c_hi))
        else:
            qrow.append(jnp.logical_not(r_hi))
            lvl.append(same & jnp.logical_not(r_hi) & c_hi)
    same8 = (row & ~(SUBLANES - 1)) == (col & ~(SUBLANES - 1))
    order = (col >= row) if backward else (col <= row)
    pair = [same8 & order & ((col & (SUBLANES - 1)) == i) for i in range(SUBLANES)]
    return qrow, lvl, pair


def _bcast_group_row(x, i):
    c, w = x.shape
    x3 = x.reshape(c // SUBLANES, SUBLANES, w)
    return jnp.broadcast_to(x3[:, i:i + 1, :], x3.shape).reshape(c, w)


def _gla_chain(z, q, v, lb, st, msum, masks, ones, backward):
    c = GLA_CHUNK
    qrow, lvl, pair = masks
    sig = jax.nn.sigmoid(z)
    f = lb + (1.0 - lb) * sig
    logf = jnp.log(jnp.maximum(f, F_FLOOR))
    kk = (1.0 - lb) * jax.nn.sigmoid(-z)
    hi = logf.astype(BF16)
    lo = (logf - hi.astype(F32)).astype(BF16)
    sums = (jnp.dot(msum, hi, preferred_element_type=F32)
            + jnp.dot(msum, lo, preferred_element_type=F32))
    b = sums[0:c]
    e_out = sums[c:2 * c]

    att = jnp.zeros((c, c), F32)
    for li in range(len(GLA_LEVELS)):
        scale = jnp.exp(sums[(2 + li) * c:(3 + li) * c])
        mixed = (jnp.where(qrow[li][:, :HG_DIM], q, kk) * scale).astype(BF16)
        a = lax.dot_general(mixed, mixed, (((1,), (1,)), ((), ())), preferred_element_type=F32)
        att = jnp.where(lvl[li], a, att)
    for i in range(SUBLANES):
        p = q * _bcast_group_row(kk, i) * jnp.exp(-jnp.abs(b - _bcast_group_row(b, i)))
        w = jnp.dot(p.astype(BF16), ones, preferred_element_type=F32)
        att = jnp.where(pair[i], w, att)

    q_in = q * jnp.exp(b)
    o = _dot(att, v) + _dot_nt(q_in, st)
    k_out = kk * jnp.exp(e_out)
    d = jnp.exp(b[0:1, :] if backward else b[c - 1:c, :])
    st_new = d * st + _dot_tn(v, k_out)
    return o, st_new


def _gla_kernel(ivf_ref, zf_ref, qf_ref, ivb_ref, zb_ref, qb_ref, lbf_ref, lbb_ref,
                s0f_ref, s0b_ref, mf_ref, mb_ref,
                of_ref, ob_ref, sf_ref, sb_ref, st_ref):
    n = pl.program_id(1)

    @pl.when(n == 0)
    def _():
        st_ref[0] = s0f_ref[0]
        st_ref[1] = s0b_ref[0]

    ones = jnp.ones((HG_DIM, HG_DIM), BF16)
    masks_f = _gla_masks(False)
    masks_b = _gla_masks(True)
    for h in range(HG_HEADS):
        sl = slice(h * HG_DIM, (h + 1) * HG_DIM)
        o, st = _gla_chain(zf_ref[0, :, sl], qf_ref[0, :, sl], ivf_ref[0, :, sl], lbf_ref[:, sl],
                           st_ref[0, h], mf_ref[...], masks_f, ones, False)
        of_ref[0, :, sl] = o
        st_ref[0, h] = st
        o, st = _gla_chain(zb_ref[0, :, sl], qb_ref[0, :, sl], ivb_ref[0, :, sl], lbb_ref[:, sl],
                           st_ref[1, h], mb_ref[...], masks_b, ones, True)
        ob_ref[0, :, sl] = o
        st_ref[1, h] = st

    @pl.when(n == pl.num_programs(1) - 1)
    def _():
        sf_ref[0] = st_ref[0]
        sb_ref[0] = st_ref[1]


def _gla_call(proj, lbf, lbb, s0f, s0b):
    bsz, t, _ = proj.shape
    c = GLA_CHUNK
    nc = t // c
    w = HG_WIDTH
    msum_f = jnp.asarray(_gla_sum_matrix(False), BF16)
    msum_b = jnp.asarray(_gla_sum_matrix(True), BF16)

    def fwd(col):
        return pl.BlockSpec((1, c, w), lambda b, n: (b, n, col))

    def bwd(col):
        return pl.BlockSpec((1, c, w), lambda b, n: (b, nc - 1 - n, col))

    state_spec = pl.BlockSpec((1, HG_HEADS, HG_DIM, HG_DIM), lambda b, n: (b, 0, 0, 0))
    const = lambda shape: pl.BlockSpec(shape, lambda b, n: (0,) * len(shape))
    return pl.pallas_call(
        _gla_kernel,
        grid=(bsz, nc),
        in_specs=[fwd(COL_IV), fwd(COL_ZF), fwd(COL_Q), bwd(COL_IV), bwd(COL_ZB), bwd(COL_Q),
                  const((1, w)), const((1, w)), state_spec, state_spec,
                  const(msum_f.shape), const(msum_b.shape)],
        out_specs=[pl.BlockSpec((1, c, w), lambda b, n: (b, n, 0)),
                   pl.BlockSpec((1, c, w), lambda b, n: (b, nc - 1 - n, 0)),
                   state_spec, state_spec],
        out_shape=[jax.ShapeDtypeStruct((bsz, t, w), F32),
                   jax.ShapeDtypeStruct((bsz, t, w), F32),
                   jax.ShapeDtypeStruct((bsz, HG_HEADS, HG_DIM, HG_DIM), F32),
                   jax.ShapeDtypeStruct((bsz, HG_HEADS, HG_DIM, HG_DIM), F32)],
        scratch_shapes=[pltpu.VMEM((2, HG_HEADS, HG_DIM, HG_DIM), F32)],
        compiler_params=_params("parallel", "arbitrary"),
        name="gla",
    )(proj, proj, proj, proj, proj, proj, lbf, lbb, s0f, s0b, msum_f, msum_b)


def _mixout_body(of_ref, ob_ref, g_ref, cg_ref, bg_ref, hv_ref, halo, cw_ref, wo_ref,
                 x_ref, g1_ref, o_ref, grid_conv):
    tm = of_ref.shape[1]
    o = of_ref[0] + ob_ref[0]
    heads = []
    for h in range(HG_HEADS):
        oh = o[:, h * HG_DIM:(h + 1) * HG_DIM]
        heads.append(oh * lax.rsqrt(jnp.mean(oh * oh, axis=-1, keepdims=True) + EPS))
    g = g_ref[0]
    o_rec = jnp.concatenate(heads, axis=-1) * (g * jax.nn.sigmoid(g))

    u = cg_ref[0] * hv_ref[0]
    w0, w1, w2 = cw_ref[0:1, :], cw_ref[1:2, :], cw_ref[2:3, :]
    pos = lax.broadcasted_iota(jnp.int32, (tm, 1), 0)
    if grid_conv:
        cgp_ref, hvp_ref, cgn_ref, hvn_ref = halo
        i = pl.program_id(1)
        last = pl.num_programs(1) - 1
        colpos = pos & (GRID_W - 1)
        uh = u[:, :SC_HALF]
        left = jnp.where(colpos == 0, 0.0, pltpu.roll(uh, 1, 0))
        right = jnp.where(colpos == GRID_W - 1, 0.0, pltpu.roll(uh, tm - 1, 0))
        conv_h = w0[:, :SC_HALF] * left + w1[:, :SC_HALF] * uh + w2[:, :SC_HALF] * right
        uv = u[:, SC_HALF:]
        up_halo = jnp.where(i == 0, 0.0, cgp_ref[0][:, SC_HALF:] * hvp_ref[0][:, SC_HALF:])
        dn_halo = jnp.where(i == last, 0.0, cgn_ref[0][:, SC_HALF:] * hvn_ref[0][:, SC_HALF:])
        up = jnp.concatenate([up_halo, uv[:tm - GRID_W]], axis=0)
        down = jnp.concatenate([uv[GRID_W:], dn_halo], axis=0)
        conv_v = w0[:, SC_HALF:] * up + w1[:, SC_HALF:] * uv + w2[:, SC_HALF:] * down
        conv = jnp.concatenate([conv_h, conv_v], axis=-1)
    else:
        left = jnp.where(pos == 0, 0.0, pltpu.roll(u, 1, 0))
        right = jnp.where(pos == tm - 1, 0.0, pltpu.roll(u, tm - 1, 0))
        conv = w0 * left + w1 * u + w2 * right
    o_conv = bg_ref[0] * conv
    y = _dot(o_rec, wo_ref[0:HG_WIDTH, :]) + _dot(o_conv, wo_ref[HG_WIDTH:, :])
    o_ref[0] = x_ref[0] + g1_ref[0] * y


def _mixout_grid_kernel(of_ref, ob_ref, g_ref, cg_ref, bg_ref, hv_ref, cgp_ref, hvp_ref,
                        cgn_ref, hvn_ref, cw_ref, wo_ref, x_ref, g1_ref, o_ref):
    _mixout_body(of_ref, ob_ref, g_ref, cg_ref, bg_ref, hv_ref,
                 (cgp_ref, hvp_ref, cgn_ref, hvn_ref), cw_ref, wo_ref, x_ref, g1_ref, o_ref, True)


def _mixout_seq_kernel(of_ref, ob_ref, g_ref, cg_ref, bg_ref, hv_ref, cw_ref, wo_ref,
                       x_ref, g1_ref, o_ref):
    _mixout_body(of_ref, ob_ref, g_ref, cg_ref, bg_ref, hv_ref, None, cw_ref, wo_ref,
                 x_ref, g1_ref, o_ref, False)


def _mixout_call(o_f, o_b, proj, conv_w, w_out, x, g1, grid_conv):
    bsz, t, d = x.shape
    w = HG_WIDTH
    tm = 512 if grid_conv else t
    nt = t // tm
    hb = tm // GRID_W
    nhalo = t // GRID_W

    def col(cidx):
        return pl.BlockSpec((1, tm, w), lambda b, i: (b, i, cidx))

    def prev(cidx):
        return pl.BlockSpec((1, GRID_W, w), lambda b, i: (b, jnp.maximum(i * hb - 1, 0), cidx))

    def nxt(cidx):
        return pl.BlockSpec((1, GRID_W, w),
                            lambda b, i: (b, jnp.minimum((i + 1) * hb, nhalo - 1), cidx))

    row = pl.BlockSpec((1, tm, w), lambda b, i: (b, i, 0))
    in_specs = [row, row, col(COL_G), col(COL_CG), col(COL_BG), col(COL_HV)]
    args = [o_f, o_b, proj, proj, proj, proj]
    if grid_conv:
        in_specs += [prev(COL_CG), prev(COL_HV), nxt(COL_CG), nxt(COL_HV)]
        args += [proj, proj, proj, proj]
    in_specs += [
        pl.BlockSpec((3, w), lambda b, i: (0, 0)),
        pl.BlockSpec(w_out.shape, lambda b, i: (0, 0)),
        pl.BlockSpec((1, tm, d), lambda b, i: (b, i, 0)),
        pl.BlockSpec((1, 1, d), lambda b, i: (b, 0, 0)),
    ]
    args += [conv_w, w_out, x, g1]
    return pl.pallas_call(
        _mixout_grid_kernel if grid_conv else _mixout_seq_kernel,
        grid=(bsz, nt),
        in_specs=in_specs,
        out_specs=pl.BlockSpec((1, tm, d), lambda b, i: (b, i, 0)),
        out_shape=jax.ShapeDtypeStruct((bsz, t, d), F32),
        compiler_params=_params("parallel", "parallel"),
        name="mixout_grid" if grid_conv else "mixout_seq",
    )(*args)


def _oddeven_merge_sort_pairs(n):
    pairs = []
    p = 1
    while p < n:
        k = p
        while k >= 1:
            for j in range(k % p, n - k, 2 * k):
                for i in range(min(k, n - j - k)):
                    if (i + j) // (2 * p) == (i + j + k) // (2 * p):
                        pairs.append((i + j, i + j + k))
            k //= 2
        p *= 2
    return pairs


_SORT16 = _oddeven_merge_sort_pairs(PEER_TOPK)


def _sort_desc(xs):
    xs = list(xs)
    for i, j in _SORT16:
        a, b = xs[i], xs[j]
        xs[i], xs[j] = jnp.maximum(a, b), jnp.minimum(a, b)
    return xs


def _bitonic_merge_desc(xs):
    xs = list(xs)
    d = len(xs) // 2
    while d >= 1:
        for i in range(len(xs)):
            if i & d == 0:
                a, b = xs[i], xs[i + d]
                xs[i], xs[i + d] = jnp.maximum(a, b), jnp.minimum(a, b)
        d //= 2
    return xs


def _top_of_union(a, b):
    k = len(a)
    return [jnp.maximum(a[r], b[k - 1 - r]) for r in range(k)]


def _topk_rows(s):
    rows = [s[SUBLANES * v:SUBLANES * (v + 1), :] for v in range(N_KEYS // SUBLANES)]
    rows = _sort_desc(rows)
    for shift in (4, 2, 1):
        rolled = [pltpu.roll(r, shift, 0) for r in rows]
        rows = _bitonic_merge_desc(_top_of_union(rows, rolled))
    return rows


def _peer_prep_kernel(x_ref, g_ref, sc_ref, sh_ref, wq_ref, keys_ref,
                      hb_ref, s1_ref, e1_ref, s2_ref, e2_ref, tau_ref):
    tq = x_ref.shape[1]
    h = _norm_mod(x_ref[0], g_ref[...], sc_ref[0], sh_ref[0]).astype(BF16)
    hb_ref[0] = h
    qf = jnp.dot(h, wq_ref[...], preferred_element_type=F32)
    sub = lax.broadcasted_iota(jnp.int32, (SUBLANES, tq), 0)
    neg = jnp.full((SUBLANES, tq), -jnp.inf, F32)
    packed = [[jnp.zeros((SUBLANES, tq), F32)] * PEER_TOPK for _ in range(2)]
    for hd in range(PEER_HEADS):
        for p in range(2):
            lo = hd * PEER_QDIM + p * PEER_HALF
            s = _dot_nt(keys_ref[hd, p], qf[:, lo:lo + PEER_HALF])
            (s1_ref if p == 0 else s2_ref)[hd] = s
            top = _topk_rows(s)
            packed[p] = [jnp.where(sub == hd, top[r], packed[p][r]) for r in range(PEER_TOPK)]
    c1, c2 = packed
    cands = [c1[a] + c2[b] for a in range(PEER_TOPK) for b in range(PEER_TOPK)
             if (a + 1) * (b + 1) <= PEER_TOPK]
    cands += [neg] * (-len(cands) % PEER_TOPK)
    groups = [_sort_desc(cands[i:i + PEER_TOPK]) for i in range(0, len(cands), PEER_TOPK)]
    while len(groups) > 2:
        nxt = [_bitonic_merge_desc(_top_of_union(groups[i], groups[i + 1]))
               for i in range(0, len(groups) - 1, 2)]
        if len(groups) % 2:
            nxt.append(groups[-1])
        groups = nxt
    top = _top_of_union(groups[0], groups[1]) if len(groups) == 2 else groups[0]
    tau = functools.reduce(jnp.minimum, top)
    m1, m2 = c1[0], c2[0]
    mx = m1 + m2
    z = functools.reduce(lambda a, b: a + b, [jnp.exp(t - mx) for t in top])
    inv_z = 1.0 / z
    tau_ref[...] = tau
    for hd in range(PEER_HEADS):
        e1_ref[hd] = jnp.exp(s1_ref[hd] - m1[hd:hd + 1, :]) * inv_z[hd:hd + 1, :]
        e2_ref[hd] = jnp.exp(s2_ref[hd] - m2[hd:hd + 1, :])


def _peer_prep_call(x, g, scale, shift, wq, keys):
    bsz, t, d = x.shape
    tq = min(256, t)
    nt = t // tq
    ttot = bsz * t
    tok = lambda b, i: (0, 0, b * nt + i)
    score_spec = pl.BlockSpec((PEER_HEADS, N_KEYS, tq), tok)
    score_shape = jax.ShapeDtypeStruct((PEER_HEADS, N_KEYS, ttot), F32)
    return pl.pallas_call(
        _peer_prep_kernel,
        grid=(bsz, nt),
        in_specs=[
            pl.BlockSpec((1, tq, d), lambda b, i: (b, i, 0)),
            pl.BlockSpec((1, d), lambda b, i: (0, 0)),
            pl.BlockSpec((1, 1, d), lambda b, i: (b, 0, 0)),
            pl.BlockSpec((1, 1, d), lambda b, i: (b, 0, 0)),
            pl.BlockSpec(wq.shape, lambda b, i: (0, 0)),
            pl.BlockSpec(keys.shape, lambda b, i: (0, 0, 0, 0)),
        ],
        out_specs=[pl.BlockSpec((1, tq, d), lambda b, i: (b, i, 0)),
                   score_spec, score_spec, score_spec, score_spec,
                   pl.BlockSpec((PEER_HEADS, tq), lambda b, i: (0, b * nt + i))],
        out_shape=[jax.ShapeDtypeStruct((bsz, t, d), BF16),
                   score_shape, score_shape, score_shape, score_shape,
                   jax.ShapeDtypeStruct((PEER_HEADS, ttot), F32)],
        compiler_params=_params("parallel", "parallel"),
        name="peer_prep",
    )(x, g, scale, shift, wq, keys)


PEER_TE = SUBLANES * N_KEYS
SQRT_HALF = math.sqrt(0.5)


def _peer_dense_kernel(hb_ref, u_ref, vt_ref, s1_ref, e1_ref, s2_ref, e2_ref, tau_ref,
                       x_ref, g2_ref, o_ref, acc_ref, a_ref, gt_ref):
    e = pl.program_id(1)
    tt = hb_ref.shape[0]
    n_lane_tiles = tt // LANES
    n_i2b = N_KEYS // SUBLANES

    @pl.when(e == 0)
    def _():
        acc_ref[...] = jnp.zeros_like(acc_ref)

    a_ref[...] = lax.dot_general(u_ref[...], hb_ref[...], (((1,), (1,)), ((), ())),
                                 preferred_element_type=F32)

    for il in range(SUBLANES):
        def body(j, carry, il=il):
            lanes = pl.ds(pl.multiple_of(j * LANES, LANES), LANES)
            acc = [jnp.zeros((SUBLANES, LANES), F32)] * n_i2b
            for hd in range(PEER_HEADS):
                s1b = jnp.broadcast_to(s1_ref[hd, e, pl.ds(il, 1), lanes], (SUBLANES, LANES))
                e1b = jnp.broadcast_to(e1_ref[hd, e, pl.ds(il, 1), lanes], (SUBLANES, LANES))
                taub = jnp.broadcast_to(tau_ref[pl.ds(hd, 1), lanes], (SUBLANES, LANES))
                for k in range(n_i2b):
                    rows = pl.ds(k * SUBLANES, SUBLANES)
                    sel = (s1b + s2_ref[hd, rows, lanes]) >= taub
                    acc[k] = acc[k] + jnp.where(sel, e2_ref[hd, rows, lanes], 0.0) * e1b
            for k in range(n_i2b):
                rows = pl.ds(il * N_KEYS + k * SUBLANES, SUBLANES)
                a = a_ref[rows, lanes]
                gelu = 0.5 * a * (1.0 + lax.erf(a * SQRT_HALF))
                gt_ref[rows, lanes] = acc[k] * gelu
            return carry

        lax.fori_loop(0, n_lane_tiles, body, 0)
    acc_ref[...] += jnp.dot(vt_ref[...], gt_ref[...].astype(BF16), preferred_element_type=F32)

    @pl.when(e == pl.num_programs(1) - 1)
    def _():
        o_ref[...] = x_ref[...] + g2_ref[0] * acc_ref[...].T


def _peer_dense_call(hb, u, vt, s1, e1, s2, e2, tau, x, g2, t_per_batch):
    ttot, d = x.shape
    tt = min(512, t_per_batch)
    n_e = N_EXPERTS // PEER_TE
    per_batch = t_per_batch // tt
    score_spec = pl.BlockSpec((PEER_HEADS, N_KEYS, tt), lambda i, e: (0, 0, i))
    grouped = (PEER_HEADS, N_KEYS // SUBLANES, SUBLANES, ttot)
    grouped_spec = pl.BlockSpec(grouped[:3] + (tt,), lambda i, e: (0, 0, 0, i))
    s1 = s1.reshape(grouped)
    e1 = e1.reshape(grouped)
    return pl.pallas_call(
        _peer_dense_kernel,
        grid=(ttot // tt, n_e),
        in_specs=[
            pl.BlockSpec((tt, d), lambda i, e: (i, 0)),
            pl.BlockSpec((PEER_TE, d), lambda i, e: (e, 0)),
            pl.BlockSpec((d, PEER_TE), lambda i, e: (0, e)),
            grouped_spec, grouped_spec, score_spec, score_spec,
            pl.BlockSpec((PEER_HEADS, tt), lambda i, e: (0, i)),
            pl.BlockSpec((tt, d), lambda i, e: (i, 0)),
            pl.BlockSpec((1, 1, d), lambda i, e: (i // per_batch, 0, 0)),
        ],
        out_specs=pl.BlockSpec((tt, d), lambda i, e: (i, 0)),
        out_shape=jax.ShapeDtypeStruct((ttot, d), F32),
        scratch_shapes=[pltpu.VMEM((d, tt), F32),
                        pltpu.VMEM((PEER_TE, tt), F32),
                        pltpu.VMEM((PEER_TE, tt), F32)],
        compiler_params=_params("parallel", "arbitrary"),
        name="peer_dense",
    )(hb, u, vt, s1, e1, s2, e2, tau, x, g2)


def _final_norm_kernel(x_ref, g_ref, o_ref):
    x = x_ref[...]
    ms = jnp.mean(x * x, axis=-1, keepdims=True)
    o_ref[...] = x * lax.rsqrt(ms + EPS) * g_ref[...]


def _final_norm_call(x, g):
    n, d = x.shape
    tm = 1024
    return pl.pallas_call(
        _final_norm_kernel,
        grid=(n // tm,),
        in_specs=[pl.BlockSpec((tm, d), lambda i: (i, 0)), pl.BlockSpec((1, d), lambda i: (0, 0))],
        out_specs=pl.BlockSpec((tm, d), lambda i: (i, 0)),
        out_shape=jax.ShapeDtypeStruct((n, d), F32),
        compiler_params=_params("parallel"),
        name="final_norm",
    )(x, g)


def _mixer(x, mod, l, norm1_g, w_in, conv_w, w_out, lbf, lbb, s0f, s0b, grid_conv, full):
    sh1, sc1, g1 = mod[0], mod[1], mod[2]
    proj = _proj_call(x, norm1_g[l][None, :], sc1, sh1, w_in[l])
    o_f, o_b, s_f, s_b = _gla_call(proj, lbf, lbb, s0f, s0b)
    if not full:
        return None, s_f, s_b
    x = _mixout_call(o_f, o_b, proj, conv_w[l], w_out[l], x, g1, grid_conv)
    return x, s_f, s_b


def _peer(x, mod, l, norm2_g, wq, keys, u, vt):
    bsz, t, d = x.shape
    sh2, sc2, g2 = mod[3], mod[4], mod[5]
    hb, s1, e1, s2, e2, tau = _peer_prep_call(x, norm2_g[l][None, :], sc2, sh2, wq[l], keys[l])
    out = _peer_dense_call(hb.reshape(bsz * t, d), u[l], vt[l], s1, e1, s2, e2, tau,
                           x.reshape(bsz * t, d), g2, t)
    return out.reshape(bsz, t, d)


def kernel(x, c, ctx, c_ctx, w_mod, b_mod, norm1_g, norm2_g, w_in, conv_w, w_out, lb_logits,
           peer_wq, peer_subkeys, peer_u, peer_v, final_g):
    bsz, t, d = x.shape
    depth = w_mod.shape[0]

    p_lb = jax.nn.softmax(lb_logits.astype(F32), axis=0)
    lower = jnp.cumsum(p_lb, axis=0) - p_lb[0]

    cond = jnp.zeros((COND_ROWS, d), F32).at[:bsz].set(c).at[bsz].set(c_ctx)
    mod = _mod_call(cond, w_mod, b_mod)
    mod = mod.reshape(depth, COND_ROWS, N_MOD, d)
    mod_x = jnp.transpose(mod[:, :bsz], (0, 2, 1, 3))[:, :, :, None, :]
    mod_c = jnp.broadcast_to(mod[:, bsz][:, :, None, None, :], mod_x.shape)

    w_in_b = w_in.astype(BF16)
    w_out_b = w_out.astype(BF16)
    wq_b = peer_wq.astype(BF16)
    keys_b = peer_subkeys.astype(BF16)
    u_b = peer_u.astype(BF16)
    vt_b = jnp.swapaxes(peer_v, 1, 2).astype(BF16)

    zero_state = jnp.zeros((bsz, HG_HEADS, HG_DIM, HG_DIM), F32)
    xc = ctx
    for l in range(depth):
        lbf = lower[l, 0][None, :]
        lbb = lower[l, 1][None, :]
        full = l < depth - 1
        xc_new, s_f, s_b = _mixer(xc, mod_c[l], l, norm1_g, w_in_b, conv_w, w_out_b, lbf, lbb,
                                  zero_state, zero_state, False, full)
        if full:
            xc = _peer(xc_new, mod_c[l], l, norm2_g, wq_b, keys_b, u_b, vt_b)
        x, _, _ = _mixer(x, mod_x[l], l, norm1_g, w_in_b, conv_w, w_out_b, lbf, lbb,
                         s_f, s_b, True, True)
        x = _peer(x, mod_x[l], l, norm2_g, wq_b, keys_b, u_b, vt_b)
    return _final_norm_call(x.reshape(bsz * t, d), final_g[None, :]).reshape(bsz, t, d)
```

```python
import functools
import math

import numpy as np
import jax
import jax.numpy as jnp
from jax import lax
from jax.experimental import pallas as pl
from jax.experimental.pallas import tpu as pltpu

F32 = jnp.float32
BF16 = jnp.bfloat16

D_MODEL = 1024
GRID_W = 64
EPS = 1e-6
F_FLOOR = 1e-20
N_MOD = 6
HG_WIDTH = 512
HG_HEADS = 4
HG_DIM = HG_WIDTH // HG_HEADS
SC_WIDTH = 512
SC_HALF = SC_WIDTH // 2
IN_COLS = 5 * HG_WIDTH + 3 * SC_WIDTH
PEER_HEADS = 8
PEER_QDIM = 256
PEER_HALF = PEER_QDIM // 2
N_KEYS = 128
N_EXPERTS = N_KEYS * N_KEYS
PEER_TOPK = 16

SUBLANES = 8
LANES = 128
VMEM_LIMIT = 48 * 1024 * 1024

GLA_CHUNK = 128
GLA_LEVELS = (64, 32, 16, 8)
COND_ROWS = 8

COL_IV, COL_ZF, COL_ZB, COL_Q, COL_G, COL_CG, COL_BG, COL_HV = range(8)


def _params(*sem):
    return pltpu.CompilerParams(dimension_semantics=sem, vmem_limit_bytes=VMEM_LIMIT)


def _dot(a, b):
    return jnp.dot(a.astype(BF16), b.astype(BF16), preferred_element_type=F32)


def _dot_nt(a, b):
    return lax.dot_general(a.astype(BF16), b.astype(BF16), (((1,), (1,)), ((), ())),
                           preferred_element_type=F32)


def _dot_tn(a, b):
    return lax.dot_general(a.astype(BF16), b.astype(BF16), (((0,), (0,)), ((), ())),
                           preferred_element_type=F32)


def _mod_kernel(cond_ref, w_ref, b_ref, o_ref):
    c = cond_ref[...]
    s = c * jax.nn.sigmoid(c)
    o_ref[0] = _dot(s, w_ref[0]) + b_ref[0]


def _mod_call(cond, w_mod, b_mod):
    depth, d, n = w_mod.shape
    tn = 1536
    return pl.pallas_call(
        _mod_kernel,
        grid=(depth, n // tn),
        in_specs=[
            pl.BlockSpec((COND_ROWS, d), lambda l, j: (0, 0)),
            pl.BlockSpec((1, d, tn), lambda l, j: (l, 0, j)),
            pl.BlockSpec((1, 1, tn), lambda l, j: (l, 0, j)),
        ],
        out_specs=pl.BlockSpec((1, COND_ROWS, tn), lambda l, j: (l, 0, j)),
        out_shape=jax.ShapeDtypeStruct((depth, COND_ROWS, n), F32),
        compiler_params=_params("parallel", "parallel"),
        name="mod",
    )(cond, w_mod, b_mod.reshape(depth, 1, n))


def _norm_mod(x, g, scale, shift):
    ms = jnp.mean(x * x, axis=-1, keepdims=True)
    y = x * lax.rsqrt(ms + EPS) * g
    return y * (1.0 + scale) + shift


def _proj_kernel(x_ref, g_ref, sc_ref, sh_ref, w_ref, o_ref):
    h = _norm_mod(x_ref[0], g_ref[...], sc_ref[0], sh_ref[0])
    o_ref[0] = _dot(h, w_ref[...])


def _proj_call(x, g, scale, shift, w):
    bsz, t, d = x.shape
    n = w.shape[1]
    tm = min(256, t)
    return pl.pallas_call(
        _proj_kernel,
        grid=(bsz, t // tm),
        in_specs=[
            pl.BlockSpec((1, tm, d), lambda b, i: (b, i, 0)),
            pl.BlockSpec((1, d), lambda b, i: (0, 0)),
            pl.BlockSpec((1, 1, d), lambda b, i: (b, 0, 0)),
            pl.BlockSpec((1, 1, d), lambda b, i: (b, 0, 0)),
            pl.BlockSpec((d, n), lambda b, i: (0, 0)),
        ],
        out_specs=pl.BlockSpec((1, tm, n), lambda b, i: (b, i, 0)),
        out_shape=jax.ShapeDtypeStruct((bsz, t, n), F32),
        compiler_params=_params("parallel", "parallel"),
        name="proj",
    )(x, g, scale, shift, w)


def _gla_sum_matrix(backward):
    c = GLA_CHUNK
    t = np.arange(c)[:, None]
    s = np.arange(c)[None, :]
    mats = []
    if not backward:
        mats.append(s <= t)
        mats.append(s > t)
    else:
        mats.append(s >= t)
        mats.append(s < t)
    for n in GLA_LEVELS:
        mid = (t // (2 * n)) * (2 * n) + n
        if not backward:
            m = np.where(t >= mid, (s >= mid) & (s <= t), (s > t) & (s < mid))
        else:
            m = np.where(t < mid, (s >= t) & (s < mid), (s >= mid) & (s < t))
        mats.append(m)
    return np.concatenate([m.astype(np.float32) for m in mats], axis=0)


def _gla_masks(backward):
    c = GLA_CHUNK
    row = lax.broadcasted_iota(jnp.int32, (c, c), 0)
    col = lax.broadcasted_iota(jnp.int32, (c, c), 1)
    qrow, lvl = [], []
    for n in GLA_LEVELS:
        r_hi = (row & (2 * n - 1)) >= n
        c_hi = (col & (2 * n - 1)) >= n
        same = (row & ~(2 * n - 1)) == (col & ~(2 * n - 1))
        if not backward:
            qrow.append(r_hi)
            lvl.append(same & r_hi & jnp.logical_not(c_hi))
        else:
            qrow.append(jnp.logical_not(r_hi))
            lvl.append(same & jnp.logical_not(r_hi) & c_hi)
    same8 = (row & ~(SUBLANES - 1)) == (col & ~(SUBLANES - 1))
    order = (col >= row) if backward else (col <= row)
    pair = [same8 & order & ((col & (SUBLANES - 1)) == i) for i in range(SUBLANES)]
    return qrow, lvl, pair


def _bcast_group_row(x, i):
    c, w = x.shape
    x3 = x.reshape(c // SUBLANES, SUBLANES, w)
    return jnp.broadcast_to(x3[:, i:i + 1, :], x3.shape).reshape(c, w)


def _gla_chain(z, q, v, lb, st, msum, masks, ones, backward):
    c = GLA_CHUNK
    qrow, lvl, pair = masks
    sig = jax.nn.sigmoid(z)
    f = lb + (1.0 - lb) * sig
    logf = jnp.log(jnp.maximum(f, F_FLOOR))
    kk = (1.0 - lb) * jax.nn.sigmoid(-z)
    hi = logf.astype(BF16)
    lo = (logf - hi.astype(F32)).astype(BF16)
    sums = (jnp.dot(msum, hi, preferred_element_type=F32)
            + jnp.dot(msum, lo, preferred_element_type=F32))
    b = sums[0:c]
    e_out = sums[c:2 * c]

    att = jnp.zeros((c, c), F32)
    for li in range(len(GLA_LEVELS)):
        scale = jnp.exp(sums[(2 + li) * c:(3 + li) * c])
        mixed = (jnp.where(qrow[li][:, :HG_DIM], q, kk) * scale).astype(BF16)
        a = lax.dot_general(mixed, mixed, (((1,), (1,)), ((), ())), preferred_element_type=F32)
        att = jnp.where(lvl[li], a, att)
    for i in range(SUBLANES):
        p = q * _bcast_group_row(kk, i) * jnp.exp(-jnp.abs(b - _bcast_group_row(b, i)))
        w = jnp.dot(p.astype(BF16), ones, preferred_element_type=F32)
        att = jnp.where(pair[i], w, att)

    q_in = q * jnp.exp(b)
    o = _dot(att, v) + _dot_nt(q_in, st)
    k_out = kk * jnp.exp(e_out)
    d = jnp.exp(b[0:1, :] if backward else b[c - 1:c, :])
    st_new = d * st + _dot_tn(v, k_out)
    return o, st_new


def _gla_kernel(ivf_ref, zf_ref, qf_ref, ivb_ref, zb_ref, qb_ref, lbf_ref, lbb_ref,
                s0f_ref, s0b_ref, mf_ref, mb_ref,
                of_ref, ob_ref, sf_ref, sb_ref, st_ref):
    n = pl.program_id(1)

    @pl.when(n == 0)
    def _():
        st_ref[0] = s0f_ref[0]
        st_ref[1] = s0b_ref[0]

    ones = jnp.ones((HG_DIM, HG_DIM), BF16)
    masks_f = _gla_masks(False)
    masks_b = _gla_masks(True)
    for h in range(HG_HEADS):
        sl = slice(h * HG_DIM, (h + 1) * HG_DIM)
        o, st = _gla_chain(zf_ref[0, :, sl], qf_ref[0, :, sl], ivf_ref[0, :, sl], lbf_ref[:, sl],
                           st_ref[0, h], mf_ref[...], masks_f, ones, False)
        of_ref[0, :, sl] = o
        st_ref[0, h] = st
        o, st = _gla_chain(zb_ref[0, :, sl], qb_ref[0, :, sl], ivb_ref[0, :, sl], lbb_ref[:, sl],
                           st_ref[1, h], mb_ref[...], masks_b, ones, True)
        ob_ref[0, :, sl] = o
        st_ref[1, h] = st

    @pl.when(n == pl.num_programs(1) - 1)
    def _():
        sf_ref[0] = st_ref[0]
        sb_ref[0] = st_ref[1]


def _gla_call(proj, lbf, lbb, s0f, s0b):
    bsz, t, _ = proj.shape
    c = GLA_CHUNK
    nc = t // c
    w = HG_WIDTH
    msum_f = jnp.asarray(_gla_sum_matrix(False), BF16)
    msum_b = jnp.asarray(_gla_sum_matrix(True), BF16)

    def fwd(col):
        return pl.BlockSpec((1, c, w), lambda b, n: (b, n, col))

    def bwd(col):
        return pl.BlockSpec((1, c, w), lambda b, n: (b, nc - 1 - n, col))

    state_spec = pl.BlockSpec((1, HG_HEADS, HG_DIM, HG_DIM), lambda b, n: (b, 0, 0, 0))
    const = lambda shape: pl.BlockSpec(shape, lambda b, n: (0,) * len(shape))
    return pl.pallas_call(
        _gla_kernel,
        grid=(bsz, nc),
        in_specs=[fwd(COL_IV), fwd(COL_ZF), fwd(COL_Q), bwd(COL_IV), bwd(COL_ZB), bwd(COL_Q),
                  const((1, w)), const((1, w)), state_spec, state_spec,
                  const(msum_f.shape), const(msum_b.shape)],
        out_specs=[pl.BlockSpec((1, c, w), lambda b, n: (b, n, 0)),
                   pl.BlockSpec((1, c, w), lambda b, n: (b, nc - 1 - n, 0)),
                   state_spec, state_spec],
        out_shape=[jax.ShapeDtypeStruct((bsz, t, w), F32),
                   jax.ShapeDtypeStruct((bsz, t, w), F32),
                   jax.ShapeDtypeStruct((bsz, HG_HEADS, HG_DIM, HG_DIM), F32),
                   jax.ShapeDtypeStruct((bsz, HG_HEADS, HG_DIM, HG_DIM), F32)],
        scratch_shapes=[pltpu.VMEM((2, HG_HEADS, HG_DIM, HG_DIM), F32)],
        compiler_params=_params("parallel", "arbitrary"),
        name="gla",
    )(proj, proj, proj, proj, proj, proj, lbf, lbb, s0f, s0b, msum_f, msum_b)


def _mixout_body(of_ref, ob_ref, g_ref, cg_ref, bg_ref, hv_ref, halo, cw_ref, wo_ref,
                 x_ref, g1_ref, o_ref, grid_conv):
    tm = of_ref.shape[1]
    o = of_ref[0] + ob_ref[0]
    heads = []
    for h in range(HG_HEADS):
        oh = o[:, h * HG_DIM:(h + 1) * HG_DIM]
        heads.append(oh * lax.rsqrt(jnp.mean(oh * oh, axis=-1, keepdims=True) + EPS))
    g = g_ref[0]
    o_rec = jnp.concatenate(heads, axis=-1) * (g * jax.nn.sigmoid(g))

    u = cg_ref[0] * hv_ref[0]
    w0, w1, w2 = cw_ref[0:1, :], cw_ref[1:2, :], cw_ref[2:3, :]
    pos = lax.broadcasted_iota(jnp.int32, (tm, 1), 0)
    if grid_conv:
        cgp_ref, hvp_ref, cgn_ref, hvn_ref = halo
        i = pl.program_id(1)
        last = pl.num_programs(1) - 1
        colpos = pos & (GRID_W - 1)
        uh = u[:, :SC_HALF]
        left = jnp.where(colpos == 0, 0.0, pltpu.roll(uh, 1, 0))
        right = jnp.where(colpos == GRID_W - 1, 0.0, pltpu.roll(uh, tm - 1, 0))
        conv_h = w0[:, :SC_HALF] * left + w1[:, :SC_HALF] * uh + w2[:, :SC_HALF] * right
        uv = u[:, SC_HALF:]
        up_halo = jnp.where(i == 0, 0.0, cgp_ref[0][:, SC_HALF:] * hvp_ref[0][:, SC_HALF:])
        dn_halo = jnp.where(i == last, 0.0, cgn_ref[0][:, SC_HALF:] * hvn_ref[0][:, SC_HALF:])
        up = jnp.concatenate([up_halo, uv[:tm - GRID_W]], axis=0)
        down = jnp.concatenate([uv[GRID_W:], dn_halo], axis=0)
        conv_v = w0[:, SC_HALF:] * up + w1[:, SC_HALF:] * uv + w2[:, SC_HALF:] * down
        conv = jnp.concatenate([conv_h, conv_v], axis=-1)
    else:
        left = jnp.where(pos == 0, 0.0, pltpu.roll(u, 1, 0))
        right = jnp.where(pos == tm - 1, 0.0, pltpu.roll(u, tm - 1, 0))
        conv = w0 * left + w1 * u + w2 * right
    o_conv = bg_ref[0] * conv
    y = _dot(o_rec, wo_ref[0:HG_WIDTH, :]) + _dot(o_conv, wo_ref[HG_WIDTH:, :])
    o_ref[0] = x_ref[0] + g1_ref[0] * y


def _mixout_grid_kernel(of_ref, ob_ref, g_ref, cg_ref, bg_ref, hv_ref, cgp_ref, hvp_ref,
                        cgn_ref, hvn_ref, cw_ref, wo_ref, x_ref, g1_ref, o_ref):
    _mixout_body(of_ref, ob_ref, g_ref, cg_ref, bg_ref, hv_ref,
                 (cgp_ref, hvp_ref, cgn_ref, hvn_ref), cw_ref, wo_ref, x_ref, g1_ref, o_ref, True)


def _mixout_seq_kernel(of_ref, ob_ref, g_ref, cg_ref, bg_ref, hv_ref, cw_ref, wo_ref,
                       x_ref, g1_ref, o_ref):
    _mixout_body(of_ref, ob_ref, g_ref, cg_ref, bg_ref, hv_ref, None, cw_ref, wo_ref,
                 x_ref, g1_ref, o_ref, False)


def _mixout_call(o_f, o_b, proj, conv_w, w_out, x, g1, grid_conv):
    bsz, t, d = x.shape
    w = HG_WIDTH
    tm = 512 if grid_conv else t
    nt = t // tm
    hb = tm // GRID_W
    nhalo = t // GRID_W

    def col(cidx):
        return pl.BlockSpec((1, tm, w), lambda b, i: (b, i, cidx))

    def prev(cidx):
        return pl.BlockSpec((1, GRID_W, w), lambda b, i: (b, jnp.maximum(i * hb - 1, 0), cidx))

    def nxt(cidx):
        return pl.BlockSpec((1, GRID_W, w),
                            lambda b, i: (b, jnp.minimum((i + 1) * hb, nhalo - 1), cidx))

    row = pl.BlockSpec((1, tm, w), lambda b, i: (b, i, 0))
    in_specs = [row, row, col(COL_G), col(COL_CG), col(COL_BG), col(COL_HV)]
    args = [o_f, o_b, proj, proj, proj, proj]
    if grid_conv:
        in_specs += [prev(COL_CG), prev(COL_HV), nxt(COL_CG), nxt(COL_HV)]
        args += [proj, proj, proj, proj]
    in_specs += [
        pl.BlockSpec((3, w), lambda b, i: (0, 0)),
        pl.BlockSpec(w_out.shape, lambda b, i: (0, 0)),
        pl.BlockSpec((1, tm, d), lambda b, i: (b, i, 0)),
        pl.BlockSpec((1, 1, d), lambda b, i: (b, 0, 0)),
    ]
    args += [conv_w, w_out, x, g1]
    return pl.pallas_call(
        _mixout_grid_kernel if grid_conv else _mixout_seq_kernel,
        grid=(bsz, nt),
        in_specs=in_specs,
        out_specs=pl.BlockSpec((1, tm, d), lambda b, i: (b, i, 0)),
        out_shape=jax.ShapeDtypeStruct((bsz, t, d), F32),
        compiler_params=_params("parallel", "parallel"),
        name="mixout_grid" if grid_conv else "mixout_seq",
    )(*args)


def _oddeven_merge_sort_pairs(n):
    pairs = []
    p = 1
    while p < n:
        k = p
        while k >= 1:
            for j in range(k % p, n - k, 2 * k):
                for i in range(min(k, n - j - k)):
                    if (i + j) // (2 * p) == (i + j + k) // (2 * p):
                        pairs.append((i + j, i + j + k))
            k //= 2
        p *= 2
    return pairs


_SORT16 = _oddeven_merge_sort_pairs(PEER_TOPK)


def _sort_desc(xs):
    xs = list(xs)
    for i, j in _SORT16:
        a, b = xs[i], xs[j]
        xs[i], xs[j] = jnp.maximum(a, b), jnp.minimum(a, b)
    return xs


def _bitonic_merge_desc(xs):
    xs = list(xs)
    d = len(xs) // 2
    while d >= 1:
        for i in range(len(xs)):
            if i & d == 0:
                a, b = xs[i], xs[i + d]
                xs[i], xs[i + d] = jnp.maximum(a, b), jnp.minimum(a, b)
        d //= 2
    return xs


def _top_of_union(a, b):
    k = len(a)
    return [jnp.maximum(a[r], b[k - 1 - r]) for r in range(k)]


def _topk_rows(s):
    rows = [s[SUBLANES * v:SUBLANES * (v + 1), :] for v in range(N_KEYS // SUBLANES)]
    rows = _sort_desc(rows)
    for shift in (4, 2, 1):
        rolled = [pltpu.roll(r, shift, 0) for r in rows]
        rows = _bitonic_merge_desc(_top_of_union(rows, rolled))
    return rows


def _peer_prep_kernel(x_ref, g_ref, sc_ref, sh_ref, wq_ref, keys_ref,
                      hb_ref, t1_ref, e1_ref, s2_ref, e2_ref, s1_ref):
    tq = x_ref.shape[1]
    h = _norm_mod(x_ref[0], g_ref[...], sc_ref[0], sh_ref[0]).astype(BF16)
    hb_ref[0] = h
    qf = jnp.dot(h, wq_ref[...], preferred_element_type=F32)
    sub = lax.broadcasted_iota(jnp.int32, (SUBLANES, tq), 0)
    neg = jnp.full((SUBLANES, tq), -jnp.inf, F32)
    pos_inf = jnp.full((SUBLANES, tq), jnp.inf, F32)
    packed = [[jnp.zeros((SUBLANES, tq), F32)] * PEER_TOPK for _ in range(2)]
    for hd in range(PEER_HEADS):
        for p in range(2):
            lo = hd * PEER_QDIM + p * PEER_HALF
            s = _dot_nt(keys_ref[hd, p], qf[:, lo:lo + PEER_HALF])
            (s1_ref if p == 0 else s2_ref)[hd] = s
            top = _topk_rows(s)
            packed[p] = [jnp.where(sub == hd, top[r], packed[p][r]) for r in range(PEER_TOPK)]
    c1, c2 = packed
    pairs = [(a, b) for a in range(PEER_TOPK) for b in range(PEER_TOPK)
             if (a + 1) * (b + 1) <= PEER_TOPK]
    cand = {ab: c1[ab[0]] + c2[ab[1]] for ab in pairs}
    cands = [cand[ab] for ab in pairs]
    cands += [neg] * (-len(cands) % PEER_TOPK)
    groups = [_sort_desc(cands[i:i + PEER_TOPK]) for i in range(0, len(cands), PEER_TOPK)]
    while len(groups) > 2:
        nxt = [_bitonic_merge_desc(_top_of_union(groups[i], groups[i + 1]))
               for i in range(0, len(groups) - 1, 2)]
        if len(groups) % 2:
            nxt.append(groups[-1])
        groups = nxt
    top = _top_of_union(groups[0], groups[1]) if len(groups) == 2 else groups[0]
    tau = functools.reduce(jnp.minimum, top)
    m1, m2 = c1[0], c2[0]
    mx = m1 + m2
    z = functools.reduce(lambda a, b: a + b, [jnp.exp(t - mx) for t in top])
    inv_z = 1.0 / z
    t1s = []
    for a in range(PEER_TOPK):
        t = pos_inf
        for b in range(PEER_TOPK):
            if (a, b) in cand:
                t = jnp.minimum(t, jnp.where(cand[(a, b)] >= tau, c2[b], jnp.inf))
        t1s.append(t)
    grouped = (N_KEYS // SUBLANES, SUBLANES, tq)
    for hd in range(PEER_HEADS):
        s1 = s1_ref[hd]
        t1 = jnp.full((N_KEYS, tq), jnp.inf, F32)
        for a in range(PEER_TOPK):
            t1 = jnp.where(s1 == c1[a][hd:hd + 1, :], t1s[a][hd:hd + 1, :], t1)
        t1_ref[hd] = t1.reshape(grouped)
        e1 = jnp.exp(s1 - m1[hd:hd + 1, :]) * inv_z[hd:hd + 1, :]
        e1_ref[hd] = e1.reshape(grouped)
        e2_ref[hd] = jnp.exp(s2_ref[hd] - m2[hd:hd + 1, :])


def _peer_prep_call(x, g, scale, shift, wq, keys):
    bsz, t, d = x.shape
    tq = min(256, t)
    nt = t // tq
    ttot = bsz * t
    score_spec = pl.BlockSpec((PEER_HEADS, N_KEYS, tq), lambda b, i: (0, 0, b * nt + i))
    score_shape = jax.ShapeDtypeStruct((PEER_HEADS, N_KEYS, ttot), F32)
    n_grp = N_KEYS // SUBLANES
    grouped_spec = pl.BlockSpec((PEER_HEADS, n_grp, SUBLANES, tq), lambda b, i: (0, 0, 0, b * nt + i))
    grouped_shape = jax.ShapeDtypeStruct((PEER_HEADS, n_grp, SUBLANES, ttot), F32)
    return pl.pallas_call(
        _peer_prep_kernel,
        grid=(bsz, nt),
        in_specs=[
            pl.BlockSpec((1, tq, d), lambda b, i: (b, i, 0)),
            pl.BlockSpec((1, d), lambda b, i: (0, 0)),
            pl.BlockSpec((1, 1, d), lambda b, i: (b, 0, 0)),
            pl.BlockSpec((1, 1, d), lambda b, i: (b, 0, 0)),
            pl.BlockSpec(wq.shape, lambda b, i: (0, 0)),
            pl.BlockSpec(keys.shape, lambda b, i: (0, 0, 0, 0)),
        ],
        out_specs=[pl.BlockSpec((1, tq, d), lambda b, i: (b, i, 0)),
                   grouped_spec, grouped_spec, score_spec, score_spec],
        out_shape=[jax.ShapeDtypeStruct((bsz, t, d), BF16),
                   grouped_shape, grouped_shape, score_shape, score_shape],
        scratch_shapes=[pltpu.VMEM((PEER_HEADS, N_KEYS, tq), F32)],
        compiler_params=_params("parallel", "parallel"),
        name="peer_prep",
    )(x, g, scale, shift, wq, keys)


PEER_TE = SUBLANES * N_KEYS
SQRT_HALF = math.sqrt(0.5)


def _peer_dense_kernel(hb_ref, u_ref, vt_ref, t1_ref, e1_ref, s2_ref, e2_ref,
                       x_ref, g2_ref, o_ref, acc_ref, a_ref, gt_ref, *, n_e, n_steps):
    s = pl.program_id(0)
    tt = hb_ref.shape[0]
    n_lane_tiles = tt // LANES
    n_i2b = N_KEYS // SUBLANES

    @pl.when(s == 0)
    def _():
        a_ref[...] = jnp.zeros_like(a_ref)
        gt_ref[...] = jnp.zeros_like(gt_ref)

    @pl.when(jnp.logical_or(s == 0, (s - 2) % n_e == 0))
    def _():
        acc_ref[...] = jnp.zeros_like(acc_ref)

    acc_ref[...] += jnp.dot(vt_ref[...], gt_ref[...], preferred_element_type=F32)

    grp = jnp.clip(s - 1, 0, n_steps - 1) % n_e
    for il in range(SUBLANES):
        for j in range(n_lane_tiles):
            lanes = pl.ds(j * LANES, LANES)
            acc = [jnp.zeros((SUBLANES, LANES), F32)] * n_i2b
            for hd in range(PEER_HEADS):
                t1b = jnp.broadcast_to(t1_ref[hd, grp, pl.ds(il, 1), lanes], (SUBLANES, LANES))
                e1b = jnp.broadcast_to(e1_ref[hd, grp, pl.ds(il, 1), lanes], (SUBLANES, LANES))
                for k in range(n_i2b):
                    rows = pl.ds(k * SUBLANES, SUBLANES)
                    sel = s2_ref[hd, rows, lanes] >= t1b
                    acc[k] = acc[k] + jnp.where(sel, e2_ref[hd, rows, lanes], 0.0) * e1b
            for k in range(0, n_i2b, 2):
                rows = pl.ds(il * N_KEYS + k * SUBLANES, 2 * SUBLANES)
                a = a_ref[rows, lanes]
                gelu = 0.5 * a * (1.0 + lax.erf(a * SQRT_HALF))
                w = jnp.concatenate([acc[k], acc[k + 1]], axis=0)
                gt_ref[rows, lanes] = (w * gelu).astype(BF16)

    a_ref[...] = lax.dot_general(u_ref[...], hb_ref[...], (((1,), (1,)), ((), ())),
                                 preferred_element_type=F32)

    @pl.when(jnp.logical_and(s >= 2, (s - 2) % n_e == n_e - 1))
    def _():
        o_ref[...] = x_ref[...] + g2_ref[0] * acc_ref[...].T


def _peer_dense_call(hb, u, vt, t1, e1, s2, e2, x, g2, t_per_batch):
    ttot, d = x.shape
    tt = min(512, t_per_batch)
    n_e = N_EXPERTS // PEER_TE
    n_steps = (ttot // tt) * n_e
    per_batch = t_per_batch // tt
    tile = lambda s, lag: jnp.clip(s - lag, 0, n_steps - 1) // n_e
    group = lambda s, lag: jnp.clip(s - lag, 0, n_steps - 1) % n_e
    score_spec = pl.BlockSpec((PEER_HEADS, N_KEYS, tt), lambda s: (0, 0, tile(s, 1)))
    grouped_spec = pl.BlockSpec((PEER_HEADS, N_KEYS // SUBLANES, SUBLANES, tt),
                                lambda s: (0, 0, 0, tile(s, 1)))
    return pl.pallas_call(
        functools.partial(_peer_dense_kernel, n_e=n_e, n_steps=n_steps),
        grid=(n_steps + 2,),
        in_specs=[
            pl.BlockSpec((tt, d), lambda s: (tile(s, 0), 0)),
            pl.BlockSpec((PEER_TE, d), lambda s: (group(s, 0), 0)),
            pl.BlockSpec((d, PEER_TE), lambda s: (0, group(s, 2))),
            grouped_spec, grouped_spec, score_spec, score_spec,
            pl.BlockSpec((tt, d), lambda s: (tile(s, 2), 0)),
            pl.BlockSpec((1, 1, d), lambda s: (tile(s, 2) // per_batch, 0, 0)),
        ],
        out_specs=pl.BlockSpec((tt, d), lambda s: (tile(s, 2), 0)),
        out_shape=jax.ShapeDtypeStruct((ttot, d), F32),
        scratch_shapes=[pltpu.VMEM((d, tt), F32),
                        pltpu.VMEM((PEER_TE, tt), F32),
                        pltpu.VMEM((PEER_TE, tt), BF16)],
        compiler_params=_params("arbitrary"),
        name="peer_dense",
    )(hb, u, vt, t1, e1, s2, e2, x, g2)


def _final_norm_kernel(x_ref, g_ref, o_ref):
    x = x_ref[...]
    ms = jnp.mean(x * x, axis=-1, keepdims=True)
    o_ref[...] = x * lax.rsqrt(ms + EPS) * g_ref[...]


def _final_norm_call(x, g):
    n, d = x.shape
    tm = 1024
    return pl.pallas_call(
        _final_norm_kernel,
        grid=(n // tm,),
        in_specs=[pl.BlockSpec((tm, d), lambda i: (i, 0)), pl.BlockSpec((1, d), lambda i: (0, 0))],
        out_specs=pl.BlockSpec((tm, d), lambda i: (i, 0)),
        out_shape=jax.ShapeDtypeStruct((n, d), F32),
        compiler_params=_params("parallel"),
        name="final_norm",
    )(x, g)


def _mixer(x, mod, l, norm1_g, w_in, conv_w, w_out, lbf, lbb, s0f, s0b, grid_conv, full):
    sh1, sc1, g1 = mod[0], mod[1], mod[2]
    proj = _proj_call(x, norm1_g[l][None, :], sc1, sh1, w_in[l])
    o_f, o_b, s_f, s_b = _gla_call(proj, lbf, lbb, s0f, s0b)
    if not full:
        return None, s_f, s_b
    x = _mixout_call(o_f, o_b, proj, conv_w[l], w_out[l], x, g1, grid_conv)
    return x, s_f, s_b


def _peer(x, mod, l, norm2_g, wq, keys, u, vt):
    bsz, t, d = x.shape
    sh2, sc2, g2 = mod[3], mod[4], mod[5]
    hb, t1, e1, s2, e2 = _peer_prep_call(x, norm2_g[l][None, :], sc2, sh2, wq[l], keys[l])
    out = _peer_dense_call(hb.reshape(bsz * t, d), u[l], vt[l], t1, e1, s2, e2,
                           x.reshape(bsz * t, d), g2, t)
    return out.reshape(bsz, t, d)


def kernel(x, c, ctx, c_ctx, w_mod, b_mod, norm1_g, norm2_g, w_in, conv_w, w_out, lb_logits,
           peer_wq, peer_subkeys, peer_u, peer_v, final_g):
    bsz, t, d = x.shape
    depth = w_mod.shape[0]

    p_lb = jax.nn.softmax(lb_logits.astype(F32), axis=0)
    lower = jnp.cumsum(p_lb, axis=0) - p_lb[0]

    cond = jnp.zeros((COND_ROWS, d), F32).at[:bsz].set(c).at[bsz].set(c_ctx)
    mod = _mod_call(cond, w_mod, b_mod)
    mod = mod.reshape(depth, COND_ROWS, N_MOD, d)
    mod_x = jnp.transpose(mod[:, :bsz], (0, 2, 1, 3))[:, :, :, None, :]
    mod_c = jnp.broadcast_to(mod[:, bsz][:, :, None, None, :], mod_x.shape)

    w_in_b = w_in.astype(BF16)
    w_out_b = w_out.astype(BF16)
    wq_b = peer_wq.astype(BF16)
    keys_b = peer_subkeys.astype(BF16)
    u_b = peer_u.astype(BF16)
    vt_b = jnp.swapaxes(peer_v, 1, 2).astype(BF16)

    zero_state = jnp.zeros((bsz, HG_HEADS, HG_DIM, HG_DIM), F32)
    xc = ctx
    for l in range(depth):
        lbf = lower[l, 0][None, :]
        lbb = lower[l, 1][None, :]
        full = l < depth - 1
        xc_new, s_f, s_b = _mixer(xc, mod_c[l], l, norm1_g, w_in_b, conv_w, w_out_b, lbf, lbb,
                                  zero_state, zero_state, False, full)
        if full:
            xc = _peer(xc_new, mod_c[l], l, norm2_g, wq_b, keys_b, u_b, vt_b)
        x, _, _ = _mixer(x, mod_x[l], l, norm1_g, w_in_b, conv_w, w_out_b, lbf, lbb,
                         s_f, s_b, True, True)
        x = _peer(x, mod_x[l], l, norm2_g, wq_b, keys_b, u_b, vt_b)
    return _final_norm_call(x.reshape(bsz * t, d), final_g[None, :]).reshape(bsz, t, d)
```

```python
import functools
import math

import numpy as np
import jax
import jax.numpy as jnp
from jax import lax
from jax.experimental import pallas as pl
from jax.experimental.pallas import tpu as pltpu

F32 = jnp.float32
BF16 = jnp.bfloat16

D_MODEL = 1024
GRID_W = 64
EPS = 1e-6
F_FLOOR = 1e-20
N_MOD = 6
HG_WIDTH = 512
HG_HEADS = 4
HG_DIM = HG_WIDTH // HG_HEADS
SC_WIDTH = 512
SC_HALF = SC_WIDTH // 2
IN_COLS = 5 * HG_WIDTH + 3 * SC_WIDTH
PEER_HEADS = 8
PEER_QDIM = 256
PEER_HALF = PEER_QDIM // 2
N_KEYS = 128
N_EXPERTS = N_KEYS * N_KEYS
PEER_TOPK = 16

SUBLANES = 8
LANES = 128
PACKED_ROWS = 2 * SUBLANES
VMEM_LIMIT = 48 * 1024 * 1024

GLA_CHUNK = 128
GLA_LEVELS = (64, 32, 16, 8)
COND_ROWS = 8

COL_IV, COL_ZF, COL_ZB, COL_Q, COL_G, COL_CG, COL_BG, COL_HV = range(8)


def _params(*sem):
    return pltpu.CompilerParams(dimension_semantics=sem, vmem_limit_bytes=VMEM_LIMIT)


def _dot(a, b):
    return jnp.dot(a.astype(BF16), b.astype(BF16), preferred_element_type=F32)


def _dot_nt(a, b):
    return lax.dot_general(a.astype(BF16), b.astype(BF16), (((1,), (1,)), ((), ())),
                           preferred_element_type=F32)


def _dot_tn(a, b):
    return lax.dot_general(a.astype(BF16), b.astype(BF16), (((0,), (0,)), ((), ())),
                           preferred_element_type=F32)


def _mod_kernel(cond_ref, w_ref, b_ref, o_ref):
    c = cond_ref[...]
    s = c * jax.nn.sigmoid(c)
    o_ref[0] = _dot(s, w_ref[0]) + b_ref[0]


def _mod_call(cond, w_mod, b_mod):
    depth, d, n = w_mod.shape
    tn = 1536
    return pl.pallas_call(
        _mod_kernel,
        grid=(depth, n // tn),
        in_specs=[
            pl.BlockSpec((COND_ROWS, d), lambda l, j: (0, 0)),
            pl.BlockSpec((1, d, tn), lambda l, j: (l, 0, j)),
            pl.BlockSpec((1, 1, tn), lambda l, j: (l, 0, j)),
        ],
        out_specs=pl.BlockSpec((1, COND_ROWS, tn), lambda l, j: (l, 0, j)),
        out_shape=jax.ShapeDtypeStruct((depth, COND_ROWS, n), F32),
        compiler_params=_params("parallel", "parallel"),
        name="mod",
    )(cond, w_mod, b_mod.reshape(depth, 1, n))


def _norm_mod(x, g, scale, shift):
    ms = jnp.mean(x * x, axis=-1, keepdims=True)
    y = x * lax.rsqrt(ms + EPS) * g
    return y * (1.0 + scale) + shift


def _proj_kernel(x_ref, g_ref, sc_ref, sh_ref, w_ref, o_ref):
    h = _norm_mod(x_ref[0], g_ref[...], sc_ref[0], sh_ref[0])
    o_ref[0] = _dot(h, w_ref[...])


def _proj_call(x, g, scale, shift, w):
    bsz, t, d = x.shape
    n = w.shape[1]
    tm = min(256, t)
    return pl.pallas_call(
        _proj_kernel,
        grid=(bsz, t // tm),
        in_specs=[
            pl.BlockSpec((1, tm, d), lambda b, i: (b, i, 0)),
            pl.BlockSpec((1, d), lambda b, i: (0, 0)),
            pl.BlockSpec((1, 1, d), lambda b, i: (b, 0, 0)),
            pl.BlockSpec((1, 1, d), lambda b, i: (b, 0, 0)),
            pl.BlockSpec((d, n), lambda b, i: (0, 0)),
        ],
        out_specs=pl.BlockSpec((1, tm, n), lambda b, i: (b, i, 0)),
        out_shape=jax.ShapeDtypeStruct((bsz, t, n), F32),
        compiler_params=_params("parallel", "parallel"),
        name="proj",
    )(x, g, scale, shift, w)


def _gla_sum_matrix(backward):
    c = GLA_CHUNK
    t = np.arange(c)[:, None]
    s = np.arange(c)[None, :]
    mats = []
    if not backward:
        mats.append(s <= t)
        mats.append(s > t)
    else:
        mats.append(s >= t)
        mats.append(s < t)
    for n in GLA_LEVELS:
        mid = (t // (2 * n)) * (2 * n) + n
        if not backward:
            m = np.where(t >= mid, (s >= mid) & (s <= t), (s > t) & (s < mid))
        else:
            m = np.where(t < mid, (s >= t) & (s < mid), (s >= mid) & (s < t))
        mats.append(m)
    return np.concatenate([m.astype(np.float32) for m in mats], axis=0)


def _gla_masks(backward):
    c = GLA_CHUNK
    row = lax.broadcasted_iota(jnp.int32, (c, c), 0)
    col = lax.broadcasted_iota(jnp.int32, (c, c), 1)
    qrow, lvl = [], []
    for n in GLA_LEVELS:
        r_hi = (row & (2 * n - 1)) >= n
        c_hi = (col & (2 * n - 1)) >= n
        same = (row & ~(2 * n - 1)) == (col & ~(2 * n - 1))
        if not backward:
            qrow.append(r_hi)
            lvl.append(same & r_hi & jnp.logical_not(c_hi))
        else:
            qrow.append(jnp.logical_not(r_hi))
            lvl.append(same & jnp.logical_not(r_hi) & c_hi)
    same8 = (row & ~(SUBLANES - 1)) == (col & ~(SUBLANES - 1))
    order = (col >= row) if backward else (col <= row)
    pair = [same8 & order & ((col & (SUBLANES - 1)) == i) for i in range(SUBLANES)]
    return qrow, lvl, pair


def _bcast_group_row(x, i):
    c, w = x.shape
    x3 = x.reshape(c // SUBLANES, SUBLANES, w)
    return jnp.broadcast_to(x3[:, i:i + 1, :], x3.shape).reshape(c, w)


def _gla_chain(z, q, v, lb, st, msum, masks, ones, backward):
    c = GLA_CHUNK
    qrow, lvl, pair = masks
    sig = jax.nn.sigmoid(z)
    f = lb + (1.0 - lb) * sig
    logf = jnp.log(jnp.maximum(f, F_FLOOR))
    kk = (1.0 - lb) * jax.nn.sigmoid(-z)
    hi = logf.astype(BF16)
    lo = (logf - hi.astype(F32)).astype(BF16)
    sums = (jnp.dot(msum, hi, preferred_element_type=F32)
            + jnp.dot(msum, lo, preferred_element_type=F32))
    b = sums[0:c]
    e_out = sums[c:2 * c]

    att = jnp.zeros((c, c), F32)
    for li in range(len(GLA_LEVELS)):
        scale = jnp.exp(sums[(2 + li) * c:(3 + li) * c])
        mixed = (jnp.where(qrow[li][:, :HG_DIM], q, kk) * scale).astype(BF16)
        a = lax.dot_general(mixed, mixed, (((1,), (1,)), ((), ())), preferred_element_type=F32)
        att = jnp.where(lvl[li], a, att)
    for i in range(SUBLANES):
        p = q * _bcast_group_row(kk, i) * jnp.exp(-jnp.abs(b - _bcast_group_row(b, i)))
        w = jnp.dot(p.astype(BF16), ones, preferred_element_type=F32)
        att = jnp.where(pair[i], w, att)

    q_in = q * jnp.exp(b)
    o = _dot(att, v) + _dot_nt(q_in, st)
    k_out = kk * jnp.exp(e_out)
    d = jnp.exp(b[0:1, :] if backward else b[c - 1:c, :])
    st_new = d * st + _dot_tn(v, k_out)
    return o, st_new


def _gla_kernel(ivf_ref, zf_ref, qf_ref, ivb_ref, zb_ref, qb_ref, lbf_ref, lbb_ref,
                s0f_ref, s0b_ref, mf_ref, mb_ref,
                of_ref, ob_ref, sf_ref, sb_ref, st_ref):
    n = pl.program_id(1)

    @pl.when(n == 0)
    def _():
        st_ref[0] = s0f_ref[0]
        st_ref[1] = s0b_ref[0]

    ones = jnp.ones((HG_DIM, HG_DIM), BF16)
    masks_f = _gla_masks(False)
    masks_b = _gla_masks(True)
    for h in range(HG_HEADS):
        sl = slice(h * HG_DIM, (h + 1) * HG_DIM)
        o, st = _gla_chain(zf_ref[0, :, sl], qf_ref[0, :, sl], ivf_ref[0, :, sl], lbf_ref[:, sl],
                           st_ref[0, h], mf_ref[...], masks_f, ones, False)
        of_ref[0, :, sl] = o
        st_ref[0, h] = st
        o, st = _gla_chain(zb_ref[0, :, sl], qb_ref[0, :, sl], ivb_ref[0, :, sl], lbb_ref[:, sl],
                           st_ref[1, h], mb_ref[...], masks_b, ones, True)
        ob_ref[0, :, sl] = o
        st_ref[1, h] = st

    @pl.when(n == pl.num_programs(1) - 1)
    def _():
        sf_ref[0] = st_ref[0]
        sb_ref[0] = st_ref[1]


def _gla_call(proj, lbf, lbb, s0f, s0b):
    bsz, t, _ = proj.shape
    c = GLA_CHUNK
    nc = t // c
    w = HG_WIDTH
    msum_f = jnp.asarray(_gla_sum_matrix(False), BF16)
    msum_b = jnp.asarray(_gla_sum_matrix(True), BF16)

    def fwd(col):
        return pl.BlockSpec((1, c, w), lambda b, n: (b, n, col))

    def bwd(col):
        return pl.BlockSpec((1, c, w), lambda b, n: (b, nc - 1 - n, col))

    state_spec = pl.BlockSpec((1, HG_HEADS, HG_DIM, HG_DIM), lambda b, n: (b, 0, 0, 0))
    const = lambda shape: pl.BlockSpec(shape, lambda b, n: (0,) * len(shape))
    return pl.pallas_call(
        _gla_kernel,
        grid=(bsz, nc),
        in_specs=[fwd(COL_IV), fwd(COL_ZF), fwd(COL_Q), bwd(COL_IV), bwd(COL_ZB), bwd(COL_Q),
                  const((1, w)), const((1, w)), state_spec, state_spec,
                  const(msum_f.shape), const(msum_b.shape)],
        out_specs=[pl.BlockSpec((1, c, w), lambda b, n: (b, n, 0)),
                   pl.BlockSpec((1, c, w), lambda b, n: (b, nc - 1 - n, 0)),
                   state_spec, state_spec],
        out_shape=[jax.ShapeDtypeStruct((bsz, t, w), F32),
                   jax.ShapeDtypeStruct((bsz, t, w), F32),
                   jax.ShapeDtypeStruct((bsz, HG_HEADS, HG_DIM, HG_DIM), F32),
                   jax.ShapeDtypeStruct((bsz, HG_HEADS, HG_DIM, HG_DIM), F32)],
        scratch_shapes=[pltpu.VMEM((2, HG_HEADS, HG_DIM, HG_DIM), F32)],
        compiler_params=_params("parallel", "arbitrary"),
        name="gla",
    )(proj, proj, proj, proj, proj, proj, lbf, lbb, s0f, s0b, msum_f, msum_b)


def _mixout_body(of_ref, ob_ref, g_ref, cg_ref, bg_ref, hv_ref, halo, cw_ref, wo_ref,
                 x_ref, g1_ref, o_ref, grid_conv):
    tm = of_ref.shape[1]
    o = of_ref[0] + ob_ref[0]
    heads = []
    for h in range(HG_HEADS):
        oh = o[:, h * HG_DIM:(h + 1) * HG_DIM]
        heads.append(oh * lax.rsqrt(jnp.mean(oh * oh, axis=-1, keepdims=True) + EPS))
    g = g_ref[0]
    o_rec = jnp.concatenate(heads, axis=-1) * (g * jax.nn.sigmoid(g))

    u = cg_ref[0] * hv_ref[0]
    w0, w1, w2 = cw_ref[0:1, :], cw_ref[1:2, :], cw_ref[2:3, :]
    pos = lax.broadcasted_iota(jnp.int32, (tm, 1), 0)
    if grid_conv:
        cgp_ref, hvp_ref, cgn_ref, hvn_ref = halo
        i = pl.program_id(1)
        last = pl.num_programs(1) - 1
        colpos = pos & (GRID_W - 1)
        uh = u[:, :SC_HALF]
        left = jnp.where(colpos == 0, 0.0, pltpu.roll(uh, 1, 0))
        right = jnp.where(colpos == GRID_W - 1, 0.0, pltpu.roll(uh, tm - 1, 0))
        conv_h = w0[:, :SC_HALF] * left + w1[:, :SC_HALF] * uh + w2[:, :SC_HALF] * right
        uv = u[:, SC_HALF:]
        up_halo = jnp.where(i == 0, 0.0, cgp_ref[0][:, SC_HALF:] * hvp_ref[0][:, SC_HALF:])
        dn_halo = jnp.where(i == last, 0.0, cgn_ref[0][:, SC_HALF:] * hvn_ref[0][:, SC_HALF:])
        up = jnp.concatenate([up_halo, uv[:tm - GRID_W]], axis=0)
        down = jnp.concatenate([uv[GRID_W:], dn_halo], axis=0)
        conv_v = w0[:, SC_HALF:] * up + w1[:, SC_HALF:] * uv + w2[:, SC_HALF:] * down
        conv = jnp.concatenate([conv_h, conv_v], axis=-1)
    else:
        left = jnp.where(pos == 0, 0.0, pltpu.roll(u, 1, 0))
        right = jnp.where(pos == tm - 1, 0.0, pltpu.roll(u, tm - 1, 0))
        conv = w0 * left + w1 * u + w2 * right
    o_conv = bg_ref[0] * conv
    y = _dot(o_rec, wo_ref[0:HG_WIDTH, :]) + _dot(o_conv, wo_ref[HG_WIDTH:, :])
    o_ref[0] = x_ref[0] + g1_ref[0] * y


def _mixout_grid_kernel(of_ref, ob_ref, g_ref, cg_ref, bg_ref, hv_ref, cgp_ref, hvp_ref,
                        cgn_ref, hvn_ref, cw_ref, wo_ref, x_ref, g1_ref, o_ref):
    _mixout_body(of_ref, ob_ref, g_ref, cg_ref, bg_ref, hv_ref,
                 (cgp_ref, hvp_ref, cgn_ref, hvn_ref), cw_ref, wo_ref, x_ref, g1_ref, o_ref, True)


def _mixout_seq_kernel(of_ref, ob_ref, g_ref, cg_ref, bg_ref, hv_ref, cw_ref, wo_ref,
                       x_ref, g1_ref, o_ref):
    _mixout_body(of_ref, ob_ref, g_ref, cg_ref, bg_ref, hv_ref, None, cw_ref, wo_ref,
                 x_ref, g1_ref, o_ref, False)


def _mixout_call(o_f, o_b, proj, conv_w, w_out, x, g1, grid_conv):
    bsz, t, d = x.shape
    w = HG_WIDTH
    tm = 512 if grid_conv else t
    nt = t // tm
    hb = tm // GRID_W
    nhalo = t // GRID_W

    def col(cidx):
        return pl.BlockSpec((1, tm, w), lambda b, i: (b, i, cidx))

    def prev(cidx):
        return pl.BlockSpec((1, GRID_W, w), lambda b, i: (b, jnp.maximum(i * hb - 1, 0), cidx))

    def nxt(cidx):
        return pl.BlockSpec((1, GRID_W, w),
                            lambda b, i: (b, jnp.minimum((i + 1) * hb, nhalo - 1), cidx))

    row = pl.BlockSpec((1, tm, w), lambda b, i: (b, i, 0))
    in_specs = [row, row, col(COL_G), col(COL_CG), col(COL_BG), col(COL_HV)]
    args = [o_f, o_b, proj, proj, proj, proj]
    if grid_conv:
        in_specs += [prev(COL_CG), prev(COL_HV), nxt(COL_CG), nxt(COL_HV)]
        args += [proj, proj, proj, proj]
    in_specs += [
        pl.BlockSpec((3, w), lambda b, i: (0, 0)),
        pl.BlockSpec(w_out.shape, lambda b, i: (0, 0)),
        pl.BlockSpec((1, tm, d), lambda b, i: (b, i, 0)),
        pl.BlockSpec((1, 1, d), lambda b, i: (b, 0, 0)),
    ]
    args += [conv_w, w_out, x, g1]
    return pl.pallas_call(
        _mixout_grid_kernel if grid_conv else _mixout_seq_kernel,
        grid=(bsz, nt),
        in_specs=in_specs,
        out_specs=pl.BlockSpec((1, tm, d), lambda b, i: (b, i, 0)),
        out_shape=jax.ShapeDtypeStruct((bsz, t, d), F32),
        compiler_params=_params("parallel", "parallel"),
        name="mixout_grid" if grid_conv else "mixout_seq",
    )(*args)


def _oddeven_merge_sort_pairs(n):
    pairs = []
    p = 1
    while p < n:
        k = p
        while k >= 1:
            for j in range(k % p, n - k, 2 * k):
                for i in range(min(k, n - j - k)):
                    if (i + j) // (2 * p) == (i + j + k) // (2 * p):
                        pairs.append((i + j, i + j + k))
            k //= 2
        p *= 2
    return pairs


_SORT16 = _oddeven_merge_sort_pairs(PEER_TOPK)


def _sort_desc(xs):
    xs = list(xs)
    for i, j in _SORT16:
        a, b = xs[i], xs[j]
        xs[i], xs[j] = jnp.maximum(a, b), jnp.minimum(a, b)
    return xs


def _bitonic_merge_desc(xs):
    xs = list(xs)
    d = len(xs) // 2
    while d >= 1:
        for i in range(len(xs)):
            if i & d == 0:
                a, b = xs[i], xs[i + d]
                xs[i], xs[i + d] = jnp.maximum(a, b), jnp.minimum(a, b)
        d //= 2
    return xs


def _top_of_union(a, b):
    k = len(a)
    return [jnp.maximum(a[r], b[k - 1 - r]) for r in range(k)]


def _topk_rows(s):
    rows = [s[SUBLANES * v:SUBLANES * (v + 1), :] for v in range(N_KEYS // SUBLANES)]
    rows = _sort_desc(rows)
    for shift in (4, 2, 1):
        rolled = [pltpu.roll(r, shift, 0) for r in rows]
        rows = _bitonic_merge_desc(_top_of_union(rows, rolled))
    return rows


def _peer_prep_kernel(x_ref, g_ref, sc_ref, sh_ref, wq_ref, keys_ref,
                      hb_ref, n1_ref, e1_ref, r2_ref, e2_ref, s1_ref, s2_ref):
    tq = x_ref.shape[1]
    h = _norm_mod(x_ref[0], g_ref[...], sc_ref[0], sh_ref[0]).astype(BF16)
    hb_ref[0] = h
    qf = jnp.dot(h, wq_ref[...], preferred_element_type=F32)
    sub = lax.broadcasted_iota(jnp.int32, (SUBLANES, tq), 0)
    neg = jnp.full((SUBLANES, tq), -jnp.inf, F32)
    packed = [[jnp.zeros((SUBLANES, tq), F32)] * PEER_TOPK for _ in range(2)]
    for hd in range(PEER_HEADS):
        for p in range(2):
            lo = hd * PEER_QDIM + p * PEER_HALF
            s = _dot_nt(keys_ref[hd, p], qf[:, lo:lo + PEER_HALF])
            (s1_ref if p == 0 else s2_ref)[hd] = s
            top = _topk_rows(s)
            packed[p] = [jnp.where(sub == hd, top[r], packed[p][r]) for r in range(PEER_TOPK)]
    c1, c2 = packed
    pairs = [(a, b) for a in range(PEER_TOPK) for b in range(PEER_TOPK)
             if (a + 1) * (b + 1) <= PEER_TOPK]
    cand = {ab: c1[ab[0]] + c2[ab[1]] for ab in pairs}
    cands = [cand[ab] for ab in pairs]
    cands += [neg] * (-len(cands) % PEER_TOPK)
    groups = [_sort_desc(cands[i:i + PEER_TOPK]) for i in range(0, len(cands), PEER_TOPK)]
    while len(groups) > 2:
        nxt = [_bitonic_merge_desc(_top_of_union(groups[i], groups[i + 1]))
               for i in range(0, len(groups) - 1, 2)]
        if len(groups) % 2:
            nxt.append(groups[-1])
        groups = nxt
    top = _top_of_union(groups[0], groups[1]) if len(groups) == 2 else groups[0]
    tau = functools.reduce(jnp.minimum, top)
    m1, m2 = c1[0], c2[0]
    mx = m1 + m2
    z = functools.reduce(lambda a, b: a + b, [jnp.exp(t - mx) for t in top])
    inv_z = 1.0 / z
    counts = []
    for a in range(PEER_TOPK):
        n = jnp.zeros((SUBLANES, tq), F32)
        for b in range(PEER_TOPK):
            if (a, b) in cand:
                n = n + jnp.where(cand[(a, b)] >= tau, 1.0, 0.0)
        counts.append(n)
    grouped = (N_KEYS // SUBLANES, SUBLANES, tq)
    for hd in range(PEER_HEADS):
        s1 = s1_ref[hd]
        s2 = s2_ref[hd]
        n1 = jnp.zeros((N_KEYS, tq), F32)
        rank2 = jnp.full((N_KEYS, tq), float(PEER_TOPK), F32)
        for r in reversed(range(PEER_TOPK)):
            n1 = jnp.where(s1 == c1[r][hd:hd + 1, :], counts[r][hd:hd + 1, :], n1)
            rank2 = jnp.where(s2 == c2[r][hd:hd + 1, :], float(r), rank2)
        n1_ref[hd] = n1.reshape(grouped)
        r2_ref[hd] = pltpu.bitcast(rank2.astype(BF16), jnp.uint32)
        e1 = jnp.exp(s1 - m1[hd:hd + 1, :]) * inv_z[hd:hd + 1, :]
        e1_ref[hd] = e1.reshape(grouped)
        e2_ref[hd] = pltpu.bitcast(jnp.exp(s2 - m2[hd:hd + 1, :]).astype(BF16), jnp.uint32)


def _peer_prep_call(x, g, scale, shift, wq, keys):
    bsz, t, d = x.shape
    tq = min(256, t)
    nt = t // tq
    ttot = bsz * t
    score_spec = pl.BlockSpec((PEER_HEADS, N_KEYS // 2, tq), lambda b, i: (0, 0, b * nt + i))
    score_shape = jax.ShapeDtypeStruct((PEER_HEADS, N_KEYS // 2, ttot), jnp.uint32)
    n_grp = N_KEYS // SUBLANES
    grouped_spec = pl.BlockSpec((PEER_HEADS, n_grp, SUBLANES, tq), lambda b, i: (0, 0, 0, b * nt + i))
    grouped_shape = jax.ShapeDtypeStruct((PEER_HEADS, n_grp, SUBLANES, ttot), F32)
    return pl.pallas_call(
        _peer_prep_kernel,
        grid=(bsz, nt),
        in_specs=[
            pl.BlockSpec((1, tq, d), lambda b, i: (b, i, 0)),
            pl.BlockSpec((1, d), lambda b, i: (0, 0)),
            pl.BlockSpec((1, 1, d), lambda b, i: (b, 0, 0)),
            pl.BlockSpec((1, 1, d), lambda b, i: (b, 0, 0)),
            pl.BlockSpec(wq.shape, lambda b, i: (0, 0)),
            pl.BlockSpec(keys.shape, lambda b, i: (0, 0, 0, 0)),
        ],
        out_specs=[pl.BlockSpec((1, tq, d), lambda b, i: (b, i, 0)),
                   grouped_spec, grouped_spec, score_spec, score_spec],
        out_shape=[jax.ShapeDtypeStruct((bsz, t, d), BF16),
                   grouped_shape, grouped_shape, score_shape, score_shape],
        scratch_shapes=[pltpu.VMEM((PEER_HEADS, N_KEYS, tq), F32),
                        pltpu.VMEM((PEER_HEADS, N_KEYS, tq), F32)],
        compiler_params=_params("parallel", "parallel"),
        name="peer_prep",
    )(x, g, scale, shift, wq, keys)


PEER_TE = SUBLANES * N_KEYS
SQRT_HALF = math.sqrt(0.5)


def _peer_dense_kernel(hb_ref, u_ref, vt_ref, n1_ref, e1_ref, r2_ref, e2_ref,
                       x_ref, g2_ref, o_ref, acc_ref, a_ref, gt_ref, *, n_e, n_steps):
    s = pl.program_id(0)
    tt = hb_ref.shape[0]
    n_lane_tiles = tt // LANES
    n_blk = N_KEYS // PACKED_ROWS

    @pl.when(s == 0)
    def _():
        a_ref[...] = jnp.zeros_like(a_ref)
        gt_ref[...] = jnp.zeros_like(gt_ref)

    @pl.when(jnp.logical_or(s == 0, (s - 2) % n_e == 0))
    def _():
        acc_ref[...] = jnp.zeros_like(acc_ref)

    acc_ref[...] += jnp.dot(vt_ref[...], gt_ref[...], preferred_element_type=F32)

    grp = jnp.clip(s - 1, 0, n_steps - 1) % n_e
    packed = (PACKED_ROWS, LANES)
    for il in range(SUBLANES):
        for j in range(n_lane_tiles):
            lanes = pl.ds(j * LANES, LANES)
            acc = [jnp.zeros(packed, BF16)] * n_blk
            for hd in range(PEER_HEADS):
                n1b = jnp.broadcast_to(n1_ref[hd, grp, pl.ds(il, 1), lanes], packed).astype(BF16)
                e1b = jnp.broadcast_to(e1_ref[hd, grp, pl.ds(il, 1), lanes], packed).astype(BF16)
                for k in range(n_blk):
                    rows = pl.ds(k * SUBLANES, SUBLANES)
                    rank2 = pltpu.bitcast(r2_ref[hd, rows, lanes], BF16)
                    e2 = pltpu.bitcast(e2_ref[hd, rows, lanes], BF16)
                    acc[k] = acc[k] + jnp.where(rank2 < n1b, e2, 0.0) * e1b
            for k in range(n_blk):
                rows = pl.ds(il * N_KEYS + k * PACKED_ROWS, PACKED_ROWS)
                a = a_ref[rows, lanes]
                gelu = 0.5 * a * (1.0 + lax.erf(a * SQRT_HALF))
                gt_ref[rows, lanes] = acc[k] * gelu.astype(BF16)

    a_ref[...] = lax.dot_general(u_ref[...], hb_ref[...], (((1,), (1,)), ((), ())),
                                 preferred_element_type=F32)

    @pl.when(jnp.logical_and(s >= 2, (s - 2) % n_e == n_e - 1))
    def _():
        o_ref[...] = x_ref[...] + g2_ref[0] * acc_ref[...].T


def _peer_dense_call(hb, u, vt, n1, e1, r2, e2, x, g2, t_per_batch):
    ttot, d = x.shape
    tt = min(512, t_per_batch)
    n_e = N_EXPERTS // PEER_TE
    n_steps = (ttot // tt) * n_e
    per_batch = t_per_batch // tt
    tile = lambda s, lag: jnp.clip(s - lag, 0, n_steps - 1) // n_e
    group = lambda s, lag: jnp.clip(s - lag, 0, n_steps - 1) % n_e
    score_spec = pl.BlockSpec((PEER_HEADS, N_KEYS // 2, tt), lambda s: (0, 0, tile(s, 1)))
    grouped_spec = pl.BlockSpec((PEER_HEADS, N_KEYS // SUBLANES, SUBLANES, tt),
                                lambda s: (0, 0, 0, tile(s, 1)))
    return pl.pallas_call(
        functools.partial(_peer_dense_kernel, n_e=n_e, n_steps=n_steps),
        grid=(n_steps + 2,),
        in_specs=[
            pl.BlockSpec((tt, d), lambda s: (tile(s, 0), 0)),
            pl.BlockSpec((PEER_TE, d), lambda s: (group(s, 0), 0)),
            pl.BlockSpec((d, PEER_TE), lambda s: (0, group(s, 2))),
            grouped_spec, grouped_spec, score_spec, score_spec,
            pl.BlockSpec((tt, d), lambda s: (tile(s, 2), 0)),
            pl.BlockSpec((1, 1, d), lambda s: (tile(s, 2) // per_batch, 0, 0)),
        ],
        out_specs=pl.BlockSpec((tt, d), lambda s: (tile(s, 2), 0)),
        out_shape=jax.ShapeDtypeStruct((ttot, d), F32),
        scratch_shapes=[pltpu.VMEM((d, tt), F32),
                        pltpu.VMEM((PEER_TE, tt), F32),
                        pltpu.VMEM((PEER_TE, tt), BF16)],
        compiler_params=_params("arbitrary"),
        name="peer_dense",
    )(hb, u, vt, n1, e1, r2, e2, x, g2)


def _final_norm_kernel(x_ref, g_ref, o_ref):
    x = x_ref[...]
    ms = jnp.mean(x * x, axis=-1, keepdims=True)
    o_ref[...] = x * lax.rsqrt(ms + EPS) * g_ref[...]


def _final_norm_call(x, g):
    n, d = x.shape
    tm = 1024
    return pl.pallas_call(
        _final_norm_kernel,
        grid=(n // tm,),
        in_specs=[pl.BlockSpec((tm, d), lambda i: (i, 0)), pl.BlockSpec((1, d), lambda i: (0, 0))],
        out_specs=pl.BlockSpec((tm, d), lambda i: (i, 0)),
        out_shape=jax.ShapeDtypeStruct((n, d), F32),
        compiler_params=_params("parallel"),
        name="final_norm",
    )(x, g)


def _mixer(x, mod, l, norm1_g, w_in, conv_w, w_out, lbf, lbb, s0f, s0b, grid_conv, full):
    sh1, sc1, g1 = mod[0], mod[1], mod[2]
    proj = _proj_call(x, norm1_g[l][None, :], sc1, sh1, w_in[l])
    o_f, o_b, s_f, s_b = _gla_call(proj, lbf, lbb, s0f, s0b)
    if not full:
        return None, s_f, s_b
    x = _mixout_call(o_f, o_b, proj, conv_w[l], w_out[l], x, g1, grid_conv)
    return x, s_f, s_b


def _peer(x, mod, l, norm2_g, wq, keys, u, vt):
    bsz, t, d = x.shape
    sh2, sc2, g2 = mod[3], mod[4], mod[5]
    hb, n1, e1, r2, e2 = _peer_prep_call(x, norm2_g[l][None, :], sc2, sh2, wq[l], keys[l])
    out = _peer_dense_call(hb.reshape(bsz * t, d), u[l], vt[l], n1, e1, r2, e2,
                           x.reshape(bsz * t, d), g2, t)
    return out.reshape(bsz, t, d)


def kernel(x, c, ctx, c_ctx, w_mod, b_mod, norm1_g, norm2_g, w_in, conv_w, w_out, lb_logits,
           peer_wq, peer_subkeys, peer_u, peer_v, final_g):
    bsz, t, d = x.shape
    depth = w_mod.shape[0]

    p_lb = jax.nn.softmax(lb_logits.astype(F32), axis=0)
    lower = jnp.cumsum(p_lb, axis=0) - p_lb[0]

    cond = jnp.zeros((COND_ROWS, d), F32).at[:bsz].set(c).at[bsz].set(c_ctx)
    mod = _mod_call(cond, w_mod, b_mod)
    mod = mod.reshape(depth, COND_ROWS, N_MOD, d)
    mod_x = jnp.transpose(mod[:, :bsz], (0, 2, 1, 3))[:, :, :, None, :]
    mod_c = jnp.broadcast_to(mod[:, bsz][:, :, None, None, :], mod_x.shape)

    w_in_b = w_in.astype(BF16)
    w_out_b = w_out.astype(BF16)
    wq_b = peer_wq.astype(BF16)
    keys_b = peer_subkeys.astype(BF16)
    u_b = peer_u.astype(BF16)
    vt_b = jnp.swapaxes(peer_v, 1, 2).astype(BF16)

    zero_state = jnp.zeros((bsz, HG_HEADS, HG_DIM, HG_DIM), F32)
    xc = ctx
    for l in range(depth):
        lbf = lower[l, 0][None, :]
        lbb = lower[l, 1][None, :]
        full = l < depth - 1
        xc_new, s_f, s_b = _mixer(xc, mod_c[l], l, norm1_g, w_in_b, conv_w, w_out_b, lbf, lbb,
                                  zero_state, zero_state, False, full)
        if full:
            xc = _peer(xc_new, mod_c[l], l, norm2_g, wq_b, keys_b, u_b, vt_b)
        x, _, _ = _mixer(x, mod_x[l], l, norm1_g, w_in_b, conv_w, w_out_b, lbf, lbb,
                         s_f, s_b, True, True)
        x = _peer(x, mod_x[l], l, norm2_g, wq_b, keys_b, u_b, vt_b)
    return _final_norm_call(x.reshape(bsz * t, d), final_g[None, :]).reshape(bsz, t, d)
```

```python
import functools
import math

import numpy as np
import jax
import jax.numpy as jnp
from jax import lax
from jax.experimental import pallas as pl
from jax.experimental.pallas import tpu as pltpu

F32 = jnp.float32
BF16 = jnp.bfloat16

D_MODEL = 1024
GRID_W = 64
EPS = 1e-6
F_FLOOR = 1e-20
N_MOD = 6
HG_WIDTH = 512
HG_HEADS = 4
HG_DIM = HG_WIDTH // HG_HEADS
SC_WIDTH = 512
SC_HALF = SC_WIDTH // 2
IN_COLS = 5 * HG_WIDTH + 3 * SC_WIDTH
PEER_HEADS = 8
PEER_QDIM = 256
PEER_HALF = PEER_QDIM // 2
N_KEYS = 128
N_EXPERTS = N_KEYS * N_KEYS
PEER_TOPK = 16

SUBLANES = 8
LANES = 128
PACKED_ROWS = 2 * SUBLANES
VMEM_LIMIT = 48 * 1024 * 1024

GLA_CHUNK = 128
GLA_LEVELS = (64, 32, 16, 8)
COND_ROWS = 8

COL_IV, COL_ZF, COL_ZB, COL_Q, COL_G, COL_CG, COL_BG, COL_HV = range(8)


def _params(*sem):
    return pltpu.CompilerParams(dimension_semantics=sem, vmem_limit_bytes=VMEM_LIMIT)


def _dot(a, b):
    return jnp.dot(a.astype(BF16), b.astype(BF16), preferred_element_type=F32)


def _dot_nt(a, b):
    return lax.dot_general(a.astype(BF16), b.astype(BF16), (((1,), (1,)), ((), ())),
                           preferred_element_type=F32)


def _dot_tn(a, b):
    return lax.dot_general(a.astype(BF16), b.astype(BF16), (((0,), (0,)), ((), ())),
                           preferred_element_type=F32)


def _mod_kernel(cond_ref, w_ref, b_ref, o_ref):
    c = cond_ref[...]
    s = c * jax.nn.sigmoid(c)
    o_ref[0] = _dot(s, w_ref[0]) + b_ref[0]


def _mod_call(cond, w_mod, b_mod):
    depth, d, n = w_mod.shape
    tn = 1536
    return pl.pallas_call(
        _mod_kernel,
        grid=(depth, n // tn),
        in_specs=[
            pl.BlockSpec((COND_ROWS, d), lambda l, j: (0, 0)),
            pl.BlockSpec((1, d, tn), lambda l, j: (l, 0, j)),
            pl.BlockSpec((1, 1, tn), lambda l, j: (l, 0, j)),
        ],
        out_specs=pl.BlockSpec((1, COND_ROWS, tn), lambda l, j: (l, 0, j)),
        out_shape=jax.ShapeDtypeStruct((depth, COND_ROWS, n), F32),
        compiler_params=_params("parallel", "parallel"),
        name="mod",
    )(cond, w_mod, b_mod.reshape(depth, 1, n))


def _norm_mod(x, g, scale, shift):
    ms = jnp.mean(x * x, axis=-1, keepdims=True)
    y = x * lax.rsqrt(ms + EPS) * g
    return y * (1.0 + scale) + shift


def _proj_kernel(x_ref, g_ref, sc_ref, sh_ref, w_ref, o_ref):
    h = _norm_mod(x_ref[0], g_ref[...], sc_ref[0], sh_ref[0])
    o_ref[0] = _dot(h, w_ref[...])


def _proj_call(x, g, scale, shift, w):
    bsz, t, d = x.shape
    n = w.shape[1]
    tm = min(256, t)
    return pl.pallas_call(
        _proj_kernel,
        grid=(bsz, t // tm),
        in_specs=[
            pl.BlockSpec((1, tm, d), lambda b, i: (b, i, 0)),
            pl.BlockSpec((1, d), lambda b, i: (0, 0)),
            pl.BlockSpec((1, 1, d), lambda b, i: (b, 0, 0)),
            pl.BlockSpec((1, 1, d), lambda b, i: (b, 0, 0)),
            pl.BlockSpec((d, n), lambda b, i: (0, 0)),
        ],
        out_specs=pl.BlockSpec((1, tm, n), lambda b, i: (b, i, 0)),
        out_shape=jax.ShapeDtypeStruct((bsz, t, n), F32),
        compiler_params=_params("parallel", "parallel"),
        name="proj",
    )(x, g, scale, shift, w)


def _gla_sum_matrix(backward):
    c = GLA_CHUNK
    t = np.arange(c)[:, None]
    s = np.arange(c)[None, :]
    mats = []
    if not backward:
        mats.append(s <= t)
        mats.append(s > t)
    else:
        mats.append(s >= t)
        mats.append(s < t)
    for n in GLA_LEVELS:
        mid = (t // (2 * n)) * (2 * n) + n
        if not backward:
            m = np.where(t >= mid, (s >= mid) & (s <= t), (s > t) & (s < mid))
        else:
            m = np.where(t < mid, (s >= t) & (s < mid), (s >= mid) & (s < t))
        mats.append(m)
    return np.concatenate([m.astype(np.float32) for m in mats], axis=0)


def _gla_masks(backward):
    c = GLA_CHUNK
    row = lax.broadcasted_iota(jnp.int32, (c, c), 0)
    col = lax.broadcasted_iota(jnp.int32, (c, c), 1)
    qrow, lvl = [], []
    for n in GLA_LEVELS:
        r_hi = (row & (2 * n - 1)) >= n
        c_hi = (col & (2 * n - 1)) >= n
        same = (row & ~(2 * n - 1)) == (col & ~(2 * n - 1))
        if not backward:
            qrow.append(r_hi)
            lvl.append(same & r_hi & jnp.logical_not(c_hi))
        else:
            qrow.append(jnp.logical_not(r_hi))
            lvl.append(same & jnp.logical_not(r_hi) & c_hi)
    same8 = (row & ~(SUBLANES - 1)) == (col & ~(SUBLANES - 1))
    order = (col >= row) if backward else (col <= row)
    pair = [same8 & order & ((col & (SUBLANES - 1)) == i) for i in range(SUBLANES)]
    return qrow, lvl, pair


def _bcast_group_row(x, i):
    c, w = x.shape
    x3 = x.reshape(c // SUBLANES, SUBLANES, w)
    return jnp.broadcast_to(x3[:, i:i + 1, :], x3.shape).reshape(c, w)


def _gla_chain(z, q, v, lb, st, msum, masks, ones, backward):
    c = GLA_CHUNK
    qrow, lvl, pair = masks
    sig = jax.nn.sigmoid(z)
    f = lb + (1.0 - lb) * sig
    logf = jnp.log(jnp.maximum(f, F_FLOOR))
    kk = (1.0 - lb) * jax.nn.sigmoid(-z)
    hi = logf.astype(BF16)
    lo = (logf - hi.astype(F32)).astype(BF16)
    sums = (jnp.dot(msum, hi, preferred_element_type=F32)
            + jnp.dot(msum, lo, preferred_element_type=F32))
    b = sums[0:c]
    e_out = sums[c:2 * c]

    att = jnp.zeros((c, c), F32)
    for li in range(len(GLA_LEVELS)):
        scale = jnp.exp(sums[(2 + li) * c:(3 + li) * c])
        mixed = (jnp.where(qrow[li][:, :HG_DIM], q, kk) * scale).astype(BF16)
        a = lax.dot_general(mixed, mixed, (((1,), (1,)), ((), ())), preferred_element_type=F32)
        att = jnp.where(lvl[li], a, att)
    for i in range(SUBLANES):
        p = q * _bcast_group_row(kk, i) * jnp.exp(-jnp.abs(b - _bcast_group_row(b, i)))
        w = jnp.dot(p.astype(BF16), ones, preferred_element_type=F32)
        att = jnp.where(pair[i], w, att)

    q_in = q * jnp.exp(b)
    o = _dot(att, v) + _dot_nt(q_in, st)
    k_out = kk * jnp.exp(e_out)
    d = jnp.exp(b[0:1, :] if backward else b[c - 1:c, :])
    st_new = d * st + _dot_tn(v, k_out)
    return o, st_new


def _gla_kernel(ivf_ref, zf_ref, qf_ref, ivb_ref, zb_ref, qb_ref, lbf_ref, lbb_ref,
                s0f_ref, s0b_ref, mf_ref, mb_ref,
                of_ref, ob_ref, sf_ref, sb_ref, st_ref):
    n = pl.program_id(1)

    @pl.when(n == 0)
    def _():
        st_ref[0] = s0f_ref[0]
        st_ref[1] = s0b_ref[0]

    ones = jnp.ones((HG_DIM, HG_DIM), BF16)
    masks_f = _gla_masks(False)
    masks_b = _gla_masks(True)
    for h in range(HG_HEADS):
        sl = slice(h * HG_DIM, (h + 1) * HG_DIM)
        o, st = _gla_chain(zf_ref[0, :, sl], qf_ref[0, :, sl], ivf_ref[0, :, sl], lbf_ref[:, sl],
                           st_ref[0, h], mf_ref[...], masks_f, ones, False)
        of_ref[0, :, sl] = o
        st_ref[0, h] = st
        o, st = _gla_chain(zb_ref[0, :, sl], qb_ref[0, :, sl], ivb_ref[0, :, sl], lbb_ref[:, sl],
                           st_ref[1, h], mb_ref[...], masks_b, ones, True)
        ob_ref[0, :, sl] = o
        st_ref[1, h] = st

    @pl.when(n == pl.num_programs(1) - 1)
    def _():
        sf_ref[0] = st_ref[0]
        sb_ref[0] = st_ref[1]


def _gla_call(proj, lbf, lbb, s0f, s0b):
    bsz, t, _ = proj.shape
    c = GLA_CHUNK
    nc = t // c
    w = HG_WIDTH
    msum_f = jnp.asarray(_gla_sum_matrix(False), BF16)
    msum_b = jnp.asarray(_gla_sum_matrix(True), BF16)

    def fwd(col):
        return pl.BlockSpec((1, c, w), lambda b, n: (b, n, col))

    def bwd(col):
        return pl.BlockSpec((1, c, w), lambda b, n: (b, nc - 1 - n, col))

    state_spec = pl.BlockSpec((1, HG_HEADS, HG_DIM, HG_DIM), lambda b, n: (b, 0, 0, 0))
    const = lambda shape: pl.BlockSpec(shape, lambda b, n: (0,) * len(shape))
    return pl.pallas_call(
        _gla_kernel,
        grid=(bsz, nc),
        in_specs=[fwd(COL_IV), fwd(COL_ZF), fwd(COL_Q), bwd(COL_IV), bwd(COL_ZB), bwd(COL_Q),
                  const((1, w)), const((1, w)), state_spec, state_spec,
                  const(msum_f.shape), const(msum_b.shape)],
        out_specs=[pl.BlockSpec((1, c, w), lambda b, n: (b, n, 0)),
                   pl.BlockSpec((1, c, w), lambda b, n: (b, nc - 1 - n, 0)),
                   state_spec, state_spec],
        out_shape=[jax.ShapeDtypeStruct((bsz, t, w), F32),
                   jax.ShapeDtypeStruct((bsz, t, w), F32),
                   jax.ShapeDtypeStruct((bsz, HG_HEADS, HG_DIM, HG_DIM), F32),
                   jax.ShapeDtypeStruct((bsz, HG_HEADS, HG_DIM, HG_DIM), F32)],
        scratch_shapes=[pltpu.VMEM((2, HG_HEADS, HG_DIM, HG_DIM), F32)],
        compiler_params=_params("parallel", "arbitrary"),
        name="gla",
    )(proj, proj, proj, proj, proj, proj, lbf, lbb, s0f, s0b, msum_f, msum_b)


def _mixout_body(of_ref, ob_ref, g_ref, cg_ref, bg_ref, hv_ref, halo, cw_ref, wo_ref,
                 x_ref, g1_ref, o_ref, grid_conv):
    tm = of_ref.shape[1]
    o = of_ref[0] + ob_ref[0]
    heads = []
    for h in range(HG_HEADS):
        oh = o[:, h * HG_DIM:(h + 1) * HG_DIM]
        heads.append(oh * lax.rsqrt(jnp.mean(oh * oh, axis=-1, keepdims=True) + EPS))
    g = g_ref[0]
    o_rec = jnp.concatenate(heads, axis=-1) * (g * jax.nn.sigmoid(g))

    u = cg_ref[0] * hv_ref[0]
    w0, w1, w2 = cw_ref[0:1, :], cw_ref[1:2, :], cw_ref[2:3, :]
    pos = lax.broadcasted_iota(jnp.int32, (tm, 1), 0)
    if grid_conv:
        cgp_ref, hvp_ref, cgn_ref, hvn_ref = halo
        i = pl.program_id(1)
        last = pl.num_programs(1) - 1
        colpos = pos & (GRID_W - 1)
        uh = u[:, :SC_HALF]
        left = jnp.where(colpos == 0, 0.0, pltpu.roll(uh, 1, 0))
        right = jnp.where(colpos == GRID_W - 1, 0.0, pltpu.roll(uh, tm - 1, 0))
        conv_h = w0[:, :SC_HALF] * left + w1[:, :SC_HALF] * uh + w2[:, :SC_HALF] * right
        uv = u[:, SC_HALF:]
        up_halo = jnp.where(i == 0, 0.0, cgp_ref[0][:, SC_HALF:] * hvp_ref[0][:, SC_HALF:])
        dn_halo = jnp.where(i == last, 0.0, cgn_ref[0][:, SC_HALF:] * hvn_ref[0][:, SC_HALF:])
        up = jnp.concatenate([up_halo, uv[:tm - GRID_W]], axis=0)
        down = jnp.concatenate([uv[GRID_W:], dn_halo], axis=0)
        conv_v = w0[:, SC_HALF:] * up + w1[:, SC_HALF:] * uv + w2[:, SC_HALF:] * down
        conv = jnp.concatenate([conv_h, conv_v], axis=-1)
    else:
        left = jnp.where(pos == 0, 0.0, pltpu.roll(u, 1, 0))
        right = jnp.where(pos == tm - 1, 0.0, pltpu.roll(u, tm - 1, 0))
        conv = w0 * left + w1 * u + w2 * right
    o_conv = bg_ref[0] * conv
    y = _dot(o_rec, wo_ref[0:HG_WIDTH, :]) + _dot(o_conv, wo_ref[HG_WIDTH:, :])
    o_ref[0] = x_ref[0] + g1_ref[0] * y


def _mixout_grid_kernel(of_ref, ob_ref, g_ref, cg_ref, bg_ref, hv_ref, cgp_ref, hvp_ref,
                        cgn_ref, hvn_ref, cw_ref, wo_ref, x_ref, g1_ref, o_ref):
    _mixout_body(of_ref, ob_ref, g_ref, cg_ref, bg_ref, hv_ref,
                 (cgp_ref, hvp_ref, cgn_ref, hvn_ref), cw_ref, wo_ref, x_ref, g1_ref, o_ref, True)


def _mixout_seq_kernel(of_ref, ob_ref, g_ref, cg_ref, bg_ref, hv_ref, cw_ref, wo_ref,
                       x_ref, g1_ref, o_ref):
    _mixout_body(of_ref, ob_ref, g_ref, cg_ref, bg_ref, hv_ref, None, cw_ref, wo_ref,
                 x_ref, g1_ref, o_ref, False)


def _mixout_call(o_f, o_b, proj, conv_w, w_out, x, g1, grid_conv):
    bsz, t, d = x.shape
    w = HG_WIDTH
    tm = 512 if grid_conv else t
    nt = t // tm
    hb = tm // GRID_W
    nhalo = t // GRID_W

    def col(cidx):
        return pl.BlockSpec((1, tm, w), lambda b, i: (b, i, cidx))

    def prev(cidx):
        return pl.BlockSpec((1, GRID_W, w), lambda b, i: (b, jnp.maximum(i * hb - 1, 0), cidx))

    def nxt(cidx):
        return pl.BlockSpec((1, GRID_W, w),
                            lambda b, i: (b, jnp.minimum((i + 1) * hb, nhalo - 1), cidx))

    row = pl.BlockSpec((1, tm, w), lambda b, i: (b, i, 0))
    in_specs = [row, row, col(COL_G), col(COL_CG), col(COL_BG), col(COL_HV)]
    args = [o_f, o_b, proj, proj, proj, proj]
    if grid_conv:
        in_specs += [prev(COL_CG), prev(COL_HV), nxt(COL_CG), nxt(COL_HV)]
        args += [proj, proj, proj, proj]
    in_specs += [
        pl.BlockSpec((3, w), lambda b, i: (0, 0)),
        pl.BlockSpec(w_out.shape, lambda b, i: (0, 0)),
        pl.BlockSpec((1, tm, d), lambda b, i: (b, i, 0)),
        pl.BlockSpec((1, 1, d), lambda b, i: (b, 0, 0)),
    ]
    args += [conv_w, w_out, x, g1]
    return pl.pallas_call(
        _mixout_grid_kernel if grid_conv else _mixout_seq_kernel,
        grid=(bsz, nt),
        in_specs=in_specs,
        out_specs=pl.BlockSpec((1, tm, d), lambda b, i: (b, i, 0)),
        out_shape=jax.ShapeDtypeStruct((bsz, t, d), F32),
        compiler_params=_params("parallel", "parallel"),
        name="mixout_grid" if grid_conv else "mixout_seq",
    )(*args)


def _oddeven_merge_sort_pairs(n):
    pairs = []
    p = 1
    while p < n:
        k = p
        while k >= 1:
            for j in range(k % p, n - k, 2 * k):
                for i in range(min(k, n - j - k)):
                    if (i + j) // (2 * p) == (i + j + k) // (2 * p):
                        pairs.append((i + j, i + j + k))
            k //= 2
        p *= 2
    return pairs


_SORT16 = _oddeven_merge_sort_pairs(PEER_TOPK)


def _sort_desc(xs):
    xs = list(xs)
    for i, j in _SORT16:
        a, b = xs[i], xs[j]
        xs[i], xs[j] = jnp.maximum(a, b), jnp.minimum(a, b)
    return xs


def _bitonic_merge_desc(xs):
    xs = list(xs)
    d = len(xs) // 2
    while d >= 1:
        for i in range(len(xs)):
            if i & d == 0:
                a, b = xs[i], xs[i + d]
                xs[i], xs[i + d] = jnp.maximum(a, b), jnp.minimum(a, b)
        d //= 2
    return xs


def _top_of_union(a, b):
    k = len(a)
    return [jnp.maximum(a[r], b[k - 1 - r]) for r in range(k)]


def _topk_rows(s):
    rows = [s[SUBLANES * v:SUBLANES * (v + 1), :] for v in range(N_KEYS // SUBLANES)]
    rows = _sort_desc(rows)
    for shift in (4, 2, 1):
        rolled = [pltpu.roll(r, shift, 0) for r in rows]
        rows = _bitonic_merge_desc(_top_of_union(rows, rolled))
    return rows


def _peer_prep_kernel(x_ref, g_ref, sc_ref, sh_ref, wq_ref, keys_ref,
                      ht_ref, n1_ref, e1_ref, r2_ref, e2_ref, s1_ref, s2_ref):
    tq = x_ref.shape[1]
    hf = _norm_mod(x_ref[0], g_ref[...], sc_ref[0], sh_ref[0])
    ht_ref[...] = hf.T.astype(BF16)
    qf = jnp.dot(hf.astype(BF16), wq_ref[...], preferred_element_type=F32)
    sub = lax.broadcasted_iota(jnp.int32, (SUBLANES, tq), 0)
    neg = jnp.full((SUBLANES, tq), -jnp.inf, F32)
    packed = [[jnp.zeros((SUBLANES, tq), F32)] * PEER_TOPK for _ in range(2)]
    for hd in range(PEER_HEADS):
        for p in range(2):
            lo = hd * PEER_QDIM + p * PEER_HALF
            s = _dot_nt(keys_ref[hd, p], qf[:, lo:lo + PEER_HALF])
            (s1_ref if p == 0 else s2_ref)[hd] = s
            top = _topk_rows(s)
            packed[p] = [jnp.where(sub == hd, top[r], packed[p][r]) for r in range(PEER_TOPK)]
    c1, c2 = packed
    pairs = [(a, b) for a in range(PEER_TOPK) for b in range(PEER_TOPK)
             if (a + 1) * (b + 1) <= PEER_TOPK]
    cand = {ab: c1[ab[0]] + c2[ab[1]] for ab in pairs}
    cands = [cand[ab] for ab in pairs]
    cands += [neg] * (-len(cands) % PEER_TOPK)
    groups = [_sort_desc(cands[i:i + PEER_TOPK]) for i in range(0, len(cands), PEER_TOPK)]
    while len(groups) > 2:
        nxt = [_bitonic_merge_desc(_top_of_union(groups[i], groups[i + 1]))
               for i in range(0, len(groups) - 1, 2)]
        if len(groups) % 2:
            nxt.append(groups[-1])
        groups = nxt
    top = _top_of_union(groups[0], groups[1]) if len(groups) == 2 else groups[0]
    tau = functools.reduce(jnp.minimum, top)
    m1, m2 = c1[0], c2[0]
    mx = m1 + m2
    z = functools.reduce(lambda a, b: a + b, [jnp.exp(t - mx) for t in top])
    inv_z = 1.0 / z
    counts = []
    for a in range(PEER_TOPK):
        n = jnp.zeros((SUBLANES, tq), F32)
        for b in range(PEER_TOPK):
            if (a, b) in cand:
                n = n + jnp.where(cand[(a, b)] >= tau, 1.0, 0.0)
        counts.append(n)
    grouped = (N_KEYS // SUBLANES, SUBLANES, tq)
    for hd in range(PEER_HEADS):
        s1 = s1_ref[hd]
        s2 = s2_ref[hd]
        n1 = jnp.zeros((N_KEYS, tq), F32)
        rank2 = jnp.full((N_KEYS, tq), float(PEER_TOPK), F32)
        for r in reversed(range(PEER_TOPK)):
            n1 = jnp.where(s1 == c1[r][hd:hd + 1, :], counts[r][hd:hd + 1, :], n1)
            rank2 = jnp.where(s2 == c2[r][hd:hd + 1, :], float(r), rank2)
        n1_ref[hd] = n1.reshape(grouped)
        r2_ref[hd] = pltpu.bitcast(rank2.astype(BF16), jnp.uint32)
        e1 = jnp.exp(s1 - m1[hd:hd + 1, :]) * inv_z[hd:hd + 1, :]
        e1_ref[hd] = e1.reshape(grouped)
        e2_ref[hd] = pltpu.bitcast(jnp.exp(s2 - m2[hd:hd + 1, :]).astype(BF16), jnp.uint32)


def _peer_prep_call(x, g, scale, shift, wq, keys):
    bsz, t, d = x.shape
    tq = min(256, t)
    nt = t // tq
    ttot = bsz * t
    score_spec = pl.BlockSpec((PEER_HEADS, N_KEYS // 2, tq), lambda b, i: (0, 0, b * nt + i))
    score_shape = jax.ShapeDtypeStruct((PEER_HEADS, N_KEYS // 2, ttot), jnp.uint32)
    n_grp = N_KEYS // SUBLANES
    grouped_spec = pl.BlockSpec((PEER_HEADS, n_grp, SUBLANES, tq), lambda b, i: (0, 0, 0, b * nt + i))
    grouped_shape = jax.ShapeDtypeStruct((PEER_HEADS, n_grp, SUBLANES, ttot), F32)
    return pl.pallas_call(
        _peer_prep_kernel,
        grid=(bsz, nt),
        in_specs=[
            pl.BlockSpec((1, tq, d), lambda b, i: (b, i, 0)),
            pl.BlockSpec((1, d), lambda b, i: (0, 0)),
            pl.BlockSpec((1, 1, d), lambda b, i: (b, 0, 0)),
            pl.BlockSpec((1, 1, d), lambda b, i: (b, 0, 0)),
            pl.BlockSpec(wq.shape, lambda b, i: (0, 0)),
            pl.BlockSpec(keys.shape, lambda b, i: (0, 0, 0, 0)),
        ],
        out_specs=[pl.BlockSpec((d, tq), lambda b, i: (0, b * nt + i)),
                   grouped_spec, grouped_spec, score_spec, score_spec],
        out_shape=[jax.ShapeDtypeStruct((d, ttot), BF16),
                   grouped_shape, grouped_shape, score_shape, score_shape],
        scratch_shapes=[pltpu.VMEM((PEER_HEADS, N_KEYS, tq), F32),
                        pltpu.VMEM((PEER_HEADS, N_KEYS, tq), F32)],
        compiler_params=_params("parallel", "parallel"),
        name="peer_prep",
    )(x, g, scale, shift, wq, keys)


PEER_TE = SUBLANES * N_KEYS
SQRT_HALF = math.sqrt(0.5)


def _peer_dense_kernel(ht_ref, u_ref, vt_ref, n1_ref, e1_ref, r2_ref, e2_ref,
                       x_ref, g2_ref, o_ref, acc_ref, a_ref, gt_ref, *, n_e, n_steps):
    s = pl.program_id(0)
    tt = ht_ref.shape[1]
    n_lane_tiles = tt // LANES
    n_blk = N_KEYS // PACKED_ROWS

    @pl.when(s == 0)
    def _():
        a_ref[...] = jnp.zeros_like(a_ref)
        gt_ref[...] = jnp.zeros_like(gt_ref)

    @pl.when(jnp.logical_or(s == 0, (s - 2) % n_e == 0))
    def _():
        acc_ref[...] = jnp.zeros_like(acc_ref)

    acc_ref[...] += jnp.dot(vt_ref[0], gt_ref[...], preferred_element_type=F32)

    grp = jnp.clip(s - 1, 0, n_steps - 1) % n_e
    packed = (PACKED_ROWS, LANES)
    for il in range(SUBLANES):
        for j in range(n_lane_tiles):
            lanes = pl.ds(j * LANES, LANES)
            acc = [jnp.zeros(packed, BF16)] * n_blk
            for hd in range(PEER_HEADS):
                n1b = jnp.broadcast_to(n1_ref[hd, grp, pl.ds(il, 1), lanes], packed).astype(BF16)
                e1b = jnp.broadcast_to(e1_ref[hd, grp, pl.ds(il, 1), lanes], packed).astype(BF16)
                for k in range(n_blk):
                    rows = pl.ds(k * SUBLANES, SUBLANES)
                    rank2 = pltpu.bitcast(r2_ref[hd, rows, lanes], BF16)
                    e2 = pltpu.bitcast(e2_ref[hd, rows, lanes], BF16)
                    acc[k] = acc[k] + jnp.where(rank2 < n1b, e2, 0.0) * e1b
            for k in range(n_blk):
                rows = pl.ds(il * N_KEYS + k * PACKED_ROWS, PACKED_ROWS)
                a = a_ref[rows, lanes]
                gelu = 0.5 * a * (1.0 + lax.erf(a * SQRT_HALF))
                gt_ref[rows, lanes] = acc[k] * gelu.astype(BF16)

    a_ref[...] = jnp.dot(u_ref[...], ht_ref[...], preferred_element_type=F32)

    @pl.when(jnp.logical_and(s >= 2, (s - 2) % n_e == n_e - 1))
    def _():
        o_ref[...] = x_ref[...] + g2_ref[0] * acc_ref[...].T


def _peer_dense_call(ht, u, vt, n1, e1, r2, e2, x, g2, t_per_batch):
    ttot, d = x.shape
    tt = min(512, t_per_batch)
    n_e = N_EXPERTS // PEER_TE
    n_steps = (ttot // tt) * n_e
    per_batch = t_per_batch // tt
    tile = lambda s, lag: jnp.clip(s - lag, 0, n_steps - 1) // n_e
    group = lambda s, lag: jnp.clip(s - lag, 0, n_steps - 1) % n_e
    score_spec = pl.BlockSpec((PEER_HEADS, N_KEYS // 2, tt), lambda s: (0, 0, tile(s, 1)))
    grouped_spec = pl.BlockSpec((PEER_HEADS, N_KEYS // SUBLANES, SUBLANES, tt),
                                lambda s: (0, 0, 0, tile(s, 1)))
    return pl.pallas_call(
        functools.partial(_peer_dense_kernel, n_e=n_e, n_steps=n_steps),
        grid=(n_steps + 2,),
        in_specs=[
            pl.BlockSpec((d, tt), lambda s: (0, tile(s, 0))),
            pl.BlockSpec((PEER_TE, d), lambda s: (group(s, 0), 0)),
            pl.BlockSpec((1, d, PEER_TE), lambda s: (group(s, 2), 0, 0)),
            grouped_spec, grouped_spec, score_spec, score_spec,
            pl.BlockSpec((tt, d), lambda s: (tile(s, 2), 0)),
            pl.BlockSpec((1, 1, d), lambda s: (tile(s, 2) // per_batch, 0, 0)),
        ],
        out_specs=pl.BlockSpec((tt, d), lambda s: (tile(s, 2), 0)),
        out_shape=jax.ShapeDtypeStruct((ttot, d), F32),
        scratch_shapes=[pltpu.VMEM((d, tt), F32),
                        pltpu.VMEM((PEER_TE, tt), F32),
                        pltpu.VMEM((PEER_TE, tt), BF16)],
        compiler_params=_params("arbitrary"),
        name="peer_dense",
    )(ht, u, vt, n1, e1, r2, e2, x, g2)


def _final_norm_kernel(x_ref, g_ref, o_ref):
    x = x_ref[...]
    ms = jnp.mean(x * x, axis=-1, keepdims=True)
    o_ref[...] = x * lax.rsqrt(ms + EPS) * g_ref[...]


def _final_norm_call(x, g):
    n, d = x.shape
    tm = 1024
    return pl.pallas_call(
        _final_norm_kernel,
        grid=(n // tm,),
        in_specs=[pl.BlockSpec((tm, d), lambda i: (i, 0)), pl.BlockSpec((1, d), lambda i: (0, 0))],
        out_specs=pl.BlockSpec((tm, d), lambda i: (i, 0)),
        out_shape=jax.ShapeDtypeStruct((n, d), F32),
        compiler_params=_params("parallel"),
        name="final_norm",
    )(x, g)


def _mixer(x, mod, l, norm1_g, w_in, conv_w, w_out, lbf, lbb, s0f, s0b, grid_conv, full):
    sh1, sc1, g1 = mod[0], mod[1], mod[2]
    proj = _proj_call(x, norm1_g[l][None, :], sc1, sh1, w_in[l])
    o_f, o_b, s_f, s_b = _gla_call(proj, lbf, lbb, s0f, s0b)
    if not full:
        return None, s_f, s_b
    x = _mixout_call(o_f, o_b, proj, conv_w[l], w_out[l], x, g1, grid_conv)
    return x, s_f, s_b


def _peer(x, mod, l, norm2_g, wq, keys, u, vt):
    bsz, t, d = x.shape
    sh2, sc2, g2 = mod[3], mod[4], mod[5]
    ht, n1, e1, r2, e2 = _peer_prep_call(x, norm2_g[l][None, :], sc2, sh2, wq[l], keys[l])
    out = _peer_dense_call(ht, u[l], vt[l], n1, e1, r2, e2,
                           x.reshape(bsz * t, d), g2, t)
    return out.reshape(bsz, t, d)


def kernel(x, c, ctx, c_ctx, w_mod, b_mod, norm1_g, norm2_g, w_in, conv_w, w_out, lb_logits,
           peer_wq, peer_subkeys, peer_u, peer_v, final_g):
    bsz, t, d = x.shape
    depth = w_mod.shape[0]

    p_lb = jax.nn.softmax(lb_logits.astype(F32), axis=0)
    lower = jnp.cumsum(p_lb, axis=0) - p_lb[0]

    cond = jnp.zeros((COND_ROWS, d), F32).at[:bsz].set(c).at[bsz].set(c_ctx)
    mod = _mod_call(cond, w_mod, b_mod)
    mod = mod.reshape(depth, COND_ROWS, N_MOD, d)
    mod_x = jnp.transpose(mod[:, :bsz], (0, 2, 1, 3))[:, :, :, None, :]
    mod_c = jnp.broadcast_to(mod[:, bsz][:, :, None, None, :], mod_x.shape)

    w_in_b = w_in.astype(BF16)
    w_out_b = w_out.astype(BF16)
    wq_b = peer_wq.astype(BF16)
    keys_b = peer_subkeys.astype(BF16)
    u_b = peer_u.astype(BF16)
    vt_b = jnp.swapaxes(peer_v.astype(BF16).reshape(depth, N_EXPERTS // PEER_TE, PEER_TE, d), 2, 3)

    zero_state = jnp.zeros((bsz, HG_HEADS, HG_DIM, HG_DIM), F32)
    xc = ctx
    for l in range(depth):
        lbf = lower[l, 0][None, :]
        lbb = lower[l, 1][None, :]
        full = l < depth - 1
        xc_new, s_f, s_b = _mixer(xc, mod_c[l], l, norm1_g, w_in_b, conv_w, w_out_b, lbf, lbb,
                                  zero_state, zero_state, False, full)
        if full:
            xc = _peer(xc_new, mod_c[l], l, norm2_g, wq_b, keys_b, u_b, vt_b)
        x, _, _ = _mixer(x, mod_x[l], l, norm1_g, w_in_b, conv_w, w_out_b, lbf, lbb,
                         s_f, s_b, True, True)
        x = _peer(x, mod_x[l], l, norm2_g, wq_b, keys_b, u_b, vt_b)
    return _final_norm_call(x.reshape(bsz * t, d), final_g[None, :]).reshape(bsz, t, d)
```

```python
import functools
import math

import numpy as np
import jax
import jax.numpy as jnp
from jax import lax
from jax.experimental import pallas as pl
from jax.experimental.pallas import tpu as pltpu

F32 = jnp.float32
BF16 = jnp.bfloat16

D_MODEL = 1024
GRID_W = 64
EPS = 1e-6
F_FLOOR = 1e-20
N_MOD = 6
HG_WIDTH = 512
HG_HEADS = 4
HG_DIM = HG_WIDTH // HG_HEADS
SC_WIDTH = 512
SC_HALF = SC_WIDTH // 2
IN_COLS = 5 * HG_WIDTH + 3 * SC_WIDTH
PEER_HEADS = 8
PEER_QDIM = 256
PEER_HALF = PEER_QDIM // 2
N_KEYS = 128
N_EXPERTS = N_KEYS * N_KEYS
PEER_TOPK = 16

SUBLANES = 8
LANES = 128
PACKED_ROWS = 2 * SUBLANES
VMEM_LIMIT = 48 * 1024 * 1024

GLA_CHUNK = 128
GLA_LEVELS = (64, 32, 16, 8, 4, 2, 1)
COND_ROWS = 8

COL_IV, COL_ZF, COL_ZB, COL_Q, COL_G, COL_CG, COL_BG, COL_HV = range(8)


def _params(*sem):
    return pltpu.CompilerParams(dimension_semantics=sem, vmem_limit_bytes=VMEM_LIMIT)


def _dot(a, b):
    return jnp.dot(a.astype(BF16), b.astype(BF16), preferred_element_type=F32)


def _dot_nt(a, b):
    return lax.dot_general(a.astype(BF16), b.astype(BF16), (((1,), (1,)), ((), ())),
                           preferred_element_type=F32)


def _dot_tn(a, b):
    return lax.dot_general(a.astype(BF16), b.astype(BF16), (((0,), (0,)), ((), ())),
                           preferred_element_type=F32)


def _mod_kernel(cond_ref, w_ref, b_ref, o_ref):
    c = cond_ref[...]
    s = c * jax.nn.sigmoid(c)
    o_ref[0] = _dot(s, w_ref[0]) + b_ref[0]


def _mod_call(cond, w_mod, b_mod):
    depth, d, n = w_mod.shape
    tn = 1536
    return pl.pallas_call(
        _mod_kernel,
        grid=(depth, n // tn),
        in_specs=[
            pl.BlockSpec((COND_ROWS, d), lambda l, j: (0, 0)),
            pl.BlockSpec((1, d, tn), lambda l, j: (l, 0, j)),
            pl.BlockSpec((1, 1, tn), lambda l, j: (l, 0, j)),
        ],
        out_specs=pl.BlockSpec((1, COND_ROWS, tn), lambda l, j: (l, 0, j)),
        out_shape=jax.ShapeDtypeStruct((depth, COND_ROWS, n), F32),
        compiler_params=_params("parallel", "parallel"),
        name="mod",
    )(cond, w_mod, b_mod.reshape(depth, 1, n))


def _norm_mod(x, g, scale, shift):
    ms = jnp.mean(x * x, axis=-1, keepdims=True)
    y = x * lax.rsqrt(ms + EPS) * g
    return y * (1.0 + scale) + shift


def _proj_kernel(x_ref, g_ref, sc_ref, sh_ref, w_ref, o_ref):
    h = _norm_mod(x_ref[0], g_ref[...], sc_ref[0], sh_ref[0])
    o_ref[0] = _dot(h, w_ref[...])


def _proj_call(x, g, scale, shift, w, l):
    bsz, t, d = x.shape
    n = w.shape[2]
    tm = min(256, t)
    return pl.pallas_call(
        _proj_kernel,
        grid=(bsz, t // tm),
        in_specs=[
            pl.BlockSpec((1, tm, d), lambda b, i: (b, i, 0)),
            pl.BlockSpec((1, d), lambda b, i: (0, 0)),
            pl.BlockSpec((1, 1, d), lambda b, i: (b, 0, 0)),
            pl.BlockSpec((1, 1, d), lambda b, i: (b, 0, 0)),
            pl.BlockSpec((None, d, n), lambda b, i: (l, 0, 0)),
        ],
        out_specs=pl.BlockSpec((1, tm, n), lambda b, i: (b, i, 0)),
        out_shape=jax.ShapeDtypeStruct((bsz, t, n), F32),
        compiler_params=_params("parallel", "parallel"),
        name="proj",
    )(x, g, scale, shift, w)


def _gla_sum_matrix(backward):
    c = GLA_CHUNK
    t = np.arange(c)[:, None]
    s = np.arange(c)[None, :]
    mats = []
    if not backward:
        mats.append(s <= t)
        mats.append(s > t)
    else:
        mats.append(s >= t)
        mats.append(s < t)
    for n in GLA_LEVELS:
        mid = (t // (2 * n)) * (2 * n) + n
        if not backward:
            m = np.where(t >= mid, (s >= mid) & (s <= t), (s > t) & (s < mid))
        else:
            m = np.where(t < mid, (s >= t) & (s < mid), (s >= mid) & (s < t))
        mats.append(m)
    return np.concatenate([m.astype(np.float32) for m in mats], axis=0)


def _gla_masks(backward):
    c = GLA_CHUNK
    row = lax.broadcasted_iota(jnp.int32, (c, c), 0)
    col = lax.broadcasted_iota(jnp.int32, (c, c), 1)
    qrow, lvl = [], []
    for n in GLA_LEVELS:
        r_hi = (row & (2 * n - 1)) >= n
        c_hi = (col & (2 * n - 1)) >= n
        same = (row & ~(2 * n - 1)) == (col & ~(2 * n - 1))
        if not backward:
            qrow.append(r_hi)
            lvl.append(same & r_hi & jnp.logical_not(c_hi))
        else:
            qrow.append(jnp.logical_not(r_hi))
            lvl.append(same & jnp.logical_not(r_hi) & c_hi)
    return qrow, lvl, row == col


def _gla_chain(z, q, v, lb, st, msum, masks, ones, backward):
    c = GLA_CHUNK
    qrow, lvl, diag = masks
    sig = jax.nn.sigmoid(z)
    f = lb + (1.0 - lb) * sig
    logf = jnp.log(jnp.maximum(f, F_FLOOR))
    kk = (1.0 - lb) * jax.nn.sigmoid(-z)
    sums = jnp.dot(msum, logf.astype(BF16), preferred_element_type=F32)
    b = sums[0:c]
    e_out = sums[c:2 * c]

    att = jnp.where(diag, jnp.dot((q * kk).astype(BF16), ones, preferred_element_type=F32), 0.0)
    for li in range(len(GLA_LEVELS)):
        scale = jnp.exp(sums[(2 + li) * c:(3 + li) * c])
        mixed = (jnp.where(qrow[li][:, :HG_DIM], q, kk) * scale).astype(BF16)
        a = lax.dot_general(mixed, mixed, (((1,), (1,)), ((), ())), preferred_element_type=F32)
        att = jnp.where(lvl[li], a, att)

    q_in = q * jnp.exp(b)
    o = _dot(att, v) + _dot_nt(q_in, st)
    k_out = kk * jnp.exp(e_out)
    d = jnp.exp(b[0:1, :] if backward else b[c - 1:c, :])
    st_new = d * st + _dot_tn(v, k_out)
    return o, st_new


def _gla_kernel(ivf_ref, zf_ref, qf_ref, ivb_ref, zb_ref, qb_ref, lbf_ref, lbb_ref,
                s0f_ref, s0b_ref, mf_ref, mb_ref,
                of_ref, ob_ref, sf_ref, sb_ref, st_ref):
    n = pl.program_id(1)

    @pl.when(n == 0)
    def _():
        st_ref[0] = s0f_ref[0]
        st_ref[1] = s0b_ref[0]

    ones = jnp.ones((HG_DIM, HG_DIM), BF16)
    masks_f = _gla_masks(False)
    masks_b = _gla_masks(True)
    for h in range(HG_HEADS):
        sl = slice(h * HG_DIM, (h + 1) * HG_DIM)
        o, st = _gla_chain(zf_ref[0, :, sl], qf_ref[0, :, sl], ivf_ref[0, :, sl], lbf_ref[:, sl],
                           st_ref[0, h], mf_ref[...], masks_f, ones, False)
        of_ref[0, :, sl] = o
        st_ref[0, h] = st
        o, st = _gla_chain(zb_ref[0, :, sl], qb_ref[0, :, sl], ivb_ref[0, :, sl], lbb_ref[:, sl],
                           st_ref[1, h], mb_ref[...], masks_b, ones, True)
        ob_ref[0, :, sl] = o
        st_ref[1, h] = st

    @pl.when(n == pl.num_programs(1) - 1)
    def _():
        sf_ref[0] = st_ref[0]
        sb_ref[0] = st_ref[1]


def _gla_call(proj, lbf, lbb, s0f, s0b):
    bsz, t, _ = proj.shape
    c = GLA_CHUNK
    nc = t // c
    w = HG_WIDTH
    msum_f = jnp.asarray(_gla_sum_matrix(False), BF16)
    msum_b = jnp.asarray(_gla_sum_matrix(True), BF16)

    def fwd(col):
        return pl.BlockSpec((1, c, w), lambda b, n: (b, n, col))

    def bwd(col):
        return pl.BlockSpec((1, c, w), lambda b, n: (b, nc - 1 - n, col))

    state_spec = pl.BlockSpec((1, HG_HEADS, HG_DIM, HG_DIM), lambda b, n: (b, 0, 0, 0))
    const = lambda shape: pl.BlockSpec(shape, lambda b, n: (0,) * len(shape))
    return pl.pallas_call(
        _gla_kernel,
        grid=(bsz, nc),
        in_specs=[fwd(COL_IV), fwd(COL_ZF), fwd(COL_Q), bwd(COL_IV), bwd(COL_ZB), bwd(COL_Q),
                  const((1, w)), const((1, w)), state_spec, state_spec,
                  const(msum_f.shape), const(msum_b.shape)],
        out_specs=[pl.BlockSpec((1, c, w), lambda b, n: (b, n, 0)),
                   pl.BlockSpec((1, c, w), lambda b, n: (b, nc - 1 - n, 0)),
                   state_spec, state_spec],
        out_shape=[jax.ShapeDtypeStruct((bsz, t, w), F32),
                   jax.ShapeDtypeStruct((bsz, t, w), F32),
                   jax.ShapeDtypeStruct((bsz, HG_HEADS, HG_DIM, HG_DIM), F32),
                   jax.ShapeDtypeStruct((bsz, HG_HEADS, HG_DIM, HG_DIM), F32)],
        scratch_shapes=[pltpu.VMEM((2, HG_HEADS, HG_DIM, HG_DIM), F32)],
        compiler_params=_params("parallel", "arbitrary"),
        name="gla",
    )(proj, proj, proj, proj, proj, proj, lbf, lbb, s0f, s0b, msum_f, msum_b)


def _mixout_body(of_ref, ob_ref, g_ref, cg_ref, bg_ref, hv_ref, halo, cw_ref, wo_ref,
                 x_ref, g1_ref, o_ref, grid_conv):
    tm = of_ref.shape[1]
    o = of_ref[0] + ob_ref[0]
    heads = []
    for h in range(HG_HEADS):
        oh = o[:, h * HG_DIM:(h + 1) * HG_DIM]
        heads.append(oh * lax.rsqrt(jnp.mean(oh * oh, axis=-1, keepdims=True) + EPS))
    g = g_ref[0]
    o_rec = jnp.concatenate(heads, axis=-1) * (g * jax.nn.sigmoid(g))

    u = cg_ref[0] * hv_ref[0]
    w0, w1, w2 = cw_ref[0:1, :], cw_ref[1:2, :], cw_ref[2:3, :]
    pos = lax.broadcasted_iota(jnp.int32, (tm, 1), 0)
    if grid_conv:
        cgp_ref, hvp_ref, cgn_ref, hvn_ref = halo
        i = pl.program_id(1)
        last = pl.num_programs(1) - 1
        colpos = pos & (GRID_W - 1)
        uh = u[:, :SC_HALF]
        left = jnp.where(colpos == 0, 0.0, pltpu.roll(uh, 1, 0))
        right = jnp.where(colpos == GRID_W - 1, 0.0, pltpu.roll(uh, tm - 1, 0))
        conv_h = w0[:, :SC_HALF] * left + w1[:, :SC_HALF] * uh + w2[:, :SC_HALF] * right
        uv = u[:, SC_HALF:]
        up_halo = jnp.where(i == 0, 0.0, cgp_ref[0][:, SC_HALF:] * hvp_ref[0][:, SC_HALF:])
        dn_halo = jnp.where(i == last, 0.0, cgn_ref[0][:, SC_HALF:] * hvn_ref[0][:, SC_HALF:])
        up = jnp.concatenate([up_halo, uv[:tm - GRID_W]], axis=0)
        down = jnp.concatenate([uv[GRID_W:], dn_halo], axis=0)
        conv_v = w0[:, SC_HALF:] * up + w1[:, SC_HALF:] * uv + w2[:, SC_HALF:] * down
        conv = jnp.concatenate([conv_h, conv_v], axis=-1)
    else:
        left = jnp.where(pos == 0, 0.0, pltpu.roll(u, 1, 0))
        right = jnp.where(pos == tm - 1, 0.0, pltpu.roll(u, tm - 1, 0))
        conv = w0 * left + w1 * u + w2 * right
    o_conv = bg_ref[0] * conv
    y = _dot(o_rec, wo_ref[0:HG_WIDTH, :]) + _dot(o_conv, wo_ref[HG_WIDTH:, :])
    o_ref[0] = x_ref[0] + g1_ref[0] * y


def _mixout_grid_kernel(of_ref, ob_ref, g_ref, cg_ref, bg_ref, hv_ref, cgp_ref, hvp_ref,
                        cgn_ref, hvn_ref, cw_ref, wo_ref, x_ref, g1_ref, o_ref):
    _mixout_body(of_ref, ob_ref, g_ref, cg_ref, bg_ref, hv_ref,
                 (cgp_ref, hvp_ref, cgn_ref, hvn_ref), cw_ref, wo_ref, x_ref, g1_ref, o_ref, True)


def _mixout_seq_kernel(of_ref, ob_ref, g_ref, cg_ref, bg_ref, hv_ref, cw_ref, wo_ref,
                       x_ref, g1_ref, o_ref):
    _mixout_body(of_ref, ob_ref, g_ref, cg_ref, bg_ref, hv_ref, None, cw_ref, wo_ref,
                 x_ref, g1_ref, o_ref, False)


def _mixout_call(o_f, o_b, proj, conv_w, w_out, x, g1, grid_conv):
    bsz, t, d = x.shape
    w = HG_WIDTH
    tm = 512 if grid_conv else t
    nt = t // tm
    hb = tm // GRID_W
    nhalo = t // GRID_W

    def col(cidx):
        return pl.BlockSpec((1, tm, w), lambda b, i: (b, i, cidx))

    def prev(cidx):
        return pl.BlockSpec((1, GRID_W, w), lambda b, i: (b, jnp.maximum(i * hb - 1, 0), cidx))

    def nxt(cidx):
        return pl.BlockSpec((1, GRID_W, w),
                            lambda b, i: (b, jnp.minimum((i + 1) * hb, nhalo - 1), cidx))

    row = pl.BlockSpec((1, tm, w), lambda b, i: (b, i, 0))
    in_specs = [row, row, col(COL_G), col(COL_CG), col(COL_BG), col(COL_HV)]
    args = [o_f, o_b, proj, proj, proj, proj]
    if grid_conv:
        in_specs += [prev(COL_CG), prev(COL_HV), nxt(COL_CG), nxt(COL_HV)]
        args += [proj, proj, proj, proj]
    in_specs += [
        pl.BlockSpec((3, w), lambda b, i: (0, 0)),
        pl.BlockSpec(w_out.shape, lambda b, i: (0, 0)),
        pl.BlockSpec((1, tm, d), lambda b, i: (b, i, 0)),
        pl.BlockSpec((1, 1, d), lambda b, i: (b, 0, 0)),
    ]
    args += [conv_w, w_out, x, g1]
    return pl.pallas_call(
        _mixout_grid_kernel if grid_conv else _mixout_seq_kernel,
        grid=(bsz, nt),
        in_specs=in_specs,
        out_specs=pl.BlockSpec((1, tm, d), lambda b, i: (b, i, 0)),
        out_shape=jax.ShapeDtypeStruct((bsz, t, d), F32),
        compiler_params=_params("parallel", "parallel"),
        name="mixout_grid" if grid_conv else "mixout_seq",
    )(*args)


def _oddeven_merge_sort_pairs(n):
    pairs = []
    p = 1
    while p < n:
        k = p
        while k >= 1:
            for j in range(k % p, n - k, 2 * k):
                for i in range(min(k, n - j - k)):
                    if (i + j) // (2 * p) == (i + j + k) // (2 * p):
                        pairs.append((i + j, i + j + k))
            k //= 2
        p *= 2
    return pairs


_SORT16 = _oddeven_merge_sort_pairs(PEER_TOPK)


def _sort_desc(xs):
    xs = list(xs)
    for i, j in _SORT16:
        a, b = xs[i], xs[j]
        xs[i], xs[j] = jnp.maximum(a, b), jnp.minimum(a, b)
    return xs


def _bitonic_merge_desc(xs):
    xs = list(xs)
    d = len(xs) // 2
    while d >= 1:
        for i in range(len(xs)):
            if i & d == 0:
                a, b = xs[i], xs[i + d]
                xs[i], xs[i + d] = jnp.maximum(a, b), jnp.minimum(a, b)
        d //= 2
    return xs


def _top_of_union(a, b):
    k = len(a)
    return [jnp.maximum(a[r], b[k - 1 - r]) for r in range(k)]


def _topk_rows(s):
    rows = [s[SUBLANES * v:SUBLANES * (v + 1), :] for v in range(N_KEYS // SUBLANES)]
    rows = _sort_desc(rows)
    for shift in (4, 2, 1):
        rolled = [pltpu.roll(r, shift, 0) for r in rows]
        rows = _bitonic_merge_desc(_top_of_union(rows, rolled))
    return rows


def _peer_prep_kernel(x_ref, g_ref, sc_ref, sh_ref, wq_ref, keys_ref,
                      ht_ref, n1_ref, e1_ref, r2_ref, e2_ref, s1_ref, s2_ref):
    tq = x_ref.shape[1]
    hf = _norm_mod(x_ref[0], g_ref[...], sc_ref[0], sh_ref[0])
    ht_ref[...] = hf.T.astype(BF16)
    qf = jnp.dot(hf.astype(BF16), wq_ref[...], preferred_element_type=F32)
    sub = lax.broadcasted_iota(jnp.int32, (SUBLANES, tq), 0)
    neg = jnp.full((SUBLANES, tq), -jnp.inf, F32)
    packed = [[jnp.zeros((SUBLANES, tq), F32)] * PEER_TOPK for _ in range(2)]
    for hd in range(PEER_HEADS):
        for p in range(2):
            lo = hd * PEER_QDIM + p * PEER_HALF
            s = _dot_nt(keys_ref[hd, p], qf[:, lo:lo + PEER_HALF])
            (s1_ref if p == 0 else s2_ref)[hd] = s
            top = _topk_rows(s)
            packed[p] = [jnp.where(sub == hd, top[r], packed[p][r]) for r in range(PEER_TOPK)]
    c1, c2 = packed
    pairs = [(a, b) for a in range(PEER_TOPK) for b in range(PEER_TOPK)
             if (a + 1) * (b + 1) <= PEER_TOPK]
    cand = {ab: c1[ab[0]] + c2[ab[1]] for ab in pairs}
    cands = [cand[ab] for ab in pairs]
    cands += [neg] * (-len(cands) % PEER_TOPK)
    groups = [_sort_desc(cands[i:i + PEER_TOPK]) for i in range(0, len(cands), PEER_TOPK)]
    while len(groups) > 2:
        nxt = [_bitonic_merge_desc(_top_of_union(groups[i], groups[i + 1]))
               for i in range(0, len(groups) - 1, 2)]
        if len(groups) % 2:
            nxt.append(groups[-1])
        groups = nxt
    top = _top_of_union(groups[0], groups[1]) if len(groups) == 2 else groups[0]
    tau = functools.reduce(jnp.minimum, top)
    m1, m2 = c1[0], c2[0]
    mx = m1 + m2
    z = functools.reduce(lambda a, b: a + b, [jnp.exp(t - mx) for t in top])
    inv_z = 1.0 / z
    counts = []
    for a in range(PEER_TOPK):
        n = jnp.zeros((SUBLANES, tq), F32)
        for b in range(PEER_TOPK):
            if (a, b) in cand:
                n = n + jnp.where(cand[(a, b)] >= tau, 1.0, 0.0)
        counts.append(n)
    grouped = (N_KEYS // SUBLANES, SUBLANES, tq)
    for hd in range(PEER_HEADS):
        s1 = s1_ref[hd]
        s2 = s2_ref[hd]
        n1 = jnp.zeros((N_KEYS, tq), F32)
        rank2 = jnp.full((N_KEYS, tq), float(PEER_TOPK), F32)
        for r in reversed(range(PEER_TOPK)):
            n1 = jnp.where(s1 == c1[r][hd:hd + 1, :], counts[r][hd:hd + 1, :], n1)
            rank2 = jnp.where(s2 == c2[r][hd:hd + 1, :], float(r), rank2)
        n1_ref[hd] = n1.reshape(grouped)
        r2_ref[hd] = pltpu.bitcast(rank2.astype(BF16), jnp.uint32)
        e1 = jnp.exp(s1 - m1[hd:hd + 1, :]) * inv_z[hd:hd + 1, :]
        e1_ref[hd] = e1.reshape(grouped)
        e2_ref[hd] = pltpu.bitcast(jnp.exp(s2 - m2[hd:hd + 1, :]).astype(BF16), jnp.uint32)


def _peer_prep_call(x, g, scale, shift, wq, keys, l):
    bsz, t, d = x.shape
    tq = min(256, t)
    nt = t // tq
    ttot = bsz * t
    score_spec = pl.BlockSpec((PEER_HEADS, N_KEYS // 2, tq), lambda b, i: (0, 0, b * nt + i))
    score_shape = jax.ShapeDtypeStruct((PEER_HEADS, N_KEYS // 2, ttot), jnp.uint32)
    n_grp = N_KEYS // SUBLANES
    grouped_spec = pl.BlockSpec((PEER_HEADS, n_grp, SUBLANES, tq), lambda b, i: (0, 0, 0, b * nt + i))
    grouped_shape = jax.ShapeDtypeStruct((PEER_HEADS, n_grp, SUBLANES, ttot), F32)
    return pl.pallas_call(
        _peer_prep_kernel,
        grid=(bsz, nt),
        in_specs=[
            pl.BlockSpec((1, tq, d), lambda b, i: (b, i, 0)),
            pl.BlockSpec((1, d), lambda b, i: (0, 0)),
            pl.BlockSpec((1, 1, d), lambda b, i: (b, 0, 0)),
            pl.BlockSpec((1, 1, d), lambda b, i: (b, 0, 0)),
            pl.BlockSpec((None,) + wq.shape[1:], lambda b, i: (l, 0, 0)),
            pl.BlockSpec((None,) + keys.shape[1:], lambda b, i: (l, 0, 0, 0, 0)),
        ],
        out_specs=[pl.BlockSpec((d, tq), lambda b, i: (0, b * nt + i)),
                   grouped_spec, grouped_spec, score_spec, score_spec],
        out_shape=[jax.ShapeDtypeStruct((d, ttot), BF16),
                   grouped_shape, grouped_shape, score_shape, score_shape],
        scratch_shapes=[pltpu.VMEM((PEER_HEADS, N_KEYS, tq), F32),
                        pltpu.VMEM((PEER_HEADS, N_KEYS, tq), F32)],
        compiler_params=_params("parallel", "parallel"),
        name="peer_prep",
    )(x, g, scale, shift, wq, keys)


PEER_TE = SUBLANES * N_KEYS
SQRT_HALF = math.sqrt(0.5)


def _peer_dense_kernel(ht_ref, u_ref, vt_ref, n1_ref, e1_ref, r2_ref, e2_ref,
                       x_ref, g2_ref, o_ref, acc_ref, a_ref, gt_ref, *, n_e, n_steps):
    s = pl.program_id(0)
    tt = ht_ref.shape[1]
    n_lane_tiles = tt // LANES
    n_blk = N_KEYS // PACKED_ROWS

    @pl.when(s == 0)
    def _():
        a_ref[...] = jnp.zeros_like(a_ref)
        gt_ref[...] = jnp.zeros_like(gt_ref)

    @pl.when(jnp.logical_or(s == 0, (s - 2) % n_e == 0))
    def _():
        acc_ref[...] = jnp.zeros_like(acc_ref)

    acc_ref[...] += jnp.dot(vt_ref[0], gt_ref[...], preferred_element_type=F32)

    grp = jnp.clip(s - 1, 0, n_steps - 1) % n_e
    packed = (PACKED_ROWS, LANES)
    for il in range(SUBLANES):
        for j in range(n_lane_tiles):
            lanes = pl.ds(j * LANES, LANES)
            acc = [jnp.zeros(packed, BF16)] * n_blk
            for hd in range(PEER_HEADS):
                n1b = jnp.broadcast_to(n1_ref[hd, grp, pl.ds(il, 1), lanes], packed).astype(BF16)
                e1b = jnp.broadcast_to(e1_ref[hd, grp, pl.ds(il, 1), lanes], packed).astype(BF16)
                for k in range(n_blk):
                    rows = pl.ds(k * SUBLANES, SUBLANES)
                    rank2 = pltpu.bitcast(r2_ref[hd, rows, lanes], BF16)
                    e2 = pltpu.bitcast(e2_ref[hd, rows, lanes], BF16)
                    acc[k] = acc[k] + jnp.where(rank2 < n1b, e2, 0.0) * e1b
            for k in range(n_blk):
                rows = pl.ds(il * N_KEYS + k * PACKED_ROWS, PACKED_ROWS)
                a = a_ref[rows, lanes]
                gelu = 0.5 * a * (1.0 + lax.erf(a * SQRT_HALF))
                gt_ref[rows, lanes] = acc[k] * gelu.astype(BF16)

    a_ref[...] = jnp.dot(u_ref[...], ht_ref[...], preferred_element_type=F32)

    @pl.when(jnp.logical_and(s >= 2, (s - 2) % n_e == n_e - 1))
    def _():
        o_ref[...] = x_ref[...] + g2_ref[0] * acc_ref[...].T


def _peer_dense_call(ht, u, vt, n1, e1, r2, e2, x, g2, t_per_batch, l):
    ttot, d = x.shape
    tt = min(512, t_per_batch)
    n_e = N_EXPERTS // PEER_TE
    n_steps = (ttot // tt) * n_e
    per_batch = t_per_batch // tt
    tile = lambda s, lag: jnp.clip(s - lag, 0, n_steps - 1) // n_e
    group = lambda s, lag: jnp.clip(s - lag, 0, n_steps - 1) % n_e
    score_spec = pl.BlockSpec((PEER_HEADS, N_KEYS // 2, tt), lambda s: (0, 0, tile(s, 1)))
    grouped_spec = pl.BlockSpec((PEER_HEADS, N_KEYS // SUBLANES, SUBLANES, tt),
                                lambda s: (0, 0, 0, tile(s, 1)))
    return pl.pallas_call(
        functools.partial(_peer_dense_kernel, n_e=n_e, n_steps=n_steps),
        grid=(n_steps + 2,),
        in_specs=[
            pl.BlockSpec((d, tt), lambda s: (0, tile(s, 0))),
            pl.BlockSpec((None, PEER_TE, d), lambda s: (l, group(s, 0), 0)),
            pl.BlockSpec((None, 1, d, PEER_TE), lambda s: (l, group(s, 2), 0, 0)),
            grouped_spec, grouped_spec, score_spec, score_spec,
            pl.BlockSpec((tt, d), lambda s: (tile(s, 2), 0)),
            pl.BlockSpec((1, 1, d), lambda s: (tile(s, 2) // per_batch, 0, 0)),
        ],
        out_specs=pl.BlockSpec((tt, d), lambda s: (tile(s, 2), 0)),
        out_shape=jax.ShapeDtypeStruct((ttot, d), F32),
        scratch_shapes=[pltpu.VMEM((d, tt), F32),
                        pltpu.VMEM((PEER_TE, tt), F32),
                        pltpu.VMEM((PEER_TE, tt), BF16)],
        compiler_params=_params("arbitrary"),
        name="peer_dense",
    )(ht, u, vt, n1, e1, r2, e2, x, g2)


def _final_norm_kernel(x_ref, g_ref, o_ref):
    x = x_ref[...]
    ms = jnp.mean(x * x, axis=-1, keepdims=True)
    o_ref[...] = x * lax.rsqrt(ms + EPS) * g_ref[...]


def _final_norm_call(x, g):
    n, d = x.shape
    tm = 1024
    return pl.pallas_call(
        _final_norm_kernel,
        grid=(n // tm,),
        in_specs=[pl.BlockSpec((tm, d), lambda i: (i, 0)), pl.BlockSpec((1, d), lambda i: (0, 0))],
        out_specs=pl.BlockSpec((tm, d), lambda i: (i, 0)),
        out_shape=jax.ShapeDtypeStruct((n, d), F32),
        compiler_params=_params("parallel"),
        name="final_norm",
    )(x, g)


def _mixer(x, mod, l, norm1_g, w_in, conv_w, w_out, lbf, lbb, s0f, s0b, grid_conv, full):
    sh1, sc1, g1 = mod[0], mod[1], mod[2]
    proj = _proj_call(x, norm1_g[l][None, :], sc1, sh1, w_in, l)
    o_f, o_b, s_f, s_b = _gla_call(proj, lbf, lbb, s0f, s0b)
    if not full:
        return None, s_f, s_b
    x = _mixout_call(o_f, o_b, proj, conv_w[l], w_out[l], x, g1, grid_conv)
    return x, s_f, s_b


def _peer(x, mod, l, norm2_g, wq, keys, u, vt):
    bsz, t, d = x.shape
    sh2, sc2, g2 = mod[3], mod[4], mod[5]
    ht, n1, e1, r2, e2 = _peer_prep_call(x, norm2_g[l][None, :], sc2, sh2, wq, keys, l)
    out = _peer_dense_call(ht, u, vt, n1, e1, r2, e2, x.reshape(bsz * t, d), g2, t, l)
    return out.reshape(bsz, t, d)


def kernel(x, c, ctx, c_ctx, w_mod, b_mod, norm1_g, norm2_g, w_in, conv_w, w_out, lb_logits,
           peer_wq, peer_subkeys, peer_u, peer_v, final_g):
    bsz, t, d = x.shape
    depth = w_mod.shape[0]

    p_lb = jax.nn.softmax(lb_logits.astype(F32), axis=0)
    lower = jnp.cumsum(p_lb, axis=0) - p_lb[0]

    cond = jnp.zeros((COND_ROWS, d), F32).at[:bsz].set(c).at[bsz].set(c_ctx)
    mod = _mod_call(cond, w_mod, b_mod)
    mod = mod.reshape(depth, COND_ROWS, N_MOD, d)
    mod_x = jnp.transpose(mod[:, :bsz], (0, 2, 1, 3))[:, :, :, None, :]
    mod_c = jnp.broadcast_to(mod[:, bsz][:, :, None, None, :], mod_x.shape)

    w_in_b = w_in.astype(BF16)
    w_out_b = w_out.astype(BF16)
    wq_b = peer_wq.astype(BF16)
    keys_b = peer_subkeys.astype(BF16)
    u_b = peer_u.astype(BF16)
    vt_b = jnp.swapaxes(peer_v.astype(BF16).reshape(depth, N_EXPERTS // PEER_TE, PEER_TE, d), 2, 3)

    zero_state = jnp.zeros((bsz, HG_HEADS, HG_DIM, HG_DIM), F32)
    xc = ctx
    for l in range(depth):
        lbf = lower[l, 0][None, :]
        lbb = lower[l, 1][None, :]
        full = l < depth - 1
        xc_new, s_f, s_b = _mixer(xc, mod_c[l], l, norm1_g, w_in_b, conv_w, w_out_b, lbf, lbb,
                                  zero_state, zero_state, False, full)
        if full:
            xc = _peer(xc_new, mod_c[l], l, norm2_g, wq_b, keys_b, u_b, vt_b)
        x, _, _ = _mixer(x, mod_x[l], l, norm1_g, w_in_b, conv_w, w_out_b, lbf, lbb,
                         s_f, s_b, True, True)
        x = _peer(x, mod_x[l], l, norm2_g, wq_b, keys_b, u_b, vt_b)
    return _final_norm_call(x.reshape(bsz * t, d), final_g[None, :]).reshape(bsz, t, d)
```

```python
import functools
import math

import numpy as np
import jax
import jax.numpy as jnp
from jax import lax
from jax.experimental import pallas as pl
from jax.experimental.pallas import tpu as pltpu

F32 = jnp.float32
BF16 = jnp.bfloat16

D_MODEL = 1024
GRID_W = 64
EPS = 1e-6
F_FLOOR = 1e-20
N_MOD = 6
HG_WIDTH = 512
HG_HEADS = 4
HG_DIM = HG_WIDTH // HG_HEADS
SC_WIDTH = 512
SC_HALF = SC_WIDTH // 2
IN_COLS = 5 * HG_WIDTH + 3 * SC_WIDTH
PEER_HEADS = 8
PEER_QDIM = 256
PEER_HALF = PEER_QDIM // 2
N_KEYS = 128
N_EXPERTS = N_KEYS * N_KEYS
PEER_TOPK = 16

SUBLANES = 8
LANES = 128
PACKED_ROWS = 2 * SUBLANES
VMEM_LIMIT = 48 * 1024 * 1024

GLA_CHUNK = 128
GLA_LEVELS = (64, 32, 16, 8, 4, 2, 1)
COND_ROWS = 8

COL_IV, COL_ZF, COL_ZB, COL_Q, COL_G, COL_CG, COL_BG, COL_HV = range(8)


def _params(*sem):
    return pltpu.CompilerParams(dimension_semantics=sem, vmem_limit_bytes=VMEM_LIMIT)


def _dot(a, b):
    return jnp.dot(a.astype(BF16), b.astype(BF16), preferred_element_type=F32)


def _dot_nt(a, b):
    return lax.dot_general(a.astype(BF16), b.astype(BF16), (((1,), (1,)), ((), ())),
                           preferred_element_type=F32)


def _dot_tn(a, b):
    return lax.dot_general(a.astype(BF16), b.astype(BF16), (((0,), (0,)), ((), ())),
                           preferred_element_type=F32)


def _mod_kernel(cond_ref, w_ref, b_ref, o_ref):
    c = cond_ref[...]
    s = c * jax.nn.sigmoid(c)
    o_ref[0] = _dot(s, w_ref[0]) + b_ref[0]


def _mod_call(cond, w_mod, b_mod):
    depth, d, n = w_mod.shape
    tn = 1536
    return pl.pallas_call(
        _mod_kernel,
        grid=(depth, n // tn),
        in_specs=[
            pl.BlockSpec((COND_ROWS, d), lambda l, j: (0, 0)),
            pl.BlockSpec((1, d, tn), lambda l, j: (l, 0, j)),
            pl.BlockSpec((1, 1, tn), lambda l, j: (l, 0, j)),
        ],
        out_specs=pl.BlockSpec((1, COND_ROWS, tn), lambda l, j: (l, 0, j)),
        out_shape=jax.ShapeDtypeStruct((depth, COND_ROWS, n), F32),
        compiler_params=_params("parallel", "parallel"),
        name="mod",
    )(cond, w_mod, b_mod.reshape(depth, 1, n))


def _norm_mod(x, g, scale, shift):
    ms = jnp.mean(x * x, axis=-1, keepdims=True)
    y = x * lax.rsqrt(ms + EPS) * g
    return y * (1.0 + scale) + shift


def _proj_kernel(x_ref, g_ref, sc_ref, sh_ref, w_ref, o_ref):
    h = _norm_mod(x_ref[0], g_ref[...], sc_ref[0], sh_ref[0])
    o_ref[0] = _dot(h, w_ref[...])


def _proj_call(x, g, scale, shift, w, l):
    bsz, t, d = x.shape
    n = w.shape[2]
    tm = min(256, t)
    return pl.pallas_call(
        _proj_kernel,
        grid=(bsz, t // tm),
        in_specs=[
            pl.BlockSpec((1, tm, d), lambda b, i: (b, i, 0)),
            pl.BlockSpec((1, d), lambda b, i: (0, 0)),
            pl.BlockSpec((1, 1, d), lambda b, i: (b, 0, 0)),
            pl.BlockSpec((1, 1, d), lambda b, i: (b, 0, 0)),
            pl.BlockSpec((None, d, n), lambda b, i: (l, 0, 0)),
        ],
        out_specs=pl.BlockSpec((1, tm, n), lambda b, i: (b, i, 0)),
        out_shape=jax.ShapeDtypeStruct((bsz, t, n), F32),
        compiler_params=_params("parallel", "parallel"),
        name="proj",
    )(x, g, scale, shift, w)


def _gla_sum_matrix(backward):
    c = GLA_CHUNK
    t = np.arange(c)[:, None]
    s = np.arange(c)[None, :]
    mats = []
    if not backward:
        mats.append(s <= t)
        mats.append(s > t)
    else:
        mats.append(s >= t)
        mats.append(s < t)
    for n in GLA_LEVELS:
        mid = (t // (2 * n)) * (2 * n) + n
        if not backward:
            m = np.where(t >= mid, (s >= mid) & (s <= t), (s > t) & (s < mid))
        else:
            m = np.where(t < mid, (s >= t) & (s < mid), (s >= mid) & (s < t))
        mats.append(m)
    return np.concatenate([m.astype(np.float32) for m in mats], axis=0)


def _gla_masks(backward):
    c = GLA_CHUNK
    row = lax.broadcasted_iota(jnp.int32, (c, c), 0)
    col = lax.broadcasted_iota(jnp.int32, (c, c), 1)
    qrow, lvl = [], []
    for n in GLA_LEVELS:
        r_hi = (row & (2 * n - 1)) >= n
        c_hi = (col & (2 * n - 1)) >= n
        same = (row & ~(2 * n - 1)) == (col & ~(2 * n - 1))
        if not backward:
            qrow.append(r_hi)
            lvl.append(same & r_hi & jnp.logical_not(c_hi))
        else:
            qrow.append(jnp.logical_not(r_hi))
            lvl.append(same & jnp.logical_not(r_hi) & c_hi)
    return qrow, lvl, row == col


def _gla_chain(z, q, v, lb, st, msum, masks, ones, backward):
    c = GLA_CHUNK
    qrow, lvl, diag = masks
    sig = jax.nn.sigmoid(z)
    f = lb + (1.0 - lb) * sig
    logf = jnp.log(jnp.maximum(f, F_FLOOR))
    kk = (1.0 - lb) * jax.nn.sigmoid(-z)
    sums = jnp.dot(msum, logf.astype(BF16), preferred_element_type=F32)
    b = sums[0:c]
    e_out = sums[c:2 * c]

    att = jnp.where(diag, jnp.dot((q * kk).astype(BF16), ones, preferred_element_type=F32), 0.0)
    for li in range(len(GLA_LEVELS)):
        scale = jnp.exp(sums[(2 + li) * c:(3 + li) * c])
        mixed = (jnp.where(qrow[li][:, :HG_DIM], q, kk) * scale).astype(BF16)
        a = lax.dot_general(mixed, mixed, (((1,), (1,)), ((), ())), preferred_element_type=F32)
        att = jnp.where(lvl[li], a, att)

    q_in = q * jnp.exp(b)
    o = _dot(att, v) + _dot_nt(q_in, st)
    k_out = kk * jnp.exp(e_out)
    d = jnp.exp(b[0:1, :] if backward else b[c - 1:c, :])
    st_new = d * st + _dot_tn(v, k_out)
    return o, st_new


def _gla_kernel(ivf_ref, zf_ref, qf_ref, ivb_ref, zb_ref, qb_ref, lbf_ref, lbb_ref,
                s0f_ref, s0b_ref, mf_ref, mb_ref,
                of_ref, ob_ref, sf_ref, sb_ref, st_ref):
    n = pl.program_id(1)

    @pl.when(n == 0)
    def _():
        st_ref[0] = s0f_ref[0]
        st_ref[1] = s0b_ref[0]

    ones = jnp.ones((HG_DIM, HG_DIM), BF16)
    masks_f = _gla_masks(False)
    masks_b = _gla_masks(True)
    for h in range(HG_HEADS):
        sl = slice(h * HG_DIM, (h + 1) * HG_DIM)
        o, st = _gla_chain(zf_ref[0, :, sl], qf_ref[0, :, sl], ivf_ref[0, :, sl], lbf_ref[:, sl],
                           st_ref[0, h], mf_ref[...], masks_f, ones, False)
        of_ref[0, :, sl] = o
        st_ref[0, h] = st
        o, st = _gla_chain(zb_ref[0, :, sl], qb_ref[0, :, sl], ivb_ref[0, :, sl], lbb_ref[:, sl],
                           st_ref[1, h], mb_ref[...], masks_b, ones, True)
        ob_ref[0, :, sl] = o
        st_ref[1, h] = st

    @pl.when(n == pl.num_programs(1) - 1)
    def _():
        sf_ref[0] = st_ref[0]
        sb_ref[0] = st_ref[1]


def _gla_call(proj, lbf, lbb, s0f, s0b):
    bsz, t, _ = proj.shape
    c = GLA_CHUNK
    nc = t // c
    w = HG_WIDTH
    msum_f = jnp.asarray(_gla_sum_matrix(False), BF16)
    msum_b = jnp.asarray(_gla_sum_matrix(True), BF16)

    def fwd(col):
        return pl.BlockSpec((1, c, w), lambda b, n: (b, n, col))

    def bwd(col):
        return pl.BlockSpec((1, c, w), lambda b, n: (b, nc - 1 - n, col))

    state_spec = pl.BlockSpec((1, HG_HEADS, HG_DIM, HG_DIM), lambda b, n: (b, 0, 0, 0))
    const = lambda shape: pl.BlockSpec(shape, lambda b, n: (0,) * len(shape))
    return pl.pallas_call(
        _gla_kernel,
        grid=(bsz, nc),
        in_specs=[fwd(COL_IV), fwd(COL_ZF), fwd(COL_Q), bwd(COL_IV), bwd(COL_ZB), bwd(COL_Q),
                  const((1, w)), const((1, w)), state_spec, state_spec,
                  const(msum_f.shape), const(msum_b.shape)],
        out_specs=[pl.BlockSpec((1, c, w), lambda b, n: (b, n, 0)),
                   pl.BlockSpec((1, c, w), lambda b, n: (b, nc - 1 - n, 0)),
                   state_spec, state_spec],
        out_shape=[jax.ShapeDtypeStruct((bsz, t, w), F32),
                   jax.ShapeDtypeStruct((bsz, t, w), F32),
                   jax.ShapeDtypeStruct((bsz, HG_HEADS, HG_DIM, HG_DIM), F32),
                   jax.ShapeDtypeStruct((bsz, HG_HEADS, HG_DIM, HG_DIM), F32)],
        scratch_shapes=[pltpu.VMEM((2, HG_HEADS, HG_DIM, HG_DIM), F32)],
        compiler_params=_params("parallel", "arbitrary"),
        name="gla",
    )(proj, proj, proj, proj, proj, proj, lbf, lbb, s0f, s0b, msum_f, msum_b)


def _mixout_body(of_ref, ob_ref, g_ref, cg_ref, bg_ref, hv_ref, halo, cw_ref, wo_ref,
                 x_ref, g1_ref, o_ref, grid_conv):
    tm = of_ref.shape[1]
    o = of_ref[0] + ob_ref[0]
    heads = []
    for h in range(HG_HEADS):
        oh = o[:, h * HG_DIM:(h + 1) * HG_DIM]
        heads.append(oh * lax.rsqrt(jnp.mean(oh * oh, axis=-1, keepdims=True) + EPS))
    g = g_ref[0]
    o_rec = jnp.concatenate(heads, axis=-1) * (g * jax.nn.sigmoid(g))

    u = cg_ref[0] * hv_ref[0]
    w0, w1, w2 = cw_ref[0:1, :], cw_ref[1:2, :], cw_ref[2:3, :]
    pos = lax.broadcasted_iota(jnp.int32, (tm, 1), 0)
    if grid_conv:
        cgp_ref, hvp_ref, cgn_ref, hvn_ref = halo
        i = pl.program_id(1)
        last = pl.num_programs(1) - 1
        colpos = pos & (GRID_W - 1)
        uh = u[:, :SC_HALF]
        left = jnp.where(colpos == 0, 0.0, pltpu.roll(uh, 1, 0))
        right = jnp.where(colpos == GRID_W - 1, 0.0, pltpu.roll(uh, tm - 1, 0))
        conv_h = w0[:, :SC_HALF] * left + w1[:, :SC_HALF] * uh + w2[:, :SC_HALF] * right
        uv = u[:, SC_HALF:]
        up_halo = jnp.where(i == 0, 0.0, cgp_ref[0][:, SC_HALF:] * hvp_ref[0][:, SC_HALF:])
        dn_halo = jnp.where(i == last, 0.0, cgn_ref[0][:, SC_HALF:] * hvn_ref[0][:, SC_HALF:])
        up = jnp.concatenate([up_halo, uv[:tm - GRID_W]], axis=0)
        down = jnp.concatenate([uv[GRID_W:], dn_halo], axis=0)
        conv_v = w0[:, SC_HALF:] * up + w1[:, SC_HALF:] * uv + w2[:, SC_HALF:] * down
        conv = jnp.concatenate([conv_h, conv_v], axis=-1)
    else:
        left = jnp.where(pos == 0, 0.0, pltpu.roll(u, 1, 0))
        right = jnp.where(pos == tm - 1, 0.0, pltpu.roll(u, tm - 1, 0))
        conv = w0 * left + w1 * u + w2 * right
    o_conv = bg_ref[0] * conv
    y = _dot(o_rec, wo_ref[0:HG_WIDTH, :]) + _dot(o_conv, wo_ref[HG_WIDTH:, :])
    o_ref[0] = x_ref[0] + g1_ref[0] * y


def _mixout_grid_kernel(of_ref, ob_ref, g_ref, cg_ref, bg_ref, hv_ref, cgp_ref, hvp_ref,
                        cgn_ref, hvn_ref, cw_ref, wo_ref, x_ref, g1_ref, o_ref):
    _mixout_body(of_ref, ob_ref, g_ref, cg_ref, bg_ref, hv_ref,
                 (cgp_ref, hvp_ref, cgn_ref, hvn_ref), cw_ref, wo_ref, x_ref, g1_ref, o_ref, True)


def _mixout_seq_kernel(of_ref, ob_ref, g_ref, cg_ref, bg_ref, hv_ref, cw_ref, wo_ref,
                       x_ref, g1_ref, o_ref):
    _mixout_body(of_ref, ob_ref, g_ref, cg_ref, bg_ref, hv_ref, None, cw_ref, wo_ref,
                 x_ref, g1_ref, o_ref, False)


def _mixout_call(o_f, o_b, proj, conv_w, w_out, x, g1, grid_conv):
    bsz, t, d = x.shape
    w = HG_WIDTH
    tm = 512 if grid_conv else t
    nt = t // tm
    hb = tm // GRID_W
    nhalo = t // GRID_W

    def col(cidx):
        return pl.BlockSpec((1, tm, w), lambda b, i: (b, i, cidx))

    def prev(cidx):
        return pl.BlockSpec((1, GRID_W, w), lambda b, i: (b, jnp.maximum(i * hb - 1, 0), cidx))

    def nxt(cidx):
        return pl.BlockSpec((1, GRID_W, w),
                            lambda b, i: (b, jnp.minimum((i + 1) * hb, nhalo - 1), cidx))

    row = pl.BlockSpec((1, tm, w), lambda b, i: (b, i, 0))
    in_specs = [row, row, col(COL_G), col(COL_CG), col(COL_BG), col(COL_HV)]
    args = [o_f, o_b, proj, proj, proj, proj]
    if grid_conv:
        in_specs += [prev(COL_CG), prev(COL_HV), nxt(COL_CG), nxt(COL_HV)]
        args += [proj, proj, proj, proj]
    in_specs += [
        pl.BlockSpec((3, w), lambda b, i: (0, 0)),
        pl.BlockSpec(w_out.shape, lambda b, i: (0, 0)),
        pl.BlockSpec((1, tm, d), lambda b, i: (b, i, 0)),
        pl.BlockSpec((1, 1, d), lambda b, i: (b, 0, 0)),
    ]
    args += [conv_w, w_out, x, g1]
    return pl.pallas_call(
        _mixout_grid_kernel if grid_conv else _mixout_seq_kernel,
        grid=(bsz, nt),
        in_specs=in_specs,
        out_specs=pl.BlockSpec((1, tm, d), lambda b, i: (b, i, 0)),
        out_shape=jax.ShapeDtypeStruct((bsz, t, d), F32),
        compiler_params=_params("parallel", "parallel"),
        name="mixout_grid" if grid_conv else "mixout_seq",
    )(*args)


def _oddeven_merge_sort_pairs(n):
    pairs = []
    p = 1
    while p < n:
        k = p
        while k >= 1:
            for j in range(k % p, n - k, 2 * k):
                for i in range(min(k, n - j - k)):
                    if (i + j) // (2 * p) == (i + j + k) // (2 * p):
                        pairs.append((i + j, i + j + k))
            k //= 2
        p *= 2
    return pairs


_SORT16 = _oddeven_merge_sort_pairs(PEER_TOPK)


def _sort_desc(xs):
    xs = list(xs)
    for i, j in _SORT16:
        a, b = xs[i], xs[j]
        xs[i], xs[j] = jnp.maximum(a, b), jnp.minimum(a, b)
    return xs


def _bitonic_merge_desc(xs):
    xs = list(xs)
    d = len(xs) // 2
    while d >= 1:
        for i in range(len(xs)):
            if i & d == 0:
                a, b = xs[i], xs[i + d]
                xs[i], xs[i + d] = jnp.maximum(a, b), jnp.minimum(a, b)
        d //= 2
    return xs


def _top_of_union(a, b):
    k = len(a)
    return [jnp.maximum(a[r], b[k - 1 - r]) for r in range(k)]


def _topk_rows(s):
    rows = [s[SUBLANES * v:SUBLANES * (v + 1), :] for v in range(N_KEYS // SUBLANES)]
    rows = _sort_desc(rows)
    for shift in (4, 2, 1):
        rolled = [pltpu.roll(r, shift, 0) for r in rows]
        rows = _bitonic_merge_desc(_top_of_union(rows, rolled))
    return rows


def _peer_prep_kernel(x_ref, g_ref, sc_ref, sh_ref, wq_ref, keys_ref,
                      ht_ref, n1_ref, e1_ref, r2_ref, e2_ref, s1_ref, s2_ref):
    tq = x_ref.shape[1]
    hf = _norm_mod(x_ref[0], g_ref[...], sc_ref[0], sh_ref[0])
    ht_ref[...] = hf.T.astype(BF16)
    qf = jnp.dot(hf.astype(BF16), wq_ref[...], preferred_element_type=F32)
    sub = lax.broadcasted_iota(jnp.int32, (SUBLANES, tq), 0)
    neg = jnp.full((SUBLANES, tq), -jnp.inf, F32)
    packed = [[jnp.zeros((SUBLANES, tq), F32)] * PEER_TOPK for _ in range(2)]
    for hd in range(PEER_HEADS):
        for p in range(2):
            lo = hd * PEER_QDIM + p * PEER_HALF
            s = _dot_nt(keys_ref[hd, p], qf[:, lo:lo + PEER_HALF])
            (s1_ref if p == 0 else s2_ref)[hd] = s
            top = _topk_rows(s)
            packed[p] = [jnp.where(sub == hd, top[r], packed[p][r]) for r in range(PEER_TOPK)]
    c1, c2 = packed
    pairs = [(a, b) for a in range(PEER_TOPK) for b in range(PEER_TOPK)
             if (a + 1) * (b + 1) <= PEER_TOPK]
    cand = {ab: c1[ab[0]] + c2[ab[1]] for ab in pairs}
    cands = [cand[ab] for ab in pairs]
    cands += [neg] * (-len(cands) % PEER_TOPK)
    groups = [_sort_desc(cands[i:i + PEER_TOPK]) for i in range(0, len(cands), PEER_TOPK)]
    while len(groups) > 2:
        nxt = [_bitonic_merge_desc(_top_of_union(groups[i], groups[i + 1]))
               for i in range(0, len(groups) - 1, 2)]
        if len(groups) % 2:
            nxt.append(groups[-1])
        groups = nxt
    top = _top_of_union(groups[0], groups[1]) if len(groups) == 2 else groups[0]
    tau = functools.reduce(jnp.minimum, top)
    m1, m2 = c1[0], c2[0]
    mx = m1 + m2
    z = functools.reduce(lambda a, b: a + b, [jnp.exp(t - mx) for t in top])
    inv_z = 1.0 / z
    counts = []
    for a in range(PEER_TOPK):
        n = jnp.zeros((SUBLANES, tq), F32)
        for b in range(PEER_TOPK):
            if (a, b) in cand:
                n = n + jnp.where(cand[(a, b)] >= tau, 1.0, 0.0)
        counts.append(n)
    grouped = (N_KEYS // SUBLANES, SUBLANES, tq)
    for hd in range(PEER_HEADS):
        s1 = s1_ref[hd]
        s2 = s2_ref[hd]
        n1 = jnp.zeros((N_KEYS, tq), F32)
        rank2 = jnp.full((N_KEYS, tq), float(PEER_TOPK), F32)
        for r in reversed(range(PEER_TOPK)):
            n1 = jnp.where(s1 == c1[r][hd:hd + 1, :], counts[r][hd:hd + 1, :], n1)
            rank2 = jnp.where(s2 == c2[r][hd:hd + 1, :], float(r), rank2)
        n1_ref[hd] = n1.reshape(grouped)
        r2_ref[hd] = pltpu.bitcast(rank2.astype(BF16), jnp.uint32)
        e1 = jnp.exp(s1 - m1[hd:hd + 1, :]) * inv_z[hd:hd + 1, :]
        e1_ref[hd] = e1.reshape(grouped)
        e2_ref[hd] = pltpu.bitcast(jnp.exp(s2 - m2[hd:hd + 1, :]).astype(BF16), jnp.uint32)


def _peer_prep_call(x, g, scale, shift, wq, keys, l):
    bsz, t, d = x.shape
    tq = min(256, t)
    nt = t // tq
    ttot = bsz * t
    score_spec = pl.BlockSpec((PEER_HEADS, N_KEYS // 2, tq), lambda b, i: (0, 0, b * nt + i))
    score_shape = jax.ShapeDtypeStruct((PEER_HEADS, N_KEYS // 2, ttot), jnp.uint32)
    n_grp = N_KEYS // SUBLANES
    grouped_spec = pl.BlockSpec((PEER_HEADS, n_grp, SUBLANES, tq), lambda b, i: (0, 0, 0, b * nt + i))
    grouped_shape = jax.ShapeDtypeStruct((PEER_HEADS, n_grp, SUBLANES, ttot), F32)
    return pl.pallas_call(
        _peer_prep_kernel,
        grid=(bsz, nt),
        in_specs=[
            pl.BlockSpec((1, tq, d), lambda b, i: (b, i, 0)),
            pl.BlockSpec((1, d), lambda b, i: (0, 0)),
            pl.BlockSpec((1, 1, d), lambda b, i: (b, 0, 0)),
            pl.BlockSpec((1, 1, d), lambda b, i: (b, 0, 0)),
            pl.BlockSpec((None,) + wq.shape[1:], lambda b, i: (l, 0, 0)),
            pl.BlockSpec((None,) + keys.shape[1:], lambda b, i: (l, 0, 0, 0, 0)),
        ],
        out_specs=[pl.BlockSpec((d, tq), lambda b, i: (0, b * nt + i)),
                   grouped_spec, grouped_spec, score_spec, score_spec],
        out_shape=[jax.ShapeDtypeStruct((d, ttot), BF16),
                   grouped_shape, grouped_shape, score_shape, score_shape],
        scratch_shapes=[pltpu.VMEM((PEER_HEADS, N_KEYS, tq), F32),
                        pltpu.VMEM((PEER_HEADS, N_KEYS, tq), F32)],
        compiler_params=_params("parallel", "parallel"),
        name="peer_prep",
    )(x, g, scale, shift, wq, keys)


PEER_TE = SUBLANES * N_KEYS
SQRT_HALF = math.sqrt(0.5)


def _peer_dense_kernel(ht_ref, u_ref, vt_ref, n1_ref, e1_ref, r2_ref, e2_ref,
                       x_ref, g2_ref, o_ref, acc_ref, a_ref, gt_ref, *, n_e, n_steps):
    s = pl.program_id(0)
    tt = ht_ref.shape[1]
    n_lane_tiles = tt // LANES
    n_blk = N_KEYS // PACKED_ROWS

    @pl.when(s == 0)
    def _():
        a_ref[...] = jnp.zeros_like(a_ref)
        gt_ref[...] = jnp.zeros_like(gt_ref)

    @pl.when(jnp.logical_or(s == 0, (s - 2) % n_e == 0))
    def _():
        acc_ref[...] = jnp.zeros_like(acc_ref)

    acc_ref[...] += jnp.dot(vt_ref[0], gt_ref[...], preferred_element_type=F32)

    grp = jnp.clip(s - 1, 0, n_steps - 1) % n_e
    packed = (PACKED_ROWS, LANES)
    for il in range(SUBLANES):
        for j in range(n_lane_tiles):
            lanes = pl.ds(j * LANES, LANES)
            acc = [None] * n_blk
            for hd in range(PEER_HEADS):
                n1b = jnp.broadcast_to(n1_ref[hd, grp, pl.ds(il, 1), lanes], packed).astype(BF16)
                e1b = jnp.broadcast_to(e1_ref[hd, grp, pl.ds(il, 1), lanes], packed).astype(BF16)
                for k in range(n_blk):
                    rows = pl.ds(k * SUBLANES, SUBLANES)
                    rank2 = pltpu.bitcast(r2_ref[hd, rows, lanes], BF16)
                    e2 = pltpu.bitcast(e2_ref[hd, rows, lanes], BF16)
                    w = jnp.where(rank2 < n1b, e2, 0.0) * e1b
                    acc[k] = w if hd == 0 else acc[k] + w
            for k in range(n_blk):
                rows = pl.ds(il * N_KEYS + k * PACKED_ROWS, PACKED_ROWS)
                a = a_ref[rows, lanes]
                gelu = (0.5 * a) * (1.0 + lax.erf(a * SQRT_HALF))
                gt_ref[rows, lanes] = acc[k] * gelu

    a_ref[...] = jnp.dot(u_ref[...], ht_ref[...], preferred_element_type=F32).astype(BF16)

    @pl.when(jnp.logical_and(s >= 2, (s - 2) % n_e == n_e - 1))
    def _():
        o_ref[...] = x_ref[...] + g2_ref[0] * acc_ref[...].T


def _peer_dense_call(ht, u, vt, n1, e1, r2, e2, x, g2, t_per_batch, l):
    ttot, d = x.shape
    tt = min(512, t_per_batch)
    n_e = N_EXPERTS // PEER_TE
    n_steps = (ttot // tt) * n_e
    per_batch = t_per_batch // tt
    tile = lambda s, lag: jnp.clip(s - lag, 0, n_steps - 1) // n_e
    group = lambda s, lag: jnp.clip(s - lag, 0, n_steps - 1) % n_e
    score_spec = pl.BlockSpec((PEER_HEADS, N_KEYS // 2, tt), lambda s: (0, 0, tile(s, 1)))
    grouped_spec = pl.BlockSpec((PEER_HEADS, N_KEYS // SUBLANES, SUBLANES, tt),
                                lambda s: (0, 0, 0, tile(s, 1)))
    return pl.pallas_call(
        functools.partial(_peer_dense_kernel, n_e=n_e, n_steps=n_steps),
        grid=(n_steps + 2,),
        in_specs=[
            pl.BlockSpec((d, tt), lambda s: (0, tile(s, 0))),
            pl.BlockSpec((None, PEER_TE, d), lambda s: (l, group(s, 0), 0)),
            pl.BlockSpec((None, 1, d, PEER_TE), lambda s: (l, group(s, 2), 0, 0)),
            grouped_spec, grouped_spec, score_spec, score_spec,
            pl.BlockSpec((tt, d), lambda s: (tile(s, 2), 0)),
            pl.BlockSpec((1, 1, d), lambda s: (tile(s, 2) // per_batch, 0, 0)),
        ],
        out_specs=pl.BlockSpec((tt, d), lambda s: (tile(s, 2), 0)),
        out_shape=jax.ShapeDtypeStruct((ttot, d), F32),
        scratch_shapes=[pltpu.VMEM((d, tt), F32),
                        pltpu.VMEM((PEER_TE, tt), BF16),
                        pltpu.VMEM((PEER_TE, tt), BF16)],
        compiler_params=_params("arbitrary"),
        name="peer_dense",
    )(ht, u, vt, n1, e1, r2, e2, x, g2)


def _final_norm_kernel(x_ref, g_ref, o_ref):
    x = x_ref[...]
    ms = jnp.mean(x * x, axis=-1, keepdims=True)
    o_ref[...] = x * lax.rsqrt(ms + EPS) * g_ref[...]


def _final_norm_call(x, g):
    n, d = x.shape
    tm = 1024
    return pl.pallas_call(
        _final_norm_kernel,
        grid=(n // tm,),
        in_specs=[pl.BlockSpec((tm, d), lambda i: (i, 0)), pl.BlockSpec((1, d), lambda i: (0, 0))],
        out_specs=pl.BlockSpec((tm, d), lambda i: (i, 0)),
        out_shape=jax.ShapeDtypeStruct((n, d), F32),
        compiler_params=_params("parallel"),
        name="final_norm",
    )(x, g)


def _mixer(x, mod, l, norm1_g, w_in, conv_w, w_out, lbf, lbb, s0f, s0b, grid_conv, full):
    sh1, sc1, g1 = mod[0], mod[1], mod[2]
    proj = _proj_call(x, norm1_g[l][None, :], sc1, sh1, w_in, l)
    o_f, o_b, s_f, s_b = _gla_call(proj, lbf, lbb, s0f, s0b)
    if not full:
        return None, s_f, s_b
    x = _mixout_call(o_f, o_b, proj, conv_w[l], w_out[l], x, g1, grid_conv)
    return x, s_f, s_b


def _peer(x, mod, l, norm2_g, wq, keys, u, vt):
    bsz, t, d = x.shape
    sh2, sc2, g2 = mod[3], mod[4], mod[5]
    ht, n1, e1, r2, e2 = _peer_prep_call(x, norm2_g[l][None, :], sc2, sh2, wq, keys, l)
    out = _peer_dense_call(ht, u, vt, n1, e1, r2, e2, x.reshape(bsz * t, d), g2, t, l)
    return out.reshape(bsz, t, d)


def kernel(x, c, ctx, c_ctx, w_mod, b_mod, norm1_g, norm2_g, w_in, conv_w, w_out, lb_logits,
           peer_wq, peer_subkeys, peer_u, peer_v, final_g):
    bsz, t, d = x.shape
    depth = w_mod.shape[0]

    p_lb = jax.nn.softmax(lb_logits.astype(F32), axis=0)
    lower = jnp.cumsum(p_lb, axis=0) - p_lb[0]

    cond = jnp.zeros((COND_ROWS, d), F32).at[:bsz].set(c).at[bsz].set(c_ctx)
    mod = _mod_call(cond, w_mod, b_mod)
    mod = mod.reshape(depth, COND_ROWS, N_MOD, d)
    mod_x = jnp.transpose(mod[:, :bsz], (0, 2, 1, 3))[:, :, :, None, :]
    mod_c = jnp.broadcast_to(mod[:, bsz][:, :, None, None, :], mod_x.shape)

    w_in_b = w_in.astype(BF16)
    w_out_b = w_out.astype(BF16)
    wq_b = peer_wq.astype(BF16)
    keys_b = peer_subkeys.astype(BF16)
    u_b = peer_u.astype(BF16)
    vt_b = jnp.swapaxes(peer_v.astype(BF16).reshape(depth, N_EXPERTS // PEER_TE, PEER_TE, d), 2, 3)

    zero_state = jnp.zeros((bsz, HG_HEADS, HG_DIM, HG_DIM), F32)
    xc = ctx
    for l in range(depth):
        lbf = lower[l, 0][None, :]
        lbb = lower[l, 1][None, :]
        full = l < depth - 1
        xc_new, s_f, s_b = _mixer(xc, mod_c[l], l, norm1_g, w_in_b, conv_w, w_out_b, lbf, lbb,
                                  zero_state, zero_state, False, full)
        if full:
            xc = _peer(xc_new, mod_c[l], l, norm2_g, wq_b, keys_b, u_b, vt_b)
        x, _, _ = _mixer(x, mod_x[l], l, norm1_g, w_in_b, conv_w, w_out_b, lbf, lbb,
                         s_f, s_b, True, True)
        x = _peer(x, mod_x[l], l, norm2_g, wq_b, keys_b, u_b, vt_b)
    return _final_norm_call(x.reshape(bsz * t, d), final_g[None, :]).reshape(bsz, t, d)
```

```python
import functools
import math

import numpy as np
import jax
import jax.numpy as jnp
from jax import lax
from jax.experimental import pallas as pl
from jax.experimental.pallas import tpu as pltpu

F32 = jnp.float32
BF16 = jnp.bfloat16

D_MODEL = 1024
GRID_W = 64
EPS = 1e-6
F_FLOOR = 1e-20
N_MOD = 6
HG_WIDTH = 512
HG_HEADS = 4
HG_DIM = HG_WIDTH // HG_HEADS
SC_WIDTH = 512
SC_HALF = SC_WIDTH // 2
IN_COLS = 5 * HG_WIDTH + 3 * SC_WIDTH
PEER_HEADS = 8
PEER_QDIM = 256
PEER_HALF = PEER_QDIM // 2
N_KEYS = 128
N_EXPERTS = N_KEYS * N_KEYS
PEER_TOPK = 16

SUBLANES = 8
LANES = 128
PACKED_ROWS = 2 * SUBLANES
VMEM_LIMIT = 48 * 1024 * 1024

GLA_CHUNK = 128
GLA_LEVELS = (64, 32, 16, 8, 4, 2, 1)
GLA_GROUP_HEADS = 2
COND_ROWS = 8

COL_IV, COL_ZF, COL_ZB, COL_Q, COL_G, COL_CG, COL_BG, COL_HV = range(8)


def _params(*sem):
    return pltpu.CompilerParams(dimension_semantics=sem, vmem_limit_bytes=VMEM_LIMIT)


def _dot(a, b):
    return jnp.dot(a.astype(BF16), b.astype(BF16), preferred_element_type=F32)


def _dot_nt(a, b):
    return lax.dot_general(a.astype(BF16), b.astype(BF16), (((1,), (1,)), ((), ())),
                           preferred_element_type=F32)


def _dot_tn(a, b):
    return lax.dot_general(a.astype(BF16), b.astype(BF16), (((0,), (0,)), ((), ())),
                           preferred_element_type=F32)


def _mod_kernel(cond_ref, w_ref, b_ref, o_ref):
    c = cond_ref[...]
    s = c * jax.nn.sigmoid(c)
    o_ref[0] = _dot(s, w_ref[0]) + b_ref[0]


def _mod_call(cond, w_mod, b_mod):
    depth, d, n = w_mod.shape
    tn = 1536
    return pl.pallas_call(
        _mod_kernel,
        grid=(depth, n // tn),
        in_specs=[
            pl.BlockSpec((COND_ROWS, d), lambda l, j: (0, 0)),
            pl.BlockSpec((1, d, tn), lambda l, j: (l, 0, j)),
            pl.BlockSpec((1, 1, tn), lambda l, j: (l, 0, j)),
        ],
        out_specs=pl.BlockSpec((1, COND_ROWS, tn), lambda l, j: (l, 0, j)),
        out_shape=jax.ShapeDtypeStruct((depth, COND_ROWS, n), F32),
        compiler_params=_params("parallel", "parallel"),
        name="mod",
    )(cond, w_mod, b_mod.reshape(depth, 1, n))


def _norm_mod(x, g, scale, shift):
    ms = jnp.mean(x * x, axis=-1, keepdims=True)
    y = x * lax.rsqrt(ms + EPS) * g
    return y * (1.0 + scale) + shift


def _proj_kernel(x_ref, g_ref, sc_ref, sh_ref, w_ref, o_ref):
    h = _norm_mod(x_ref[0], g_ref[...], sc_ref[0], sh_ref[0])
    o_ref[0] = _dot(h, w_ref[...])


def _proj_call(x, g, scale, shift, w, l):
    bsz, t, d = x.shape
    n = w.shape[2]
    tm = min(256, t)
    return pl.pallas_call(
        _proj_kernel,
        grid=(bsz, t // tm),
        in_specs=[
            pl.BlockSpec((1, tm, d), lambda b, i: (b, i, 0)),
            pl.BlockSpec((1, d), lambda b, i: (0, 0)),
            pl.BlockSpec((1, 1, d), lambda b, i: (b, 0, 0)),
            pl.BlockSpec((1, 1, d), lambda b, i: (b, 0, 0)),
            pl.BlockSpec((None, d, n), lambda b, i: (l, 0, 0)),
        ],
        out_specs=pl.BlockSpec((1, tm, n), lambda b, i: (b, i, 0)),
        out_shape=jax.ShapeDtypeStruct((bsz, t, n), F32),
        compiler_params=_params("parallel", "parallel"),
        name="proj",
    )(x, g, scale, shift, w)


def _gla_sum_matrix(backward):
    c = GLA_CHUNK
    t = np.arange(c)[:, None]
    s = np.arange(c)[None, :]
    mats = [s >= t if backward else s <= t]
    for n in GLA_LEVELS:
        if n >= SUBLANES:
            continue
        mid = (t // (2 * n)) * (2 * n) + n
        if not backward:
            m = np.where(t >= mid, (s >= mid) & (s <= t), (s > t) & (s < mid))
        else:
            m = np.where(t < mid, (s >= t) & (s < mid), (s >= mid) & (s < t))
        mats.append(m)
    return np.concatenate([m.astype(np.float32) for m in mats], axis=0)


def _gla_masks(backward):
    c = GLA_CHUNK
    row = lax.broadcasted_iota(jnp.int32, (c, c), 0)
    col = lax.broadcasted_iota(jnp.int32, (c, c), 1)
    qrow, lvl = [], []
    for n in GLA_LEVELS:
        r_hi = (row & (2 * n - 1)) >= n
        c_hi = (col & (2 * n - 1)) >= n
        same = (row & ~(2 * n - 1)) == (col & ~(2 * n - 1))
        if not backward:
            qrow.append(r_hi)
            lvl.append(same & r_hi & jnp.logical_not(c_hi))
        else:
            qrow.append(jnp.logical_not(r_hi))
            lvl.append(same & jnp.logical_not(r_hi) & c_hi)
    return qrow, lvl, row == col


def _gla_gates(z, lb, msum):
    sig = jax.nn.sigmoid(z)
    f = lb + (1.0 - lb) * sig
    logf = jnp.log(jnp.maximum(f, F_FLOOR))
    kk = (1.0 - lb) * jax.nn.sigmoid(-z)
    sums = jnp.dot(msum, logf.astype(BF16), preferred_element_type=F32)
    return kk, sums


def _boundary_rows(b, n, backward):
    c, w = b.shape
    parts = []
    for start in range(0, c, 2 * n):
        r = start + n if backward else start + n - 1
        parts.append(jnp.broadcast_to(b[r:r + 1, :], (2 * n, w)))
    return jnp.concatenate(parts, axis=0)


def _gla_chains(chains, ones):
    c = GLA_CHUNK
    atts = [jnp.where(masks[2], jnp.dot((q * kk).astype(BF16), ones, preferred_element_type=F32), 0.0)
            for q, v, kk, sums, st, masks, backward in chains]
    sub_tile = [n for n in GLA_LEVELS if n < SUBLANES]
    for li, n in enumerate(GLA_LEVELS):
        for ci, (q, v, kk, sums, st, masks, backward) in enumerate(chains):
            if n >= SUBLANES:
                exponent = -jnp.abs(sums[0:c] - _boundary_rows(sums[0:c], n, backward))
            else:
                k = 1 + sub_tile.index(n)
                exponent = sums[k * c:(k + 1) * c]
            mixed = (jnp.where(masks[0][li][:, :HG_DIM], q, kk) * jnp.exp(exponent)).astype(BF16)
            a = lax.dot_general(mixed, mixed, (((1,), (1,)), ((), ())), preferred_element_type=F32)
            atts[ci] = jnp.where(masks[1][li], a, atts[ci])
    out = []
    for att, (q, v, kk, sums, st, masks, backward) in zip(atts, chains):
        b = sums[0:c]
        b_end = b[0:1, :] if backward else b[c - 1:c, :]
        q_in = q * jnp.exp(b)
        o = _dot(att, v) + _dot_nt(q_in, st)
        k_out = kk * jnp.exp(b_end - b)
        out.append((o, jnp.exp(b_end) * st + _dot_tn(v, k_out)))
    return out


def _gla_kernel(ivf_ref, zf_ref, qf_ref, ivb_ref, zb_ref, qb_ref, lbf_ref, lbb_ref,
                s0f_ref, s0b_ref, mf_ref, mb_ref,
                of_ref, ob_ref, sf_ref, sb_ref, st_ref):
    n = pl.program_id(1)

    @pl.when(n == 0)
    def _():
        st_ref[0] = s0f_ref[0]
        st_ref[1] = s0b_ref[0]

    ones = jnp.ones((HG_DIM, HG_DIM), BF16)
    masks_f = _gla_masks(False)
    masks_b = _gla_masks(True)
    kk_f, sums_f = _gla_gates(zf_ref[0], lbf_ref[...], mf_ref[...])
    kk_b, sums_b = _gla_gates(zb_ref[0], lbb_ref[...], mb_ref[...])
    for h0 in range(0, HG_HEADS, GLA_GROUP_HEADS):
        heads = range(h0, h0 + GLA_GROUP_HEADS)
        sls = [slice(h * HG_DIM, (h + 1) * HG_DIM) for h in heads]
        chains = [(qf_ref[0, :, sl], ivf_ref[0, :, sl], kk_f[:, sl], sums_f[:, sl],
                   st_ref[0, h], masks_f, False) for h, sl in zip(heads, sls)]
        chains += [(qb_ref[0, :, sl], ivb_ref[0, :, sl], kk_b[:, sl], sums_b[:, sl],
                    st_ref[1, h], masks_b, True) for h, sl in zip(heads, sls)]
        res = _gla_chains(chains, ones)
        for i, (h, sl) in enumerate(zip(heads, sls)):
            of_ref[0, :, sl], st_ref[0, h] = res[i]
            ob_ref[0, :, sl], st_ref[1, h] = res[GLA_GROUP_HEADS + i]

    @pl.when(n == pl.num_programs(1) - 1)
    def _():
        sf_ref[0] = st_ref[0]
        sb_ref[0] = st_ref[1]


def _gla_call(proj, lbf, lbb, s0f, s0b):
    bsz, t, _ = proj.shape
    c = GLA_CHUNK
    nc = t // c
    w = HG_WIDTH
    msum_f = jnp.asarray(_gla_sum_matrix(False), BF16)
    msum_b = jnp.asarray(_gla_sum_matrix(True), BF16)

    def fwd(col):
        return pl.BlockSpec((1, c, w), lambda b, n: (b, n, col))

    def bwd(col):
        return pl.BlockSpec((1, c, w), lambda b, n: (b, nc - 1 - n, col))

    state_spec = pl.BlockSpec((1, HG_HEADS, HG_DIM, HG_DIM), lambda b, n: (b, 0, 0, 0))
    const = lambda shape: pl.BlockSpec(shape, lambda b, n: (0,) * len(shape))
    return pl.pallas_call(
        _gla_kernel,
        grid=(bsz, nc),
        in_specs=[fwd(COL_IV), fwd(COL_ZF), fwd(COL_Q), bwd(COL_IV), bwd(COL_ZB), bwd(COL_Q),
                  const((1, w)), const((1, w)), state_spec, state_spec,
                  const(msum_f.shape), const(msum_b.shape)],
        out_specs=[pl.BlockSpec((1, c, w), lambda b, n: (b, n, 0)),
                   pl.BlockSpec((1, c, w), lambda b, n: (b, nc - 1 - n, 0)),
                   state_spec, state_spec],
        out_shape=[jax.ShapeDtypeStruct((bsz, t, w), F32),
                   jax.ShapeDtypeStruct((bsz, t, w), F32),
                   jax.ShapeDtypeStruct((bsz, HG_HEADS, HG_DIM, HG_DIM), F32),
                   jax.ShapeDtypeStruct((bsz, HG_HEADS, HG_DIM, HG_DIM), F32)],
        scratch_shapes=[pltpu.VMEM((2, HG_HEADS, HG_DIM, HG_DIM), F32)],
        compiler_params=_params("parallel", "arbitrary"),
        name="gla",
    )(proj, proj, proj, proj, proj, proj, lbf, lbb, s0f, s0b, msum_f, msum_b)


def _mixout_body(of_ref, ob_ref, g_ref, cg_ref, bg_ref, hv_ref, halo, cw_ref, wo_ref,
                 x_ref, g1_ref, o_ref, grid_conv):
    tm = of_ref.shape[1]
    o = of_ref[0] + ob_ref[0]
    heads = []
    for h in range(HG_HEADS):
        oh = o[:, h * HG_DIM:(h + 1) * HG_DIM]
        heads.append(oh * lax.rsqrt(jnp.mean(oh * oh, axis=-1, keepdims=True) + EPS))
    g = g_ref[0]
    o_rec = jnp.concatenate(heads, axis=-1) * (g * jax.nn.sigmoid(g))

    u = cg_ref[0] * hv_ref[0]
    w0, w1, w2 = cw_ref[0:1, :], cw_ref[1:2, :], cw_ref[2:3, :]
    pos = lax.broadcasted_iota(jnp.int32, (tm, 1), 0)
    if grid_conv:
        cgp_ref, hvp_ref, cgn_ref, hvn_ref = halo
        i = pl.program_id(1)
        last = pl.num_programs(1) - 1
        colpos = pos & (GRID_W - 1)
        uh = u[:, :SC_HALF]
        left = jnp.where(colpos == 0, 0.0, pltpu.roll(uh, 1, 0))
        right = jnp.where(colpos == GRID_W - 1, 0.0, pltpu.roll(uh, tm - 1, 0))
        conv_h = w0[:, :SC_HALF] * left + w1[:, :SC_HALF] * uh + w2[:, :SC_HALF] * right
        uv = u[:, SC_HALF:]
        up_halo = jnp.where(i == 0, 0.0, cgp_ref[0][:, SC_HALF:] * hvp_ref[0][:, SC_HALF:])
        dn_halo = jnp.where(i == last, 0.0, cgn_ref[0][:, SC_HALF:] * hvn_ref[0][:, SC_HALF:])
        up = jnp.concatenate([up_halo, uv[:tm - GRID_W]], axis=0)
        down = jnp.concatenate([uv[GRID_W:], dn_halo], axis=0)
        conv_v = w0[:, SC_HALF:] * up + w1[:, SC_HALF:] * uv + w2[:, SC_HALF:] * down
        conv = jnp.concatenate([conv_h, conv_v], axis=-1)
    else:
        left = jnp.where(pos == 0, 0.0, pltpu.roll(u, 1, 0))
        right = jnp.where(pos == tm - 1, 0.0, pltpu.roll(u, tm - 1, 0))
        conv = w0 * left + w1 * u + w2 * right
    o_conv = bg_ref[0] * conv
    y = _dot(o_rec, wo_ref[0:HG_WIDTH, :]) + _dot(o_conv, wo_ref[HG_WIDTH:, :])
    o_ref[0] = x_ref[0] + g1_ref[0] * y


def _mixout_grid_kernel(of_ref, ob_ref, g_ref, cg_ref, bg_ref, hv_ref, cgp_ref, hvp_ref,
                        cgn_ref, hvn_ref, cw_ref, wo_ref, x_ref, g1_ref, o_ref):
    _mixout_body(of_ref, ob_ref, g_ref, cg_ref, bg_ref, hv_ref,
                 (cgp_ref, hvp_ref, cgn_ref, hvn_ref), cw_ref, wo_ref, x_ref, g1_ref, o_ref, True)


def _mixout_seq_kernel(of_ref, ob_ref, g_ref, cg_ref, bg_ref, hv_ref, cw_ref, wo_ref,
                       x_ref, g1_ref, o_ref):
    _mixout_body(of_ref, ob_ref, g_ref, cg_ref, bg_ref, hv_ref, None, cw_ref, wo_ref,
                 x_ref, g1_ref, o_ref, False)


def _mixout_call(o_f, o_b, proj, conv_w, w_out, x, g1, grid_conv):
    bsz, t, d = x.shape
    w = HG_WIDTH
    tm = 512 if grid_conv else t
    nt = t // tm
    hb = tm // GRID_W
    nhalo = t // GRID_W

    def col(cidx):
        return pl.BlockSpec((1, tm, w), lambda b, i: (b, i, cidx))

    def prev(cidx):
        return pl.BlockSpec((1, GRID_W, w), lambda b, i: (b, jnp.maximum(i * hb - 1, 0), cidx))

    def nxt(cidx):
        return pl.BlockSpec((1, GRID_W, w),
                            lambda b, i: (b, jnp.minimum((i + 1) * hb, nhalo - 1), cidx))

    row = pl.BlockSpec((1, tm, w), lambda b, i: (b, i, 0))
    in_specs = [row, row, col(COL_G), col(COL_CG), col(COL_BG), col(COL_HV)]
    args = [o_f, o_b, proj, proj, proj, proj]
    if grid_conv:
        in_specs += [prev(COL_CG), prev(COL_HV), nxt(COL_CG), nxt(COL_HV)]
        args += [proj, proj, proj, proj]
    in_specs += [
        pl.BlockSpec((3, w), lambda b, i: (0, 0)),
        pl.BlockSpec(w_out.shape, lambda b, i: (0, 0)),
        pl.BlockSpec((1, tm, d), lambda b, i: (b, i, 0)),
        pl.BlockSpec((1, 1, d), lambda b, i: (b, 0, 0)),
    ]
    args += [conv_w, w_out, x, g1]
    return pl.pallas_call(
        _mixout_grid_kernel if grid_conv else _mixout_seq_kernel,
        grid=(bsz, nt),
        in_specs=in_specs,
        out_specs=pl.BlockSpec((1, tm, d), lambda b, i: (b, i, 0)),
        out_shape=jax.ShapeDtypeStruct((bsz, t, d), F32),
        compiler_params=_params("parallel", "parallel"),
        name="mixout_grid" if grid_conv else "mixout_seq",
    )(*args)


def _oddeven_merge_sort_pairs(n):
    pairs = []
    p = 1
    while p < n:
        k = p
        while k >= 1:
            for j in range(k % p, n - k, 2 * k):
                for i in range(min(k, n - j - k)):
                    if (i + j) // (2 * p) == (i + j + k) // (2 * p):
                        pairs.append((i + j, i + j + k))
            k //= 2
        p *= 2
    return pairs


_SORT16 = _oddeven_merge_sort_pairs(PEER_TOPK)


def _sort_desc(xs):
    xs = list(xs)
    for i, j in _SORT16:
        a, b = xs[i], xs[j]
        xs[i], xs[j] = jnp.maximum(a, b), jnp.minimum(a, b)
    return xs


def _bitonic_merge_desc(xs):
    xs = list(xs)
    d = len(xs) // 2
    while d >= 1:
        for i in range(len(xs)):
            if i & d == 0:
                a, b = xs[i], xs[i + d]
                xs[i], xs[i + d] = jnp.maximum(a, b), jnp.minimum(a, b)
        d //= 2
    return xs


def _top_of_union(a, b):
    k = len(a)
    return [jnp.maximum(a[r], b[k - 1 - r]) for r in range(k)]


def _topk_rows(s):
    rows = [s[SUBLANES * v:SUBLANES * (v + 1), :] for v in range(N_KEYS // SUBLANES)]
    rows = _sort_desc(rows)
    for shift in (4, 2, 1):
        rolled = [pltpu.roll(r, shift, 0) for r in rows]
        rows = _bitonic_merge_desc(_top_of_union(rows, rolled))
    return rows


def _peer_prep_kernel(x_ref, g_ref, sc_ref, sh_ref, wq_ref, keys_ref,
                      ht_ref, n1_ref, e1_ref, r2_ref, e2_ref, s1_ref, s2_ref):
    tq = x_ref.shape[1]
    hf = _norm_mod(x_ref[0], g_ref[...], sc_ref[0], sh_ref[0])
    ht_ref[...] = hf.T.astype(BF16)
    qf = jnp.dot(hf.astype(BF16), wq_ref[...], preferred_element_type=F32)
    sub = lax.broadcasted_iota(jnp.int32, (SUBLANES, tq), 0)
    neg = jnp.full((SUBLANES, tq), -jnp.inf, F32)
    packed = [[jnp.zeros((SUBLANES, tq), F32)] * PEER_TOPK for _ in range(2)]
    for hd in range(PEER_HEADS):
        for p in range(2):
            lo = hd * PEER_QDIM + p * PEER_HALF
            s = _dot_nt(keys_ref[hd, p], qf[:, lo:lo + PEER_HALF])
            (s1_ref if p == 0 else s2_ref)[hd] = s
            top = _topk_rows(s)
            packed[p] = [jnp.where(sub == hd, top[r], packed[p][r]) for r in range(PEER_TOPK)]
    c1, c2 = packed
    pairs = [(a, b) for a in range(PEER_TOPK) for b in range(PEER_TOPK)
             if (a + 1) * (b + 1) <= PEER_TOPK]
    cand = {ab: c1[ab[0]] + c2[ab[1]] for ab in pairs}
    cands = [cand[ab] for ab in pairs]
    cands += [neg] * (-len(cands) % PEER_TOPK)
    groups = [_sort_desc(cands[i:i + PEER_TOPK]) for i in range(0, len(cands), PEER_TOPK)]
    while len(groups) > 2:
        nxt = [_bitonic_merge_desc(_top_of_union(groups[i], groups[i + 1]))
               for i in range(0, len(groups) - 1, 2)]
        if len(groups) % 2:
            nxt.append(groups[-1])
        groups = nxt
    top = _top_of_union(groups[0], groups[1]) if len(groups) == 2 else groups[0]
    tau = functools.reduce(jnp.minimum, top)
    m1, m2 = c1[0], c2[0]
    mx = m1 + m2
    z = functools.reduce(lambda a, b: a + b, [jnp.exp(t - mx) for t in top])
    inv_z = 1.0 / z
    counts = []
    for a in range(PEER_TOPK):
        n = jnp.zeros((SUBLANES, tq), F32)
        for b in range(PEER_TOPK):
            if (a, b) in cand:
                n = n + jnp.where(cand[(a, b)] >= tau, 1.0, 0.0)
        counts.append(n)
    grouped = (N_KEYS // SUBLANES, SUBLANES, tq)
    for hd in range(PEER_HEADS):
        s1 = s1_ref[hd]
        s2 = s2_ref[hd]
        n1 = jnp.zeros((N_KEYS, tq), F32)
        rank2 = jnp.full((N_KEYS, tq), float(PEER_TOPK), F32)
        for r in reversed(range(PEER_TOPK)):
            n1 = jnp.where(s1 == c1[r][hd:hd + 1, :], counts[r][hd:hd + 1, :], n1)
            rank2 = jnp.where(s2 == c2[r][hd:hd + 1, :], float(r), rank2)
        n1_ref[hd] = n1.reshape(grouped)
        r2_ref[hd] = pltpu.bitcast(rank2.astype(BF16), jnp.uint32)
        e1 = jnp.exp(s1 - m1[hd:hd + 1, :]) * inv_z[hd:hd + 1, :]
        e1_ref[hd] = e1.reshape(grouped)
        e2_ref[hd] = pltpu.bitcast(jnp.exp(s2 - m2[hd:hd + 1, :]).astype(BF16), jnp.uint32)


def _peer_prep_call(x, g, scale, shift, wq, keys, l):
    bsz, t, d = x.shape
    tq = min(256, t)
    nt = t // tq
    ttot = bsz * t
    score_spec = pl.BlockSpec((PEER_HEADS, N_KEYS // 2, tq), lambda b, i: (0, 0, b * nt + i))
    score_shape = jax.ShapeDtypeStruct((PEER_HEADS, N_KEYS // 2, ttot), jnp.uint32)
    n_grp = N_KEYS // SUBLANES
    grouped_spec = pl.BlockSpec((PEER_HEADS, n_grp, SUBLANES, tq), lambda b, i: (0, 0, 0, b * nt + i))
    grouped_shape = jax.ShapeDtypeStruct((PEER_HEADS, n_grp, SUBLANES, ttot), F32)
    return pl.pallas_call(
        _peer_prep_kernel,
        grid=(bsz, nt),
        in_specs=[
            pl.BlockSpec((1, tq, d), lambda b, i: (b, i, 0)),
            pl.BlockSpec((1, d), lambda b, i: (0, 0)),
            pl.BlockSpec((1, 1, d), lambda b, i: (b, 0, 0)),
            pl.BlockSpec((1, 1, d), lambda b, i: (b, 0, 0)),
            pl.BlockSpec((None,) + wq.shape[1:], lambda b, i: (l, 0, 0)),
            pl.BlockSpec((None,) + keys.shape[1:], lambda b, i: (l, 0, 0, 0, 0)),
        ],
        out_specs=[pl.BlockSpec((d, tq), lambda b, i: (0, b * nt + i)),
                   grouped_spec, grouped_spec, score_spec, score_spec],
        out_shape=[jax.ShapeDtypeStruct((d, ttot), BF16),
                   grouped_shape, grouped_shape, score_shape, score_shape],
        scratch_shapes=[pltpu.VMEM((PEER_HEADS, N_KEYS, tq), F32),
                        pltpu.VMEM((PEER_HEADS, N_KEYS, tq), F32)],
        compiler_params=_params("parallel", "parallel"),
        name="peer_prep",
    )(x, g, scale, shift, wq, keys)


PEER_TE = SUBLANES * N_KEYS
SQRT_HALF = math.sqrt(0.5)


def _peer_dense_kernel(ht_ref, u_ref, vt_ref, n1_ref, e1_ref, r2_ref, e2_ref,
                       x_ref, g2_ref, o_ref, acc_ref, a_ref, gt_ref, *, n_e, n_steps):
    s = pl.program_id(0)
    tt = ht_ref.shape[1]
    n_lane_tiles = tt // LANES
    n_blk = N_KEYS // PACKED_ROWS

    @pl.when(s == 0)
    def _():
        a_ref[...] = jnp.zeros_like(a_ref)
        gt_ref[...] = jnp.zeros_like(gt_ref)

    @pl.when(jnp.logical_or(s == 0, (s - 2) % n_e == 0))
    def _():
        acc_ref[...] = jnp.zeros_like(acc_ref)

    acc_ref[...] += jnp.dot(vt_ref[0], gt_ref[...], preferred_element_type=F32)

    grp = jnp.clip(s - 1, 0, n_steps - 1) % n_e
    packed = (PACKED_ROWS, LANES)
    for il in range(SUBLANES):
        for j in range(n_lane_tiles):
            lanes = pl.ds(j * LANES, LANES)
            acc = [jnp.zeros(packed, BF16)] * n_blk
            for hd in range(PEER_HEADS):
                n1b = jnp.broadcast_to(n1_ref[hd, grp, pl.ds(il, 1), lanes], packed).astype(BF16)
                e1b = jnp.broadcast_to(e1_ref[hd, grp, pl.ds(il, 1), lanes], packed).astype(BF16)
                for k in range(n_blk):
                    rows = pl.ds(k * SUBLANES, SUBLANES)
                    rank2 = pltpu.bitcast(r2_ref[hd, rows, lanes], BF16)
                    e2 = pltpu.bitcast(e2_ref[hd, rows, lanes], BF16)
                    acc[k] = acc[k] + jnp.where(rank2 < n1b, e2, 0.0) * e1b
            for k in range(n_blk):
                rows = pl.ds(il * N_KEYS + k * PACKED_ROWS, PACKED_ROWS)
                a = a_ref[rows, lanes]
                gelu = 0.5 * a * (1.0 + lax.erf(a * SQRT_HALF))
                gt_ref[rows, lanes] = acc[k] * gelu.astype(BF16)

    a_ref[...] = jnp.dot(u_ref[...], ht_ref[...], preferred_element_type=F32)

    @pl.when(jnp.logical_and(s >= 2, (s - 2) % n_e == n_e - 1))
    def _():
        o_ref[...] = x_ref[...] + g2_ref[0] * acc_ref[...].T


def _peer_dense_call(ht, u, vt, n1, e1, r2, e2, x, g2, t_per_batch, l):
    ttot, d = x.shape
    tt = min(512, t_per_batch)
    n_e = N_EXPERTS // PEER_TE
    n_steps = (ttot // tt) * n_e
    per_batch = t_per_batch // tt
    tile = lambda s, lag: jnp.clip(s - lag, 0, n_steps - 1) // n_e
    group = lambda s, lag: jnp.clip(s - lag, 0, n_steps - 1) % n_e
    score_spec = pl.BlockSpec((PEER_HEADS, N_KEYS // 2, tt), lambda s: (0, 0, tile(s, 1)))
    grouped_spec = pl.BlockSpec((PEER_HEADS, N_KEYS // SUBLANES, SUBLANES, tt),
                                lambda s: (0, 0, 0, tile(s, 1)))
    return pl.pallas_call(
        functools.partial(_peer_dense_kernel, n_e=n_e, n_steps=n_steps),
        grid=(n_steps + 2,),
        in_specs=[
            pl.BlockSpec((d, tt), lambda s: (0, tile(s, 0))),
            pl.BlockSpec((None, PEER_TE, d), lambda s: (l, group(s, 0), 0)),
            pl.BlockSpec((None, 1, d, PEER_TE), lambda s: (l, group(s, 2), 0, 0)),
            grouped_spec, grouped_spec, score_spec, score_spec,
            pl.BlockSpec((tt, d), lambda s: (tile(s, 2), 0)),
            pl.BlockSpec((1, 1, d), lambda s: (tile(s, 2) // per_batch, 0, 0)),
        ],
        out_specs=pl.BlockSpec((tt, d), lambda s: (tile(s, 2), 0)),
        out_shape=jax.ShapeDtypeStruct((ttot, d), F32),
        scratch_shapes=[pltpu.VMEM((d, tt), F32),
                        pltpu.VMEM((PEER_TE, tt), F32),
                        pltpu.VMEM((PEER_TE, tt), BF16)],
        compiler_params=_params("arbitrary"),
        name="peer_dense",
    )(ht, u, vt, n1, e1, r2, e2, x, g2)


def _final_norm_kernel(x_ref, g_ref, o_ref):
    x = x_ref[...]
    ms = jnp.mean(x * x, axis=-1, keepdims=True)
    o_ref[...] = x * lax.rsqrt(ms + EPS) * g_ref[...]


def _final_norm_call(x, g):
    n, d = x.shape
    tm = 1024
    return pl.pallas_call(
        _final_norm_kernel,
        grid=(n // tm,),
        in_specs=[pl.BlockSpec((tm, d), lambda i: (i, 0)), pl.BlockSpec((1, d), lambda i: (0, 0))],
        out_specs=pl.BlockSpec((tm, d), lambda i: (i, 0)),
        out_shape=jax.ShapeDtypeStruct((n, d), F32),
        compiler_params=_params("parallel"),
        name="final_norm",
    )(x, g)


def _mixer(x, mod, l, norm1_g, w_in, conv_w, w_out, lbf, lbb, s0f, s0b, grid_conv, full):
    sh1, sc1, g1 = mod[0], mod[1], mod[2]
    proj = _proj_call(x, norm1_g[l][None, :], sc1, sh1, w_in, l)
    o_f, o_b, s_f, s_b = _gla_call(proj, lbf, lbb, s0f, s0b)
    if not full:
        return None, s_f, s_b
    x = _mixout_call(o_f, o_b, proj, conv_w[l], w_out[l], x, g1, grid_conv)
    return x, s_f, s_b


def _peer(x, mod, l, norm2_g, wq, keys, u, vt):
    bsz, t, d = x.shape
    sh2, sc2, g2 = mod[3], mod[4], mod[5]
    ht, n1, e1, r2, e2 = _peer_prep_call(x, norm2_g[l][None, :], sc2, sh2, wq, keys, l)
    out = _peer_dense_call(ht, u, vt, n1, e1, r2, e2, x.reshape(bsz * t, d), g2, t, l)
    return out.reshape(bsz, t, d)


def kernel(x, c, ctx, c_ctx, w_mod, b_mod, norm1_g, norm2_g, w_in, conv_w, w_out, lb_logits,
           peer_wq, peer_subkeys, peer_u, peer_v, final_g):
    bsz, t, d = x.shape
    depth = w_mod.shape[0]

    p_lb = jax.nn.softmax(lb_logits.astype(F32), axis=0)
    lower = jnp.cumsum(p_lb, axis=0) - p_lb[0]

    cond = jnp.zeros((COND_ROWS, d), F32).at[:bsz].set(c).at[bsz].set(c_ctx)
    mod = _mod_call(cond, w_mod, b_mod)
    mod = mod.reshape(depth, COND_ROWS, N_MOD, d)
    mod_x = jnp.transpose(mod[:, :bsz], (0, 2, 1, 3))[:, :, :, None, :]
    mod_c = jnp.broadcast_to(mod[:, bsz][:, :, None, None, :], mod_x.shape)

    w_in_b = w_in.astype(BF16)
    w_out_b = w_out.astype(BF16)
    wq_b = peer_wq.astype(BF16)
    keys_b = peer_subkeys.astype(BF16)
    u_b = peer_u.astype(BF16)
    vt_b = jnp.swapaxes(peer_v.astype(BF16).reshape(depth, N_EXPERTS // PEER_TE, PEER_TE, d), 2, 3)

    zero_state = jnp.zeros((bsz, HG_HEADS, HG_DIM, HG_DIM), F32)
    xc = ctx
    for l in range(depth):
        lbf = lower[l, 0][None, :]
        lbb = lower[l, 1][None, :]
        full = l < depth - 1
        xc_new, s_f, s_b = _mixer(xc, mod_c[l], l, norm1_g, w_in_b, conv_w, w_out_b, lbf, lbb,
                                  zero_state, zero_state, False, full)
        if full:
            xc = _peer(xc_new, mod_c[l], l, norm2_g, wq_b, keys_b, u_b, vt_b)
        x, _, _ = _mixer(x, mod_x[l], l, norm1_g, w_in_b, conv_w, w_out_b, lbf, lbb,
                         s_f, s_b, True, True)
        x = _peer(x, mod_x[l], l, norm2_g, wq_b, keys_b, u_b, vt_b)
    return _final_norm_call(x.reshape(bsz * t, d), final_g[None, :]).reshape(bsz, t, d)
```

```python
import functools
import math

import numpy as np
import jax
import jax.numpy as jnp
from jax import lax
from jax.experimental import pallas as pl
from jax.experimental.pallas import tpu as pltpu

F32 = jnp.float32
BF16 = jnp.bfloat16

D_MODEL = 1024
GRID_W = 64
EPS = 1e-6
F_FLOOR = 1e-20
N_MOD = 6
HG_WIDTH = 512
HG_HEADS = 4
HG_DIM = HG_WIDTH // HG_HEADS
SC_WIDTH = 512
SC_HALF = SC_WIDTH // 2
IN_COLS = 5 * HG_WIDTH + 3 * SC_WIDTH
PEER_HEADS = 8
PEER_QDIM = 256
PEER_HALF = PEER_QDIM // 2
N_KEYS = 128
N_EXPERTS = N_KEYS * N_KEYS
PEER_TOPK = 16

SUBLANES = 8
LANES = 128
PACKED_ROWS = 2 * SUBLANES
VMEM_LIMIT = 56 * 1024 * 1024

GLA_CHUNK = 128
GLA_LEVELS = (64, 32, 16, 8, 4, 2, 1)
GLA_GROUP_HEADS = 2
COND_ROWS = 8

COL_IV, COL_ZF, COL_ZB, COL_Q, COL_G, COL_CG, COL_BG, COL_HV = range(8)


def _params(*sem):
    return pltpu.CompilerParams(dimension_semantics=sem, vmem_limit_bytes=VMEM_LIMIT)


def _dot(a, b):
    return jnp.dot(a.astype(BF16), b.astype(BF16), preferred_element_type=F32)


def _dot_nt(a, b):
    return lax.dot_general(a.astype(BF16), b.astype(BF16), (((1,), (1,)), ((), ())),
                           preferred_element_type=F32)


def _dot_tn(a, b):
    return lax.dot_general(a.astype(BF16), b.astype(BF16), (((0,), (0,)), ((), ())),
                           preferred_element_type=F32)


def _mod_kernel(cond_ref, w_ref, b_ref, o_ref):
    c = cond_ref[...]
    s = c * jax.nn.sigmoid(c)
    o_ref[0] = _dot(s, w_ref[0]) + b_ref[0]


def _mod_call(cond, w_mod, b_mod):
    depth, d, n = w_mod.shape
    tn = 1536
    return pl.pallas_call(
        _mod_kernel,
        grid=(depth, n // tn),
        in_specs=[
            pl.BlockSpec((COND_ROWS, d), lambda l, j: (0, 0)),
            pl.BlockSpec((1, d, tn), lambda l, j: (l, 0, j)),
            pl.BlockSpec((1, 1, tn), lambda l, j: (l, 0, j)),
        ],
        out_specs=pl.BlockSpec((1, COND_ROWS, tn), lambda l, j: (l, 0, j)),
        out_shape=jax.ShapeDtypeStruct((depth, COND_ROWS, n), F32),
        compiler_params=_params("parallel", "parallel"),
        name="mod",
    )(cond, w_mod, b_mod.reshape(depth, 1, n))


def _norm_mod(x, g, scale, shift):
    ms = jnp.mean(x * x, axis=-1, keepdims=True)
    y = x * lax.rsqrt(ms + EPS) * g
    return y * (1.0 + scale) + shift


def _proj_kernel(x_ref, g_ref, sc_ref, sh_ref, w_ref, o_ref):
    h = _norm_mod(x_ref[0], g_ref[...], sc_ref[0], sh_ref[0])
    o_ref[0] = _dot(h, w_ref[...])


def _proj_call(x, g, scale, shift, w, l):
    bsz, t, d = x.shape
    n = w.shape[2]
    tm = min(256, t)
    return pl.pallas_call(
        _proj_kernel,
        grid=(bsz, t // tm),
        in_specs=[
            pl.BlockSpec((1, tm, d), lambda b, i: (b, i, 0)),
            pl.BlockSpec((1, d), lambda b, i: (0, 0)),
            pl.BlockSpec((1, 1, d), lambda b, i: (b, 0, 0)),
            pl.BlockSpec((1, 1, d), lambda b, i: (b, 0, 0)),
            pl.BlockSpec((None, d, n), lambda b, i: (l, 0, 0)),
        ],
        out_specs=pl.BlockSpec((1, tm, n), lambda b, i: (b, i, 0)),
        out_shape=jax.ShapeDtypeStruct((bsz, t, n), F32),
        compiler_params=_params("parallel", "parallel"),
        name="proj",
    )(x, g, scale, shift, w)


def _gla_sum_matrix(backward):
    c = GLA_CHUNK
    t = np.arange(c)[:, None]
    s = np.arange(c)[None, :]
    mats = [s >= t if backward else s <= t]
    for n in GLA_LEVELS:
        if n >= SUBLANES:
            continue
        mid = (t // (2 * n)) * (2 * n) + n
        if not backward:
            m = np.where(t >= mid, (s >= mid) & (s <= t), (s > t) & (s < mid))
        else:
            m = np.where(t < mid, (s >= t) & (s < mid), (s >= mid) & (s < t))
        mats.append(m)
    return np.concatenate([m.astype(np.float32) for m in mats], axis=0)


def _gla_masks(backward):
    c = GLA_CHUNK
    row = lax.broadcasted_iota(jnp.int32, (c, c), 0)
    col = lax.broadcasted_iota(jnp.int32, (c, c), 1)
    qrow, lvl = [], []
    for n in GLA_LEVELS:
        r_hi = (row & (2 * n - 1)) >= n
        c_hi = (col & (2 * n - 1)) >= n
        same = (row & ~(2 * n - 1)) == (col & ~(2 * n - 1))
        if not backward:
            qrow.append(r_hi)
            lvl.append(same & r_hi & jnp.logical_not(c_hi))
        else:
            qrow.append(jnp.logical_not(r_hi))
            lvl.append(same & jnp.logical_not(r_hi) & c_hi)
    return qrow, lvl, row == col


def _gla_gates(z, lb, msum):
    sig = jax.nn.sigmoid(z)
    f = lb + (1.0 - lb) * sig
    logf = jnp.log(jnp.maximum(f, F_FLOOR))
    kk = (1.0 - lb) * jax.nn.sigmoid(-z)
    sums = jnp.dot(msum, logf.astype(BF16), preferred_element_type=F32)
    return kk, sums


def _boundary_rows(b, n, backward):
    c, w = b.shape
    parts = []
    for start in range(0, c, 2 * n):
        r = start + n if backward else start + n - 1
        parts.append(jnp.broadcast_to(b[r:r + 1, :], (2 * n, w)))
    return jnp.concatenate(parts, axis=0)


def _gla_chains(chains, ones):
    c = GLA_CHUNK
    atts = [jnp.where(masks[2], jnp.dot((q * kk).astype(BF16), ones, preferred_element_type=F32), 0.0)
            for q, v, kk, sums, st, masks, backward in chains]
    sub_tile = [n for n in GLA_LEVELS if n < SUBLANES]
    for li, n in enumerate(GLA_LEVELS):
        for ci, (q, v, kk, sums, st, masks, backward) in enumerate(chains):
            if n >= SUBLANES:
                exponent = -jnp.abs(sums[0:c] - _boundary_rows(sums[0:c], n, backward))
            else:
                k = 1 + sub_tile.index(n)
                exponent = sums[k * c:(k + 1) * c]
            mixed = (jnp.where(masks[0][li][:, :HG_DIM], q, kk) * jnp.exp(exponent)).astype(BF16)
            a = lax.dot_general(mixed, mixed, (((1,), (1,)), ((), ())), preferred_element_type=F32)
            atts[ci] = jnp.where(masks[1][li], a, atts[ci])
    out = []
    for att, (q, v, kk, sums, st, masks, backward) in zip(atts, chains):
        b = sums[0:c]
        b_end = b[0:1, :] if backward else b[c - 1:c, :]
        q_in = q * jnp.exp(b)
        o = _dot(att, v) + _dot_nt(q_in, st)
        k_out = kk * jnp.exp(b_end - b)
        out.append((o, jnp.exp(b_end) * st + _dot_tn(v, k_out)))
    return out


def _gla_kernel(ivf_ref, zf_ref, qf_ref, ivb_ref, zb_ref, qb_ref, lbf_ref, lbb_ref,
                s0f_ref, s0b_ref, mf_ref, mb_ref,
                of_ref, ob_ref, sf_ref, sb_ref, st_ref):
    n = pl.program_id(1)

    @pl.when(n == 0)
    def _():
        st_ref[0] = s0f_ref[0]
        st_ref[1] = s0b_ref[0]

    ones = jnp.ones((HG_DIM, HG_DIM), BF16)
    masks_f = _gla_masks(False)
    masks_b = _gla_masks(True)
    kk_f, sums_f = _gla_gates(zf_ref[0], lbf_ref[...], mf_ref[...])
    kk_b, sums_b = _gla_gates(zb_ref[0], lbb_ref[...], mb_ref[...])
    for h0 in range(0, HG_HEADS, GLA_GROUP_HEADS):
        heads = range(h0, h0 + GLA_GROUP_HEADS)
        sls = [slice(h * HG_DIM, (h + 1) * HG_DIM) for h in heads]
        chains = [(qf_ref[0, :, sl], ivf_ref[0, :, sl], kk_f[:, sl], sums_f[:, sl],
                   st_ref[0, h], masks_f, False) for h, sl in zip(heads, sls)]
        chains += [(qb_ref[0, :, sl], ivb_ref[0, :, sl], kk_b[:, sl], sums_b[:, sl],
                    st_ref[1, h], masks_b, True) for h, sl in zip(heads, sls)]
        res = _gla_chains(chains, ones)
        for i, (h, sl) in enumerate(zip(heads, sls)):
            of_ref[0, :, sl], st_ref[0, h] = res[i]
            ob_ref[0, :, sl], st_ref[1, h] = res[GLA_GROUP_HEADS + i]

    @pl.when(n == pl.num_programs(1) - 1)
    def _():
        sf_ref[0] = st_ref[0]
        sb_ref[0] = st_ref[1]


def _gla_call(proj, lbf, lbb, s0f, s0b):
    bsz, t, _ = proj.shape
    c = GLA_CHUNK
    nc = t // c
    w = HG_WIDTH
    msum_f = jnp.asarray(_gla_sum_matrix(False), BF16)
    msum_b = jnp.asarray(_gla_sum_matrix(True), BF16)

    def fwd(col):
        return pl.BlockSpec((1, c, w), lambda b, n: (b, n, col))

    def bwd(col):
        return pl.BlockSpec((1, c, w), lambda b, n: (b, nc - 1 - n, col))

    state_spec = pl.BlockSpec((1, HG_HEADS, HG_DIM, HG_DIM), lambda b, n: (b, 0, 0, 0))
    const = lambda shape: pl.BlockSpec(shape, lambda b, n: (0,) * len(shape))
    return pl.pallas_call(
        _gla_kernel,
        grid=(bsz, nc),
        in_specs=[fwd(COL_IV), fwd(COL_ZF), fwd(COL_Q), bwd(COL_IV), bwd(COL_ZB), bwd(COL_Q),
                  const((1, w)), const((1, w)), state_spec, state_spec,
                  const(msum_f.shape), const(msum_b.shape)],
        out_specs=[pl.BlockSpec((1, c, w), lambda b, n: (b, n, 0)),
                   pl.BlockSpec((1, c, w), lambda b, n: (b, nc - 1 - n, 0)),
                   state_spec, state_spec],
        out_shape=[jax.ShapeDtypeStruct((bsz, t, w), F32),
                   jax.ShapeDtypeStruct((bsz, t, w), F32),
                   jax.ShapeDtypeStruct((bsz, HG_HEADS, HG_DIM, HG_DIM), F32),
                   jax.ShapeDtypeStruct((bsz, HG_HEADS, HG_DIM, HG_DIM), F32)],
        scratch_shapes=[pltpu.VMEM((2, HG_HEADS, HG_DIM, HG_DIM), F32)],
        compiler_params=_params("parallel", "arbitrary"),
        name="gla",
    )(proj, proj, proj, proj, proj, proj, lbf, lbb, s0f, s0b, msum_f, msum_b)


def _mixout_body(of_ref, ob_ref, g_ref, cg_ref, bg_ref, hv_ref, halo, cw_ref, wo_ref,
                 x_ref, g1_ref, o_ref, grid_conv):
    tm = of_ref.shape[1]
    o = of_ref[0] + ob_ref[0]
    heads = []
    for h in range(HG_HEADS):
        oh = o[:, h * HG_DIM:(h + 1) * HG_DIM]
        heads.append(oh * lax.rsqrt(jnp.mean(oh * oh, axis=-1, keepdims=True) + EPS))
    g = g_ref[0]
    o_rec = jnp.concatenate(heads, axis=-1) * (g * jax.nn.sigmoid(g))

    u = cg_ref[0] * hv_ref[0]
    w0, w1, w2 = cw_ref[0:1, :], cw_ref[1:2, :], cw_ref[2:3, :]
    pos = lax.broadcasted_iota(jnp.int32, (tm, 1), 0)
    if grid_conv:
        cgp_ref, hvp_ref, cgn_ref, hvn_ref = halo
        i = pl.program_id(1)
        last = pl.num_programs(1) - 1
        colpos = pos & (GRID_W - 1)
        uh = u[:, :SC_HALF]
        left = jnp.where(colpos == 0, 0.0, pltpu.roll(uh, 1, 0))
        right = jnp.where(colpos == GRID_W - 1, 0.0, pltpu.roll(uh, tm - 1, 0))
        conv_h = w0[:, :SC_HALF] * left + w1[:, :SC_HALF] * uh + w2[:, :SC_HALF] * right
        uv = u[:, SC_HALF:]
        up_halo = jnp.where(i == 0, 0.0, cgp_ref[0][:, SC_HALF:] * hvp_ref[0][:, SC_HALF:])
        dn_halo = jnp.where(i == last, 0.0, cgn_ref[0][:, SC_HALF:] * hvn_ref[0][:, SC_HALF:])
        up = jnp.concatenate([up_halo, uv[:tm - GRID_W]], axis=0)
        down = jnp.concatenate([uv[GRID_W:], dn_halo], axis=0)
        conv_v = w0[:, SC_HALF:] * up + w1[:, SC_HALF:] * uv + w2[:, SC_HALF:] * down
        conv = jnp.concatenate([conv_h, conv_v], axis=-1)
    else:
        left = jnp.where(pos == 0, 0.0, pltpu.roll(u, 1, 0))
        right = jnp.where(pos == tm - 1, 0.0, pltpu.roll(u, tm - 1, 0))
        conv = w0 * left + w1 * u + w2 * right
    o_conv = bg_ref[0] * conv
    y = _dot(o_rec, wo_ref[0:HG_WIDTH, :]) + _dot(o_conv, wo_ref[HG_WIDTH:, :])
    o_ref[0] = x_ref[0] + g1_ref[0] * y


def _mixout_grid_kernel(of_ref, ob_ref, g_ref, cg_ref, bg_ref, hv_ref, cgp_ref, hvp_ref,
                        cgn_ref, hvn_ref, cw_ref, wo_ref, x_ref, g1_ref, o_ref):
    _mixout_body(of_ref, ob_ref, g_ref, cg_ref, bg_ref, hv_ref,
                 (cgp_ref, hvp_ref, cgn_ref, hvn_ref), cw_ref, wo_ref, x_ref, g1_ref, o_ref, True)


def _mixout_seq_kernel(of_ref, ob_ref, g_ref, cg_ref, bg_ref, hv_ref, cw_ref, wo_ref,
                       x_ref, g1_ref, o_ref):
    _mixout_body(of_ref, ob_ref, g_ref, cg_ref, bg_ref, hv_ref, None, cw_ref, wo_ref,
                 x_ref, g1_ref, o_ref, False)


def _mixout_call(o_f, o_b, proj, conv_w, w_out, x, g1, grid_conv):
    bsz, t, d = x.shape
    w = HG_WIDTH
    tm = 512 if grid_conv else t
    nt = t // tm
    hb = tm // GRID_W
    nhalo = t // GRID_W

    def col(cidx):
        return pl.BlockSpec((1, tm, w), lambda b, i: (b, i, cidx))

    def prev(cidx):
        return pl.BlockSpec((1, GRID_W, w), lambda b, i: (b, jnp.maximum(i * hb - 1, 0), cidx))

    def nxt(cidx):
        return pl.BlockSpec((1, GRID_W, w),
                            lambda b, i: (b, jnp.minimum((i + 1) * hb, nhalo - 1), cidx))

    row = pl.BlockSpec((1, tm, w), lambda b, i: (b, i, 0))
    in_specs = [row, row, col(COL_G), col(COL_CG), col(COL_BG), col(COL_HV)]
    args = [o_f, o_b, proj, proj, proj, proj]
    if grid_conv:
        in_specs += [prev(COL_CG), prev(COL_HV), nxt(COL_CG), nxt(COL_HV)]
        args += [proj, proj, proj, proj]
    in_specs += [
        pl.BlockSpec((3, w), lambda b, i: (0, 0)),
        pl.BlockSpec(w_out.shape, lambda b, i: (0, 0)),
        pl.BlockSpec((1, tm, d), lambda b, i: (b, i, 0)),
        pl.BlockSpec((1, 1, d), lambda b, i: (b, 0, 0)),
    ]
    args += [conv_w, w_out, x, g1]
    return pl.pallas_call(
        _mixout_grid_kernel if grid_conv else _mixout_seq_kernel,
        grid=(bsz, nt),
        in_specs=in_specs,
        out_specs=pl.BlockSpec((1, tm, d), lambda b, i: (b, i, 0)),
        out_shape=jax.ShapeDtypeStruct((bsz, t, d), F32),
        compiler_params=_params("parallel", "parallel"),
        name="mixout_grid" if grid_conv else "mixout_seq",
    )(*args)


def _oddeven_merge_sort_pairs(n):
    pairs = []
    p = 1
    while p < n:
        k = p
        while k >= 1:
            for j in range(k % p, n - k, 2 * k):
                for i in range(min(k, n - j - k)):
                    if (i + j) // (2 * p) == (i + j + k) // (2 * p):
                        pairs.append((i + j, i + j + k))
            k //= 2
        p *= 2
    return pairs


_SORT16 = _oddeven_merge_sort_pairs(PEER_TOPK)


def _sort_desc(xs):
    xs = list(xs)
    for i, j in _SORT16:
        a, b = xs[i], xs[j]
        xs[i], xs[j] = jnp.maximum(a, b), jnp.minimum(a, b)
    return xs


def _bitonic_merge_desc(xs):
    xs = list(xs)
    d = len(xs) // 2
    while d >= 1:
        for i in range(len(xs)):
            if i & d == 0:
                a, b = xs[i], xs[i + d]
                xs[i], xs[i + d] = jnp.maximum(a, b), jnp.minimum(a, b)
        d //= 2
    return xs


def _top_of_union(a, b):
    k = len(a)
    return [jnp.maximum(a[r], b[k - 1 - r]) for r in range(k)]


def _topk_rows(s):
    rows = [s[SUBLANES * v:SUBLANES * (v + 1), :] for v in range(N_KEYS // SUBLANES)]
    rows = _sort_desc(rows)
    for shift in (4, 2, 1):
        rolled = [pltpu.roll(r, shift, 0) for r in rows]
        rows = _bitonic_merge_desc(_top_of_union(rows, rolled))
    return rows


def _peer_prep_kernel(x_ref, g_ref, sc_ref, sh_ref, wq_ref, keys_ref,
                      ht_ref, n1_ref, e1_ref, r2_ref, e2_ref, s1_ref, s2_ref):
    tq = x_ref.shape[1]
    hf = _norm_mod(x_ref[0], g_ref[...], sc_ref[0], sh_ref[0])
    ht_ref[...] = hf.T.astype(BF16)
    qf = jnp.dot(hf.astype(BF16), wq_ref[...], preferred_element_type=F32)
    sub = lax.broadcasted_iota(jnp.int32, (SUBLANES, tq), 0)
    neg = jnp.full((SUBLANES, tq), -jnp.inf, F32)
    packed = [[jnp.zeros((SUBLANES, tq), F32)] * PEER_TOPK for _ in range(2)]
    for hd in range(PEER_HEADS):
        for p in range(2):
            lo = hd * PEER_QDIM + p * PEER_HALF
            s = _dot_nt(keys_ref[hd, p], qf[:, lo:lo + PEER_HALF])
            (s1_ref if p == 0 else s2_ref)[hd] = s
            top = _topk_rows(s)
            packed[p] = [jnp.where(sub == hd, top[r], packed[p][r]) for r in range(PEER_TOPK)]
    c1, c2 = packed
    pairs = [(a, b) for a in range(PEER_TOPK) for b in range(PEER_TOPK)
             if (a + 1) * (b + 1) <= PEER_TOPK]
    cand = {ab: c1[ab[0]] + c2[ab[1]] for ab in pairs}
    cands = [cand[ab] for ab in pairs]
    cands += [neg] * (-len(cands) % PEER_TOPK)
    groups = [_sort_desc(cands[i:i + PEER_TOPK]) for i in range(0, len(cands), PEER_TOPK)]
    while len(groups) > 2:
        nxt = [_bitonic_merge_desc(_top_of_union(groups[i], groups[i + 1]))
               for i in range(0, len(groups) - 1, 2)]
        if len(groups) % 2:
            nxt.append(groups[-1])
        groups = nxt
    top = _top_of_union(groups[0], groups[1]) if len(groups) == 2 else groups[0]
    tau = functools.reduce(jnp.minimum, top)
    m1, m2 = c1[0], c2[0]
    mx = m1 + m2
    z = functools.reduce(lambda a, b: a + b, [jnp.exp(t - mx) for t in top])
    inv_z = 1.0 / z
    counts = []
    for a in range(PEER_TOPK):
        n = jnp.zeros((SUBLANES, tq), F32)
        for b in range(PEER_TOPK):
            if (a, b) in cand:
                n = n + jnp.where(cand[(a, b)] >= tau, 1.0, 0.0)
        counts.append(n)
    grouped = (N_KEYS // SUBLANES, SUBLANES, tq)
    for hd in range(PEER_HEADS):
        s1 = s1_ref[hd]
        s2 = s2_ref[hd]
        n1 = jnp.zeros((N_KEYS, tq), F32)
        rank2 = jnp.full((N_KEYS, tq), float(PEER_TOPK), F32)
        for r in reversed(range(PEER_TOPK)):
            n1 = jnp.where(s1 == c1[r][hd:hd + 1, :], counts[r][hd:hd + 1, :], n1)
            rank2 = jnp.where(s2 == c2[r][hd:hd + 1, :], float(r), rank2)
        n1_ref[hd] = n1.reshape(grouped)
        r2_ref[hd] = pltpu.bitcast(rank2.astype(BF16), jnp.uint32)
        e1 = jnp.exp(s1 - m1[hd:hd + 1, :]) * inv_z[hd:hd + 1, :]
        e1_ref[hd] = e1.reshape(grouped)
        e2_ref[hd] = pltpu.bitcast(jnp.exp(s2 - m2[hd:hd + 1, :]).astype(BF16), jnp.uint32)


def _peer_prep_call(x, g, scale, shift, wq, keys, l):
    bsz, t, d = x.shape
    tq = min(256, t)
    nt = t // tq
    ttot = bsz * t
    score_spec = pl.BlockSpec((PEER_HEADS, N_KEYS // 2, tq), lambda b, i: (0, 0, b * nt + i))
    score_shape = jax.ShapeDtypeStruct((PEER_HEADS, N_KEYS // 2, ttot), jnp.uint32)
    n_grp = N_KEYS // SUBLANES
    grouped_spec = pl.BlockSpec((PEER_HEADS, n_grp, SUBLANES, tq), lambda b, i: (0, 0, 0, b * nt + i))
    grouped_shape = jax.ShapeDtypeStruct((PEER_HEADS, n_grp, SUBLANES, ttot), F32)
    return pl.pallas_call(
        _peer_prep_kernel,
        grid=(bsz, nt),
        in_specs=[
            pl.BlockSpec((1, tq, d), lambda b, i: (b, i, 0)),
            pl.BlockSpec((1, d), lambda b, i: (0, 0)),
            pl.BlockSpec((1, 1, d), lambda b, i: (b, 0, 0)),
            pl.BlockSpec((1, 1, d), lambda b, i: (b, 0, 0)),
            pl.BlockSpec((None,) + wq.shape[1:], lambda b, i: (l, 0, 0)),
            pl.BlockSpec((None,) + keys.shape[1:], lambda b, i: (l, 0, 0, 0, 0)),
        ],
        out_specs=[pl.BlockSpec((d, tq), lambda b, i: (0, b * nt + i)),
                   grouped_spec, grouped_spec, score_spec, score_spec],
        out_shape=[jax.ShapeDtypeStruct((d, ttot), BF16),
                   grouped_shape, grouped_shape, score_shape, score_shape],
        scratch_shapes=[pltpu.VMEM((PEER_HEADS, N_KEYS, tq), F32),
                        pltpu.VMEM((PEER_HEADS, N_KEYS, tq), F32)],
        compiler_params=_params("parallel", "parallel"),
        name="peer_prep",
    )(x, g, scale, shift, wq, keys)


PEER_GROUPS = 2
PEER_TE = PEER_GROUPS * SUBLANES * N_KEYS
SQRT_HALF = math.sqrt(0.5)


def _peer_dense_kernel(ht_ref, u_ref, vt_ref, n1_ref, e1_ref, r2_ref, e2_ref,
                       x_ref, g2_ref, o_ref, acc_ref, a_ref, gt_ref, *, n_e, n_steps):
    s = pl.program_id(0)
    tt = ht_ref.shape[1]
    n_lane_tiles = tt // LANES
    n_blk = N_KEYS // PACKED_ROWS

    @pl.when(s == 0)
    def _():
        a_ref[...] = jnp.zeros_like(a_ref)
        gt_ref[...] = jnp.zeros_like(gt_ref)

    @pl.when(jnp.logical_or(s == 0, (s - 2) % n_e == 0))
    def _():
        acc_ref[...] = jnp.zeros_like(acc_ref)

    acc_ref[...] += jnp.dot(vt_ref[0], gt_ref[...], preferred_element_type=F32)

    grp0 = (jnp.clip(s - 1, 0, n_steps - 1) % n_e) * PEER_GROUPS
    packed = (PACKED_ROWS, LANES)
    for il in range(PEER_GROUPS * SUBLANES):
        grp = grp0 + il // SUBLANES
        sub = pl.ds(il % SUBLANES, 1)
        for j in range(n_lane_tiles):
            lanes = pl.ds(j * LANES, LANES)
            acc = [jnp.zeros(packed, BF16)] * n_blk
            for hd in range(PEER_HEADS):
                n1b = jnp.broadcast_to(n1_ref[hd, grp, sub, lanes], packed).astype(BF16)
                e1b = jnp.broadcast_to(e1_ref[hd, grp, sub, lanes], packed).astype(BF16)
                for k in range(n_blk):
                    rows = pl.ds(k * SUBLANES, SUBLANES)
                    rank2 = pltpu.bitcast(r2_ref[hd, rows, lanes], BF16)
                    e2 = pltpu.bitcast(e2_ref[hd, rows, lanes], BF16)
                    acc[k] = acc[k] + jnp.where(rank2 < n1b, e2, 0.0) * e1b
            for k in range(n_blk):
                rows = pl.ds(il * N_KEYS + k * PACKED_ROWS, PACKED_ROWS)
                a = a_ref[rows, lanes]
                gelu = 0.5 * a * (1.0 + lax.erf(a * SQRT_HALF))
                gt_ref[rows, lanes] = acc[k] * gelu.astype(BF16)

    a_ref[...] = jnp.dot(u_ref[...], ht_ref[...], preferred_element_type=F32)

    @pl.when(jnp.logical_and(s >= 2, (s - 2) % n_e == n_e - 1))
    def _():
        o_ref[...] = x_ref[...] + g2_ref[0] * acc_ref[...].T


def _peer_dense_call(ht, u, vt, n1, e1, r2, e2, x, g2, t_per_batch, l):
    ttot, d = x.shape
    tt = min(512, t_per_batch)
    n_e = N_EXPERTS // PEER_TE
    n_steps = (ttot // tt) * n_e
    per_batch = t_per_batch // tt
    tile = lambda s, lag: jnp.clip(s - lag, 0, n_steps - 1) // n_e
    group = lambda s, lag: jnp.clip(s - lag, 0, n_steps - 1) % n_e
    score_spec = pl.BlockSpec((PEER_HEADS, N_KEYS // 2, tt), lambda s: (0, 0, tile(s, 1)))
    grouped_spec = pl.BlockSpec((PEER_HEADS, N_KEYS // SUBLANES, SUBLANES, tt),
                                lambda s: (0, 0, 0, tile(s, 1)))
    return pl.pallas_call(
        functools.partial(_peer_dense_kernel, n_e=n_e, n_steps=n_steps),
        grid=(n_steps + 2,),
        in_specs=[
            pl.BlockSpec((d, tt), lambda s: (0, tile(s, 0))),
            pl.BlockSpec((None, PEER_TE, d), lambda s: (l, group(s, 0), 0)),
            pl.BlockSpec((None, 1, d, PEER_TE), lambda s: (l, group(s, 2), 0, 0)),
            grouped_spec, grouped_spec, score_spec, score_spec,
            pl.BlockSpec((tt, d), lambda s: (tile(s, 2), 0)),
            pl.BlockSpec((1, 1, d), lambda s: (tile(s, 2) // per_batch, 0, 0)),
        ],
        out_specs=pl.BlockSpec((tt, d), lambda s: (tile(s, 2), 0)),
        out_shape=jax.ShapeDtypeStruct((ttot, d), F32),
        scratch_shapes=[pltpu.VMEM((d, tt), F32),
                        pltpu.VMEM((PEER_TE, tt), F32),
                        pltpu.VMEM((PEER_TE, tt), BF16)],
        compiler_params=_params("arbitrary"),
        name="peer_dense",
    )(ht, u, vt, n1, e1, r2, e2, x, g2)


def _final_norm_kernel(x_ref, g_ref, o_ref):
    x = x_ref[...]
    ms = jnp.mean(x * x, axis=-1, keepdims=True)
    o_ref[...] = x * lax.rsqrt(ms + EPS) * g_ref[...]


def _final_norm_call(x, g):
    n, d = x.shape
    tm = 1024
    return pl.pallas_call(
        _final_norm_kernel,
        grid=(n // tm,),
        in_specs=[pl.BlockSpec((tm, d), lambda i: (i, 0)), pl.BlockSpec((1, d), lambda i: (0, 0))],
        out_specs=pl.BlockSpec((tm, d), lambda i: (i, 0)),
        out_shape=jax.ShapeDtypeStruct((n, d), F32),
        compiler_params=_params("parallel"),
        name="final_norm",
    )(x, g)


def _mixer(x, mod, l, norm1_g, w_in, conv_w, w_out, lbf, lbb, s0f, s0b, grid_conv, full):
    sh1, sc1, g1 = mod[0], mod[1], mod[2]
    proj = _proj_call(x, norm1_g[l][None, :], sc1, sh1, w_in, l)
    o_f, o_b, s_f, s_b = _gla_call(proj, lbf, lbb, s0f, s0b)
    if not full:
        return None, s_f, s_b
    x = _mixout_call(o_f, o_b, proj, conv_w[l], w_out[l], x, g1, grid_conv)
    return x, s_f, s_b


def _peer(x, mod, l, norm2_g, wq, keys, u, vt):
    bsz, t, d = x.shape
    sh2, sc2, g2 = mod[3], mod[4], mod[5]
    ht, n1, e1, r2, e2 = _peer_prep_call(x, norm2_g[l][None, :], sc2, sh2, wq, keys, l)
    out = _peer_dense_call(ht, u, vt, n1, e1, r2, e2, x.reshape(bsz * t, d), g2, t, l)
    return out.reshape(bsz, t, d)


def kernel(x, c, ctx, c_ctx, w_mod, b_mod, norm1_g, norm2_g, w_in, conv_w, w_out, lb_logits,
           peer_wq, peer_subkeys, peer_u, peer_v, final_g):
    bsz, t, d = x.shape
    depth = w_mod.shape[0]

    p_lb = jax.nn.softmax(lb_logits.astype(F32), axis=0)
    lower = jnp.cumsum(p_lb, axis=0) - p_lb[0]

    cond = jnp.zeros((COND_ROWS, d), F32).at[:bsz].set(c).at[bsz].set(c_ctx)
    mod = _mod_call(cond, w_mod, b_mod)
    mod = mod.reshape(depth, COND_ROWS, N_MOD, d)
    mod_x = jnp.transpose(mod[:, :bsz], (0, 2, 1, 3))[:, :, :, None, :]
    mod_c = jnp.broadcast_to(mod[:, bsz][:, :, None, None, :], mod_x.shape)

    w_in_b = w_in.astype(BF16)
    w_out_b = w_out.astype(BF16)
    wq_b = peer_wq.astype(BF16)
    keys_b = peer_subkeys.astype(BF16)
    u_b = peer_u.astype(BF16)
    vt_b = jnp.swapaxes(peer_v.astype(BF16).reshape(depth, N_EXPERTS // PEER_TE, PEER_TE, d), 2, 3)

    zero_state = jnp.zeros((bsz, HG_HEADS, HG_DIM, HG_DIM), F32)
    xc = ctx
    for l in range(depth):
        lbf = lower[l, 0][None, :]
        lbb = lower[l, 1][None, :]
        full = l < depth - 1
        xc_new, s_f, s_b = _mixer(xc, mod_c[l], l, norm1_g, w_in_b, conv_w, w_out_b, lbf, lbb,
                                  zero_state, zero_state, False, full)
        if full:
            xc = _peer(xc_new, mod_c[l], l, norm2_g, wq_b, keys_b, u_b, vt_b)
        x, _, _ = _mixer(x, mod_x[l], l, norm1_g, w_in_b, conv_w, w_out_b, lbf, lbb,
                         s_f, s_b, True, True)
        x = _peer(x, mod_x[l], l, norm2_g, wq_b, keys_b, u_b, vt_b)
    return _final_norm_call(x.reshape(bsz * t, d), final_g[None, :]).reshape(bsz, t, d)
```

```python
import functools
import math

import numpy as np
import jax
import jax.numpy as jnp
from jax import lax
from jax.experimental import pallas as pl
from jax.experimental.pallas import tpu as pltpu

F32 = jnp.float32
BF16 = jnp.bfloat16

D_MODEL = 1024
GRID_W = 64
EPS = 1e-6
F_FLOOR = 1e-20
N_MOD = 6
HG_WIDTH = 512
HG_HEADS = 4
HG_DIM = HG_WIDTH // HG_HEADS
SC_WIDTH = 512
SC_HALF = SC_WIDTH // 2
IN_COLS = 5 * HG_WIDTH + 3 * SC_WIDTH
PEER_HEADS = 8
PEER_QDIM = 256
PEER_HALF = PEER_QDIM // 2
N_KEYS = 128
N_EXPERTS = N_KEYS * N_KEYS
PEER_TOPK = 16

SUBLANES = 8
LANES = 128
PACKED_ROWS = 2 * SUBLANES
VMEM_LIMIT = 48 * 1024 * 1024

GLA_CHUNK = 128
GLA_LEVELS = (64, 32, 16, 8, 4, 2, 1)
GLA_GROUP_HEADS = 4
COND_ROWS = 8

COL_IV, COL_ZF, COL_ZB, COL_Q, COL_G, COL_CG, COL_BG, COL_HV = range(8)
GATE_ZF, GATE_ZB = range(2)
FEAT_IV, FEAT_Q, FEAT_G, FEAT_CG, FEAT_BG, FEAT_HV = range(6)


def _params(*sem):
    return pltpu.CompilerParams(dimension_semantics=sem, vmem_limit_bytes=VMEM_LIMIT)


def _dot(a, b):
    return jnp.dot(a.astype(BF16), b.astype(BF16), preferred_element_type=F32)


def _dot_nt(a, b):
    return lax.dot_general(a.astype(BF16), b.astype(BF16), (((1,), (1,)), ((), ())),
                           preferred_element_type=F32)


def _dot_tn(a, b):
    return lax.dot_general(a.astype(BF16), b.astype(BF16), (((0,), (0,)), ((), ())),
                           preferred_element_type=F32)


def _mod_kernel(cond_ref, w_ref, b_ref, o_ref):
    c = cond_ref[...]
    s = c * jax.nn.sigmoid(c)
    o_ref[0] = _dot(s, w_ref[0]) + b_ref[0]


def _mod_call(cond, w_mod, b_mod):
    depth, d, n = w_mod.shape
    tn = 1536
    return pl.pallas_call(
        _mod_kernel,
        grid=(depth, n // tn),
        in_specs=[
            pl.BlockSpec((COND_ROWS, d), lambda l, j: (0, 0)),
            pl.BlockSpec((1, d, tn), lambda l, j: (l, 0, j)),
            pl.BlockSpec((1, 1, tn), lambda l, j: (l, 0, j)),
        ],
        out_specs=pl.BlockSpec((1, COND_ROWS, tn), lambda l, j: (l, 0, j)),
        out_shape=jax.ShapeDtypeStruct((depth, COND_ROWS, n), F32),
        compiler_params=_params("parallel", "parallel"),
        name="mod",
    )(cond, w_mod, b_mod.reshape(depth, 1, n))


def _norm_mod(x, g, scale, shift):
    ms = jnp.mean(x * x, axis=-1, keepdims=True)
    y = x * lax.rsqrt(ms + EPS) * g
    return y * (1.0 + scale) + shift


def _proj_kernel(x_ref, g_ref, sc_ref, sh_ref, w_ref, gates_ref, feats_ref):
    h = _norm_mod(x_ref[0], g_ref[...], sc_ref[0], sh_ref[0])
    y = _dot(h, w_ref[...])
    w = HG_WIDTH
    gates_ref[0] = y[:, COL_ZF * w:(COL_ZB + 1) * w]
    feats_ref[0, :, 0:w] = y[:, COL_IV * w:(COL_IV + 1) * w].astype(BF16)
    feats_ref[0, :, w:] = y[:, COL_Q * w:].astype(BF16)


def _proj_call(x, g, scale, shift, w, l):
    bsz, t, d = x.shape
    n = w.shape[2]
    n_gate = 2 * HG_WIDTH
    tm = min(256, t)
    return pl.pallas_call(
        _proj_kernel,
        grid=(bsz, t // tm),
        in_specs=[
            pl.BlockSpec((1, tm, d), lambda b, i: (b, i, 0)),
            pl.BlockSpec((1, d), lambda b, i: (0, 0)),
            pl.BlockSpec((1, 1, d), lambda b, i: (b, 0, 0)),
            pl.BlockSpec((1, 1, d), lambda b, i: (b, 0, 0)),
            pl.BlockSpec((None, d, n), lambda b, i: (l, 0, 0)),
        ],
        out_specs=[pl.BlockSpec((1, tm, n_gate), lambda b, i: (b, i, 0)),
                   pl.BlockSpec((1, tm, n - n_gate), lambda b, i: (b, i, 0))],
        out_shape=[jax.ShapeDtypeStruct((bsz, t, n_gate), F32),
                   jax.ShapeDtypeStruct((bsz, t, n - n_gate), BF16)],
        compiler_params=_params("parallel", "parallel"),
        name="proj",
    )(x, g, scale, shift, w)


def _gla_sum_matrix(backward):
    c = GLA_CHUNK
    t = np.arange(c)[:, None]
    s = np.arange(c)[None, :]
    mats = [s >= t if backward else s <= t]
    for n in GLA_LEVELS:
        if n >= SUBLANES:
            continue
        mid = (t // (2 * n)) * (2 * n) + n
        if not backward:
            m = np.where(t >= mid, (s >= mid) & (s <= t), (s > t) & (s < mid))
        else:
            m = np.where(t < mid, (s >= t) & (s < mid), (s >= mid) & (s < t))
        mats.append(m)
    return np.concatenate([m.astype(np.float32) for m in mats], axis=0)


def _gla_masks(backward):
    c = GLA_CHUNK
    row = lax.broadcasted_iota(jnp.int32, (c, c), 0)
    col = lax.broadcasted_iota(jnp.int32, (c, c), 1)
    qrow, lvl = [], []
    for n in GLA_LEVELS:
        r_hi = (row & (2 * n - 1)) >= n
        c_hi = (col & (2 * n - 1)) >= n
        same = (row & ~(2 * n - 1)) == (col & ~(2 * n - 1))
        if not backward:
            qrow.append(r_hi)
            lvl.append(same & r_hi & jnp.logical_not(c_hi))
        else:
            qrow.append(jnp.logical_not(r_hi))
            lvl.append(same & jnp.logical_not(r_hi) & c_hi)
    return qrow, lvl, row == col


def _gla_gates(z, lb, msum):
    sig = jax.nn.sigmoid(z)
    f = lb + (1.0 - lb) * sig
    logf = jnp.log(jnp.maximum(f, F_FLOOR))
    kk = (1.0 - lb) * jax.nn.sigmoid(-z)
    sums = jnp.dot(msum, logf.astype(BF16), preferred_element_type=F32)
    return kk, sums


def _boundary_rows(b, n, backward):
    c, w = b.shape
    parts = []
    for start in range(0, c, 2 * n):
        r = start + n if backward else start + n - 1
        parts.append(jnp.broadcast_to(b[r:r + 1, :], (2 * n, w)))
    return jnp.concatenate(parts, axis=0)


def _gla_chains(chains, ones):
    c = GLA_CHUNK
    atts = [jnp.where(masks[2], jnp.dot((q * kk).astype(BF16), ones, preferred_element_type=F32), 0.0)
            for q, v, kk, sums, st, masks, backward in chains]
    sub_tile = [n for n in GLA_LEVELS if n < SUBLANES]
    for li, n in enumerate(GLA_LEVELS):
        for ci, (q, v, kk, sums, st, masks, backward) in enumerate(chains):
            if n >= SUBLANES:
                exponent = -jnp.abs(sums[0:c] - _boundary_rows(sums[0:c], n, backward))
            else:
                k = 1 + sub_tile.index(n)
                exponent = sums[k * c:(k + 1) * c]
            mixed = (jnp.where(masks[0][li][:, :HG_DIM], q, kk) * jnp.exp(exponent)).astype(BF16)
            a = lax.dot_general(mixed, mixed, (((1,), (1,)), ((), ())), preferred_element_type=F32)
            atts[ci] = jnp.where(masks[1][li], a, atts[ci])
    out = []
    for att, (q, v, kk, sums, st, masks, backward) in zip(atts, chains):
        b = sums[0:c]
        b_end = b[0:1, :] if backward else b[c - 1:c, :]
        q_in = q * jnp.exp(b)
        o = _dot(att, v) + _dot_nt(q_in, st)
        k_out = kk * jnp.exp(b_end - b)
        out.append((o, jnp.exp(b_end) * st + _dot_tn(v, k_out)))
    return out


def _gla_kernel(ivf_ref, zf_ref, qf_ref, ivb_ref, zb_ref, qb_ref, lbf_ref, lbb_ref,
                s0f_ref, s0b_ref, mf_ref, mb_ref,
                of_ref, ob_ref, sf_ref, sb_ref, st_ref):
    n = pl.program_id(1)

    @pl.when(n == 0)
    def _():
        st_ref[0] = s0f_ref[0]
        st_ref[1] = s0b_ref[0]

    ones = jnp.ones((HG_DIM, HG_DIM), BF16)
    masks_f = _gla_masks(False)
    masks_b = _gla_masks(True)
    kk_f, sums_f = _gla_gates(zf_ref[0], lbf_ref[...], mf_ref[...])
    kk_b, sums_b = _gla_gates(zb_ref[0], lbb_ref[...], mb_ref[...])
    for h0 in range(0, HG_HEADS, GLA_GROUP_HEADS):
        heads = range(h0, h0 + GLA_GROUP_HEADS)
        sls = [slice(h * HG_DIM, (h + 1) * HG_DIM) for h in heads]
        chains = [(qf_ref[0, :, sl].astype(F32), ivf_ref[0, :, sl], kk_f[:, sl], sums_f[:, sl],
                   st_ref[0, h], masks_f, False) for h, sl in zip(heads, sls)]
        chains += [(qb_ref[0, :, sl].astype(F32), ivb_ref[0, :, sl], kk_b[:, sl], sums_b[:, sl],
                    st_ref[1, h], masks_b, True) for h, sl in zip(heads, sls)]
        res = _gla_chains(chains, ones)
        for i, (h, sl) in enumerate(zip(heads, sls)):
            of_ref[0, :, sl], st_ref[0, h] = res[i]
            ob_ref[0, :, sl], st_ref[1, h] = res[GLA_GROUP_HEADS + i]

    @pl.when(n == pl.num_programs(1) - 1)
    def _():
        sf_ref[0] = st_ref[0]
        sb_ref[0] = st_ref[1]


def _gla_call(gates, feats, lbf, lbb, s0f, s0b):
    bsz, t, _ = gates.shape
    c = GLA_CHUNK
    nc = t // c
    w = HG_WIDTH
    msum_f = jnp.asarray(_gla_sum_matrix(False), BF16)
    msum_b = jnp.asarray(_gla_sum_matrix(True), BF16)

    def fwd(col):
        return pl.BlockSpec((1, c, w), lambda b, n: (b, n, col))

    def bwd(col):
        return pl.BlockSpec((1, c, w), lambda b, n: (b, nc - 1 - n, col))

    state_spec = pl.BlockSpec((1, HG_HEADS, HG_DIM, HG_DIM), lambda b, n: (b, 0, 0, 0))
    const = lambda shape: pl.BlockSpec(shape, lambda b, n: (0,) * len(shape))
    return pl.pallas_call(
        _gla_kernel,
        grid=(bsz, nc),
        in_specs=[fwd(FEAT_IV), fwd(GATE_ZF), fwd(FEAT_Q), bwd(FEAT_IV), bwd(GATE_ZB), bwd(FEAT_Q),
                  const((1, w)), const((1, w)), state_spec, state_spec,
                  const(msum_f.shape), const(msum_b.shape)],
        out_specs=[pl.BlockSpec((1, c, w), lambda b, n: (b, n, 0)),
                   pl.BlockSpec((1, c, w), lambda b, n: (b, nc - 1 - n, 0)),
                   state_spec, state_spec],
        out_shape=[jax.ShapeDtypeStruct((bsz, t, w), F32),
                   jax.ShapeDtypeStruct((bsz, t, w), F32),
                   jax.ShapeDtypeStruct((bsz, HG_HEADS, HG_DIM, HG_DIM), F32),
                   jax.ShapeDtypeStruct((bsz, HG_HEADS, HG_DIM, HG_DIM), F32)],
        scratch_shapes=[pltpu.VMEM((2, HG_HEADS, HG_DIM, HG_DIM), F32)],
        compiler_params=_params("parallel", "arbitrary"),
        name="gla",
    )(feats, gates, feats, feats, gates, feats, lbf, lbb, s0f, s0b, msum_f, msum_b)


def _mixout_body(of_ref, ob_ref, g_ref, cg_ref, bg_ref, hv_ref, halo, cw_ref, wo_ref,
                 x_ref, g1_ref, o_ref, grid_conv):
    tm = of_ref.shape[1]
    o = of_ref[0] + ob_ref[0]
    heads = []
    for h in range(HG_HEADS):
        oh = o[:, h * HG_DIM:(h + 1) * HG_DIM]
        heads.append(oh * lax.rsqrt(jnp.mean(oh * oh, axis=-1, keepdims=True) + EPS))
    g = g_ref[0].astype(F32)
    o_rec = jnp.concatenate(heads, axis=-1) * (g * jax.nn.sigmoid(g))

    u = cg_ref[0].astype(F32) * hv_ref[0].astype(F32)
    w0, w1, w2 = cw_ref[0:1, :], cw_ref[1:2, :], cw_ref[2:3, :]
    pos = lax.broadcasted_iota(jnp.int32, (tm, 1), 0)
    if grid_conv:
        cgp_ref, hvp_ref, cgn_ref, hvn_ref = halo
        i = pl.program_id(1)
        last = pl.num_programs(1) - 1
        colpos = pos & (GRID_W - 1)
        uh = u[:, :SC_HALF]
        left = jnp.where(colpos == 0, 0.0, pltpu.roll(uh, 1, 0))
        right = jnp.where(colpos == GRID_W - 1, 0.0, pltpu.roll(uh, tm - 1, 0))
        conv_h = w0[:, :SC_HALF] * left + w1[:, :SC_HALF] * uh + w2[:, :SC_HALF] * right
        uv = u[:, SC_HALF:]
        halo_u = lambda cg, hv: cg[0][:, SC_HALF:].astype(F32) * hv[0][:, SC_HALF:].astype(F32)
        up_halo = jnp.where(i == 0, 0.0, halo_u(cgp_ref, hvp_ref))
        dn_halo = jnp.where(i == last, 0.0, halo_u(cgn_ref, hvn_ref))
        up = jnp.concatenate([up_halo, uv[:tm - GRID_W]], axis=0)
        down = jnp.concatenate([uv[GRID_W:], dn_halo], axis=0)
        conv_v = w0[:, SC_HALF:] * up + w1[:, SC_HALF:] * uv + w2[:, SC_HALF:] * down
        conv = jnp.concatenate([conv_h, conv_v], axis=-1)
    else:
        left = jnp.where(pos == 0, 0.0, pltpu.roll(u, 1, 0))
        right = jnp.where(pos == tm - 1, 0.0, pltpu.roll(u, tm - 1, 0))
        conv = w0 * left + w1 * u + w2 * right
    o_conv = bg_ref[0].astype(F32) * conv
    y = _dot(o_rec, wo_ref[0:HG_WIDTH, :]) + _dot(o_conv, wo_ref[HG_WIDTH:, :])
    o_ref[0] = x_ref[0] + g1_ref[0] * y


def _mixout_grid_kernel(of_ref, ob_ref, g_ref, cg_ref, bg_ref, hv_ref, cgp_ref, hvp_ref,
                        cgn_ref, hvn_ref, cw_ref, wo_ref, x_ref, g1_ref, o_ref):
    _mixout_body(of_ref, ob_ref, g_ref, cg_ref, bg_ref, hv_ref,
                 (cgp_ref, hvp_ref, cgn_ref, hvn_ref), cw_ref, wo_ref, x_ref, g1_ref, o_ref, True)


def _mixout_seq_kernel(of_ref, ob_ref, g_ref, cg_ref, bg_ref, hv_ref, cw_ref, wo_ref,
                       x_ref, g1_ref, o_ref):
    _mixout_body(of_ref, ob_ref, g_ref, cg_ref, bg_ref, hv_ref, None, cw_ref, wo_ref,
                 x_ref, g1_ref, o_ref, False)


def _mixout_call(o_f, o_b, proj, conv_w, w_out, x, g1, grid_conv):
    bsz, t, d = x.shape
    w = HG_WIDTH
    tm = 512 if grid_conv else t
    nt = t // tm
    hb = tm // GRID_W
    nhalo = t // GRID_W

    def col(cidx):
        return pl.BlockSpec((1, tm, w), lambda b, i: (b, i, cidx))

    def prev(cidx):
        return pl.BlockSpec((1, GRID_W, w), lambda b, i: (b, jnp.maximum(i * hb - 1, 0), cidx))

    def nxt(cidx):
        return pl.BlockSpec((1, GRID_W, w),
                            lambda b, i: (b, jnp.minimum((i + 1) * hb, nhalo - 1), cidx))

    row = pl.BlockSpec((1, tm, w), lambda b, i: (b, i, 0))
    in_specs = [row, row, col(FEAT_G), col(FEAT_CG), col(FEAT_BG), col(FEAT_HV)]
    args = [o_f, o_b, proj, proj, proj, proj]
    if grid_conv:
        in_specs += [prev(FEAT_CG), prev(FEAT_HV), nxt(FEAT_CG), nxt(FEAT_HV)]
        args += [proj, proj, proj, proj]
    in_specs += [
        pl.BlockSpec((3, w), lambda b, i: (0, 0)),
        pl.BlockSpec(w_out.shape, lambda b, i: (0, 0)),
        pl.BlockSpec((1, tm, d), lambda b, i: (b, i, 0)),
        pl.BlockSpec((1, 1, d), lambda b, i: (b, 0, 0)),
    ]
    args += [conv_w, w_out, x, g1]
    return pl.pallas_call(
        _mixout_grid_kernel if grid_conv else _mixout_seq_kernel,
        grid=(bsz, nt),
        in_specs=in_specs,
        out_specs=pl.BlockSpec((1, tm, d), lambda b, i: (b, i, 0)),
        out_shape=jax.ShapeDtypeStruct((bsz, t, d), F32),
        compiler_params=_params("parallel", "parallel"),
        name="mixout_grid" if grid_conv else "mixout_seq",
    )(*args)


def _oddeven_merge_sort_pairs(n):
    pairs = []
    p = 1
    while p < n:
        k = p
        while k >= 1:
            for j in range(k % p, n - k, 2 * k):
                for i in range(min(k, n - j - k)):
                    if (i + j) // (2 * p) == (i + j + k) // (2 * p):
                        pairs.append((i + j, i + j + k))
            k //= 2
        p *= 2
    return pairs


_SORT16 = _oddeven_merge_sort_pairs(PEER_TOPK)


def _sort_desc(xs):
    xs = list(xs)
    for i, j in _SORT16:
        a, b = xs[i], xs[j]
        xs[i], xs[j] = jnp.maximum(a, b), jnp.minimum(a, b)
    return xs


def _bitonic_merge_desc(xs):
    xs = list(xs)
    d = len(xs) // 2
    while d >= 1:
        for i in range(len(xs)):
            if i & d == 0:
                a, b = xs[i], xs[i + d]
                xs[i], xs[i + d] = jnp.maximum(a, b), jnp.minimum(a, b)
        d //= 2
    return xs


def _top_of_union(a, b):
    k = len(a)
    return [jnp.maximum(a[r], b[k - 1 - r]) for r in range(k)]


def _topk_rows(s):
    rows = [s[SUBLANES * v:SUBLANES * (v + 1), :] for v in range(N_KEYS // SUBLANES)]
    rows = _sort_desc(rows)
    for shift in (4, 2, 1):
        rolled = [pltpu.roll(r, shift, 0) for r in rows]
        rows = _bitonic_merge_desc(_top_of_union(rows, rolled))
    return rows


def _peer_prep_kernel(x_ref, g_ref, sc_ref, sh_ref, wq_ref, keys_ref,
                      ht_ref, n1_ref, e1_ref, r2_ref, e2_ref, s1_ref, s2_ref):
    tq = x_ref.shape[1]
    hf = _norm_mod(x_ref[0], g_ref[...], sc_ref[0], sh_ref[0])
    ht_ref[...] = hf.T.astype(BF16)
    qf = jnp.dot(hf.astype(BF16), wq_ref[...], preferred_element_type=F32)
    sub = lax.broadcasted_iota(jnp.int32, (SUBLANES, tq), 0)
    neg = jnp.full((SUBLANES, tq), -jnp.inf, F32)
    packed = [[jnp.zeros((SUBLANES, tq), F32)] * PEER_TOPK for _ in range(2)]
    for hd in range(PEER_HEADS):
        for p in range(2):
            lo = hd * PEER_QDIM + p * PEER_HALF
            s = _dot_nt(keys_ref[hd, p], qf[:, lo:lo + PEER_HALF])
            (s1_ref if p == 0 else s2_ref)[hd] = s
            top = _topk_rows(s)
            packed[p] = [jnp.where(sub == hd, top[r], packed[p][r]) for r in range(PEER_TOPK)]
    c1, c2 = packed
    pairs = [(a, b) for a in range(PEER_TOPK) for b in range(PEER_TOPK)
             if (a + 1) * (b + 1) <= PEER_TOPK]
    cand = {ab: c1[ab[0]] + c2[ab[1]] for ab in pairs}
    cands = [cand[ab] for ab in pairs]
    cands += [neg] * (-len(cands) % PEER_TOPK)
    groups = [_sort_desc(cands[i:i + PEER_TOPK]) for i in range(0, len(cands), PEER_TOPK)]
    while len(groups) > 2:
        nxt = [_bitonic_merge_desc(_top_of_union(groups[i], groups[i + 1]))
               for i in range(0, len(groups) - 1, 2)]
        if len(groups) % 2:
            nxt.append(groups[-1])
        groups = nxt
    top = _top_of_union(groups[0], groups[1]) if len(groups) == 2 else groups[0]
    tau = functools.reduce(jnp.minimum, top)
    m1, m2 = c1[0], c2[0]
    mx = m1 + m2
    z = functools.reduce(lambda a, b: a + b, [jnp.exp(t - mx) for t in top])
    inv_z = 1.0 / z
    counts = []
    for a in range(PEER_TOPK):
        n = jnp.zeros((SUBLANES, tq), F32)
        for b in range(PEER_TOPK):
            if (a, b) in cand:
                n = n + jnp.where(cand[(a, b)] >= tau, 1.0, 0.0)
        counts.append(n)
    grouped = (N_KEYS // SUBLANES, SUBLANES, tq)
    for hd in range(PEER_HEADS):
        s1 = s1_ref[hd]
        s2 = s2_ref[hd]
        n1 = jnp.zeros((N_KEYS, tq), F32)
        rank2 = jnp.full((N_KEYS, tq), float(PEER_TOPK), F32)
        for r in reversed(range(PEER_TOPK)):
            n1 = jnp.where(s1 == c1[r][hd:hd + 1, :], counts[r][hd:hd + 1, :], n1)
            rank2 = jnp.where(s2 == c2[r][hd:hd + 1, :], float(r), rank2)
        n1_ref[hd] = n1.reshape(grouped)
        r2_ref[hd] = pltpu.bitcast(rank2.astype(BF16), jnp.uint32)
        e1 = jnp.exp(s1 - m1[hd:hd + 1, :]) * inv_z[hd:hd + 1, :]
        e1_ref[hd] = e1.reshape(grouped)
        e2_ref[hd] = pltpu.bitcast(jnp.exp(s2 - m2[hd:hd + 1, :]).astype(BF16), jnp.uint32)


def _peer_prep_call(x, g, scale, shift, wq, keys, l):
    bsz, t, d = x.shape
    tq = min(256, t)
    nt = t // tq
    ttot = bsz * t
    score_spec = pl.BlockSpec((PEER_HEADS, N_KEYS // 2, tq), lambda b, i: (0, 0, b * nt + i))
    score_shape = jax.ShapeDtypeStruct((PEER_HEADS, N_KEYS // 2, ttot), jnp.uint32)
    n_grp = N_KEYS // SUBLANES
    grouped_spec = pl.BlockSpec((PEER_HEADS, n_grp, SUBLANES, tq), lambda b, i: (0, 0, 0, b * nt + i))
    grouped_shape = jax.ShapeDtypeStruct((PEER_HEADS, n_grp, SUBLANES, ttot), F32)
    return pl.pallas_call(
        _peer_prep_kernel,
        grid=(bsz, nt),
        in_specs=[
            pl.BlockSpec((1, tq, d), lambda b, i: (b, i, 0)),
            pl.BlockSpec((1, d), lambda b, i: (0, 0)),
            pl.BlockSpec((1, 1, d), lambda b, i: (b, 0, 0)),
            pl.BlockSpec((1, 1, d), lambda b, i: (b, 0, 0)),
            pl.BlockSpec((None,) + wq.shape[1:], lambda b, i: (l, 0, 0)),
            pl.BlockSpec((None,) + keys.shape[1:], lambda b, i: (l, 0, 0, 0, 0)),
        ],
        out_specs=[pl.BlockSpec((d, tq), lambda b, i: (0, b * nt + i)),
                   grouped_spec, grouped_spec, score_spec, score_spec],
        out_shape=[jax.ShapeDtypeStruct((d, ttot), BF16),
                   grouped_shape, grouped_shape, score_shape, score_shape],
        scratch_shapes=[pltpu.VMEM((PEER_HEADS, N_KEYS, tq), F32),
                        pltpu.VMEM((PEER_HEADS, N_KEYS, tq), F32)],
        compiler_params=_params("parallel", "parallel"),
        name="peer_prep",
    )(x, g, scale, shift, wq, keys)


PEER_TE = SUBLANES * N_KEYS
SQRT_HALF = math.sqrt(0.5)


def _peer_dense_kernel(ht_ref, u_ref, vt_ref, n1_ref, e1_ref, r2_ref, e2_ref,
                       x_ref, g2_ref, o_ref, acc_ref, a_ref, gt_ref, *, n_e, n_steps):
    s = pl.program_id(0)
    tt = ht_ref.shape[1]
    n_lane_tiles = tt // LANES
    n_blk = N_KEYS // PACKED_ROWS

    @pl.when(s == 0)
    def _():
        a_ref[...] = jnp.zeros_like(a_ref)
        gt_ref[...] = jnp.zeros_like(gt_ref)

    @pl.when(jnp.logical_or(s == 0, (s - 2) % n_e == 0))
    def _():
        acc_ref[...] = jnp.zeros_like(acc_ref)

    acc_ref[...] += jnp.dot(vt_ref[0], gt_ref[...], preferred_element_type=F32)

    grp = jnp.clip(s - 1, 0, n_steps - 1) % n_e
    packed = (PACKED_ROWS, LANES)
    for il in range(SUBLANES):
        for j in range(n_lane_tiles):
            lanes = pl.ds(j * LANES, LANES)
            acc = [jnp.zeros(packed, BF16)] * n_blk
            for hd in range(PEER_HEADS):
                n1b = jnp.broadcast_to(n1_ref[hd, grp, pl.ds(il, 1), lanes], packed).astype(BF16)
                e1b = jnp.broadcast_to(e1_ref[hd, grp, pl.ds(il, 1), lanes], packed).astype(BF16)
                for k in range(n_blk):
                    rows = pl.ds(k * SUBLANES, SUBLANES)
                    rank2 = pltpu.bitcast(r2_ref[hd, rows, lanes], BF16)
                    e2 = pltpu.bitcast(e2_ref[hd, rows, lanes], BF16)
                    acc[k] = acc[k] + jnp.where(rank2 < n1b, e2, 0.0) * e1b
            for k in range(n_blk):
                rows = pl.ds(il * N_KEYS + k * PACKED_ROWS, PACKED_ROWS)
                a = a_ref[rows, lanes]
                gelu = 0.5 * a * (1.0 + lax.erf(a * SQRT_HALF))
                gt_ref[rows, lanes] = acc[k] * gelu.astype(BF16)

    a_ref[...] = jnp.dot(u_ref[...], ht_ref[...], preferred_element_type=F32)

    @pl.when(jnp.logical_and(s >= 2, (s - 2) % n_e == n_e - 1))
    def _():
        o_ref[...] = x_ref[...] + g2_ref[0] * acc_ref[...].T


def _peer_dense_call(ht, u, vt, n1, e1, r2, e2, x, g2, t_per_batch, l):
    ttot, d = x.shape
    tt = min(512, t_per_batch)
    n_e = N_EXPERTS // PEER_TE
    n_steps = (ttot // tt) * n_e
    per_batch = t_per_batch // tt
    tile = lambda s, lag: jnp.clip(s - lag, 0, n_steps - 1) // n_e
    group = lambda s, lag: jnp.clip(s - lag, 0, n_steps - 1) % n_e
    score_spec = pl.BlockSpec((PEER_HEADS, N_KEYS // 2, tt), lambda s: (0, 0, tile(s, 1)))
    grouped_spec = pl.BlockSpec((PEER_HEADS, N_KEYS // SUBLANES, SUBLANES, tt),
                                lambda s: (0, 0, 0, tile(s, 1)))
    return pl.pallas_call(
        functools.partial(_peer_dense_kernel, n_e=n_e, n_steps=n_steps),
        grid=(n_steps + 2,),
        in_specs=[
            pl.BlockSpec((d, tt), lambda s: (0, tile(s, 0))),
            pl.BlockSpec((None, PEER_TE, d), lambda s: (l, group(s, 0), 0)),
            pl.BlockSpec((None, 1, d, PEER_TE), lambda s: (l, group(s, 2), 0, 0)),
            grouped_spec, grouped_spec, score_spec, score_spec,
            pl.BlockSpec((tt, d), lambda s: (tile(s, 2), 0)),
            pl.BlockSpec((1, 1, d), lambda s: (tile(s, 2) // per_batch, 0, 0)),
        ],
        out_specs=pl.BlockSpec((tt, d), lambda s: (tile(s, 2), 0)),
        out_shape=jax.ShapeDtypeStruct((ttot, d), F32),
        scratch_shapes=[pltpu.VMEM((d, tt), F32),
                        pltpu.VMEM((PEER_TE, tt), F32),
                        pltpu.VMEM((PEER_TE, tt), BF16)],
        compiler_params=_params("arbitrary"),
        name="peer_dense",
    )(ht, u, vt, n1, e1, r2, e2, x, g2)


def _final_norm_kernel(x_ref, g_ref, o_ref):
    x = x_ref[...]
    ms = jnp.mean(x * x, axis=-1, keepdims=True)
    o_ref[...] = x * lax.rsqrt(ms + EPS) * g_ref[...]


def _final_norm_call(x, g):
    n, d = x.shape
    tm = 1024
    return pl.pallas_call(
        _final_norm_kernel,
        grid=(n // tm,),
        in_specs=[pl.BlockSpec((tm, d), lambda i: (i, 0)), pl.BlockSpec((1, d), lambda i: (0, 0))],
        out_specs=pl.BlockSpec((tm, d), lambda i: (i, 0)),
        out_shape=jax.ShapeDtypeStruct((n, d), F32),
        compiler_params=_params("parallel"),
        name="final_norm",
    )(x, g)


def _mixer(x, mod, l, norm1_g, w_in, conv_w, w_out, lbf, lbb, s0f, s0b, grid_conv, full):
    sh1, sc1, g1 = mod[0], mod[1], mod[2]
    gates, feats = _proj_call(x, norm1_g[l][None, :], sc1, sh1, w_in, l)
    o_f, o_b, s_f, s_b = _gla_call(gates, feats, lbf, lbb, s0f, s0b)
    if not full:
        return None, s_f, s_b
    x = _mixout_call(o_f, o_b, feats, conv_w[l], w_out[l], x, g1, grid_conv)
    return x, s_f, s_b


def _peer(x, mod, l, norm2_g, wq, keys, u, vt):
    bsz, t, d = x.shape
    sh2, sc2, g2 = mod[3], mod[4], mod[5]
    ht, n1, e1, r2, e2 = _peer_prep_call(x, norm2_g[l][None, :], sc2, sh2, wq, keys, l)
    out = _peer_dense_call(ht, u, vt, n1, e1, r2, e2, x.reshape(bsz * t, d), g2, t, l)
    return out.reshape(bsz, t, d)


def kernel(x, c, ctx, c_ctx, w_mod, b_mod, norm1_g, norm2_g, w_in, conv_w, w_out, lb_logits,
           peer_wq, peer_subkeys, peer_u, peer_v, final_g):
    bsz, t, d = x.shape
    depth = w_mod.shape[0]

    p_lb = jax.nn.softmax(lb_logits.astype(F32), axis=0)
    lower = jnp.cumsum(p_lb, axis=0) - p_lb[0]

    cond = jnp.zeros((COND_ROWS, d), F32).at[:bsz].set(c).at[bsz].set(c_ctx)
    mod = _mod_call(cond, w_mod, b_mod)
    mod = mod.reshape(depth, COND_ROWS, N_MOD, d)
    mod_x = jnp.transpose(mod[:, :bsz], (0, 2, 1, 3))[:, :, :, None, :]
    mod_c = jnp.broadcast_to(mod[:, bsz][:, :, None, None, :], mod_x.shape)

    w_in_b = w_in.astype(BF16)
    w_out_b = w_out.astype(BF16)
    wq_b = peer_wq.astype(BF16)
    keys_b = peer_subkeys.astype(BF16)
    u_b = peer_u.astype(BF16)
    vt_b = jnp.swapaxes(peer_v.astype(BF16).reshape(depth, N_EXPERTS // PEER_TE, PEER_TE, d), 2, 3)

    zero_state = jnp.zeros((bsz, HG_HEADS, HG_DIM, HG_DIM), F32)
    xc = ctx
    for l in range(depth):
        lbf = lower[l, 0][None, :]
        lbb = lower[l, 1][None, :]
        full = l < depth - 1
        xc_new, s_f, s_b = _mixer(xc, mod_c[l], l, norm1_g, w_in_b, conv_w, w_out_b, lbf, lbb,
                                  zero_state, zero_state, False, full)
        if full:
            xc = _peer(xc_new, mod_c[l], l, norm2_g, wq_b, keys_b, u_b, vt_b)
        x, _, _ = _mixer(x, mod_x[l], l, norm1_g, w_in_b, conv_w, w_out_b, lbf, lbb,
                         s_f, s_b, True, True)
        x = _peer(x, mod_x[l], l, norm2_g, wq_b, keys_b, u_b, vt_b)
    return _final_norm_call(x.reshape(bsz * t, d), final_g[None, :]).reshape(bsz, t, d)
```

```python
import functools
import math

import numpy as np
import jax
import jax.numpy as jnp
from jax import lax
from jax.experimental import pallas as pl
from jax.experimental.pallas import tpu as pltpu

F32 = jnp.float32
BF16 = jnp.bfloat16

D_MODEL = 1024
GRID_W = 64
EPS = 1e-6
F_FLOOR = 1e-20
N_MOD = 6
HG_WIDTH = 512
HG_HEADS = 4
HG_DIM = HG_WIDTH // HG_HEADS
SC_WIDTH = 512
SC_HALF = SC_WIDTH // 2
IN_COLS = 5 * HG_WIDTH + 3 * SC_WIDTH
PEER_HEADS = 8
PEER_QDIM = 256
PEER_HALF = PEER_QDIM // 2
N_KEYS = 128
N_EXPERTS = N_KEYS * N_KEYS
PEER_TOPK = 16

SUBLANES = 8
LANES = 128
PACKED_ROWS = 2 * SUBLANES
VMEM_LIMIT = 48 * 1024 * 1024

GLA_CHUNK = 128
GLA_LEVELS = (64, 32, 16, 8, 4, 2, 1)
GLA_GROUP_HEADS = 4
COND_ROWS = 8

COL_IV, COL_ZF, COL_ZB, COL_Q, COL_G, COL_CG, COL_BG, COL_HV = range(8)
GATE_ZF, GATE_ZB = range(2)
FEAT_IV, FEAT_Q, FEAT_G, FEAT_CG, FEAT_BG, FEAT_HV = range(6)


def _params(*sem):
    return pltpu.CompilerParams(dimension_semantics=sem, vmem_limit_bytes=VMEM_LIMIT)


def _dot(a, b):
    return jnp.dot(a.astype(BF16), b.astype(BF16), preferred_element_type=F32)


def _dot_nt(a, b):
    return lax.dot_general(a.astype(BF16), b.astype(BF16), (((1,), (1,)), ((), ())),
                           preferred_element_type=F32)


def _dot_tn(a, b):
    return lax.dot_general(a.astype(BF16), b.astype(BF16), (((0,), (0,)), ((), ())),
                           preferred_element_type=F32)


def _mod_kernel(cond_ref, w_ref, b_ref, o_ref):
    c = cond_ref[...]
    s = c * jax.nn.sigmoid(c)
    o_ref[0] = _dot(s, w_ref[0]) + b_ref[0]


def _mod_call(cond, w_mod, b_mod):
    depth, d, n = w_mod.shape
    tn = 1536
    return pl.pallas_call(
        _mod_kernel,
        grid=(depth, n // tn),
        in_specs=[
            pl.BlockSpec((COND_ROWS, d), lambda l, j: (0, 0)),
            pl.BlockSpec((1, d, tn), lambda l, j: (l, 0, j)),
            pl.BlockSpec((1, 1, tn), lambda l, j: (l, 0, j)),
        ],
        out_specs=pl.BlockSpec((1, COND_ROWS, tn), lambda l, j: (l, 0, j)),
        out_shape=jax.ShapeDtypeStruct((depth, COND_ROWS, n), F32),
        compiler_params=_params("parallel", "parallel"),
        name="mod",
    )(cond, w_mod, b_mod.reshape(depth, 1, n))


def _norm_mod(x, g, scale, shift):
    ms = jnp.mean(x * x, axis=-1, keepdims=True)
    y = x * lax.rsqrt(ms + EPS) * g
    return y * (1.0 + scale) + shift


def _proj_kernel(x_ref, g_ref, sc_ref, sh_ref, w_ref, gates_ref, feats_ref):
    h = _norm_mod(x_ref[0], g_ref[...], sc_ref[0], sh_ref[0])
    y = _dot(h, w_ref[...])
    w = HG_WIDTH
    gates_ref[0] = y[:, COL_ZF * w:(COL_ZB + 1) * w]
    feats_ref[0, :, 0:w] = y[:, COL_IV * w:(COL_IV + 1) * w].astype(BF16)
    feats_ref[0, :, w:] = y[:, COL_Q * w:].astype(BF16)


def _proj_call(x, g, scale, shift, w, l):
    bsz, t, d = x.shape
    n = w.shape[2]
    n_gate = 2 * HG_WIDTH
    tm = min(256, t)
    return pl.pallas_call(
        _proj_kernel,
        grid=(bsz, t // tm),
        in_specs=[
            pl.BlockSpec((1, tm, d), lambda b, i: (b, i, 0)),
            pl.BlockSpec((1, d), lambda b, i: (0, 0)),
            pl.BlockSpec((1, 1, d), lambda b, i: (b, 0, 0)),
            pl.BlockSpec((1, 1, d), lambda b, i: (b, 0, 0)),
            pl.BlockSpec((None, d, n), lambda b, i: (l, 0, 0)),
        ],
        out_specs=[pl.BlockSpec((1, tm, n_gate), lambda b, i: (b, i, 0)),
                   pl.BlockSpec((1, tm, n - n_gate), lambda b, i: (b, i, 0))],
        out_shape=[jax.ShapeDtypeStruct((bsz, t, n_gate), F32),
                   jax.ShapeDtypeStruct((bsz, t, n - n_gate), BF16)],
        compiler_params=_params("parallel", "parallel"),
        name="proj",
    )(x, g, scale, shift, w)


def _gla_sum_matrix(backward):
    c = GLA_CHUNK
    t = np.arange(c)[:, None]
    s = np.arange(c)[None, :]
    mats = [s >= t if backward else s <= t]
    for n in GLA_LEVELS:
        if n >= SUBLANES:
            continue
        mid = (t // (2 * n)) * (2 * n) + n
        if not backward:
            m = np.where(t >= mid, (s >= mid) & (s <= t), (s > t) & (s < mid))
        else:
            m = np.where(t < mid, (s >= t) & (s < mid), (s >= mid) & (s < t))
        mats.append(m)
    return np.concatenate([m.astype(np.float32) for m in mats], axis=0)


def _gla_masks(backward):
    c = GLA_CHUNK
    row = lax.broadcasted_iota(jnp.int32, (c, c), 0)
    col = lax.broadcasted_iota(jnp.int32, (c, c), 1)
    qrow, lvl = [], []
    for n in GLA_LEVELS:
        r_hi = (row & (2 * n - 1)) >= n
        c_hi = (col & (2 * n - 1)) >= n
        same = (row & ~(2 * n - 1)) == (col & ~(2 * n - 1))
        if not backward:
            qrow.append(r_hi)
            lvl.append(same & r_hi & jnp.logical_not(c_hi))
        else:
            qrow.append(jnp.logical_not(r_hi))
            lvl.append(same & jnp.logical_not(r_hi) & c_hi)
    return qrow, lvl, row == col


def _gla_gates(z, lb, msum):
    sig = jax.nn.sigmoid(z)
    f = lb + (1.0 - lb) * sig
    logf = jnp.log(jnp.maximum(f, F_FLOOR))
    kk = (1.0 - lb) * jax.nn.sigmoid(-z)
    sums = jnp.dot(msum, logf.astype(BF16), preferred_element_type=F32)
    return kk, sums


def _boundary_rows(b, n, backward):
    c, w = b.shape
    parts = []
    for start in range(0, c, 2 * n):
        r = start + n if backward else start + n - 1
        parts.append(jnp.broadcast_to(b[r:r + 1, :], (2 * n, w)))
    return jnp.concatenate(parts, axis=0)


def _gla_chains(chains, ones):
    c = GLA_CHUNK
    atts = [jnp.where(masks[2], jnp.dot((q * kk).astype(BF16), ones, preferred_element_type=F32), 0.0)
            for q, v, kk, sums, st, masks, backward in chains]
    sub_tile = [n for n in GLA_LEVELS if n < SUBLANES]
    for li, n in enumerate(GLA_LEVELS):
        for ci, (q, v, kk, sums, st, masks, backward) in enumerate(chains):
            if n >= SUBLANES:
                exponent = -jnp.abs(sums[0:c] - _boundary_rows(sums[0:c], n, backward))
            else:
                k = 1 + sub_tile.index(n)
                exponent = sums[k * c:(k + 1) * c]
            mixed = (jnp.where(masks[0][li][:, :HG_DIM], q, kk) * jnp.exp(exponent)).astype(BF16)
            a = lax.dot_general(mixed, mixed, (((1,), (1,)), ((), ())), preferred_element_type=F32)
            atts[ci] = jnp.where(masks[1][li], a, atts[ci])
    out = []
    for att, (q, v, kk, sums, st, masks, backward) in zip(atts, chains):
        b = sums[0:c]
        b_end = b[0:1, :] if backward else b[c - 1:c, :]
        q_in = q * jnp.exp(b)
        o = _dot(att, v) + _dot_nt(q_in, st)
        k_out = kk * jnp.exp(b_end - b)
        out.append((o, jnp.exp(b_end) * st + _dot_tn(v, k_out)))
    return out


def _gla_kernel(ivf_ref, zf_ref, qf_ref, ivb_ref, zb_ref, qb_ref, lbf_ref, lbb_ref,
                s0f_ref, s0b_ref, mf_ref, mb_ref,
                of_ref, ob_ref, sf_ref, sb_ref, st_ref):
    n = pl.program_id(1)

    @pl.when(n == 0)
    def _():
        st_ref[0] = s0f_ref[0]
        st_ref[1] = s0b_ref[0]

    ones = jnp.ones((HG_DIM, HG_DIM), BF16)
    masks_f = _gla_masks(False)
    masks_b = _gla_masks(True)
    kk_f, sums_f = _gla_gates(zf_ref[0], lbf_ref[...], mf_ref[...])
    kk_b, sums_b = _gla_gates(zb_ref[0], lbb_ref[...], mb_ref[...])
    for h0 in range(0, HG_HEADS, GLA_GROUP_HEADS):
        heads = range(h0, h0 + GLA_GROUP_HEADS)
        sls = [slice(h * HG_DIM, (h + 1) * HG_DIM) for h in heads]
        chains = [(qf_ref[0, :, sl].astype(F32), ivf_ref[0, :, sl], kk_f[:, sl], sums_f[:, sl],
                   st_ref[0, h], masks_f, False) for h, sl in zip(heads, sls)]
        chains += [(qb_ref[0, :, sl].astype(F32), ivb_ref[0, :, sl], kk_b[:, sl], sums_b[:, sl],
                    st_ref[1, h], masks_b, True) for h, sl in zip(heads, sls)]
        res = _gla_chains(chains, ones)
        for i, (h, sl) in enumerate(zip(heads, sls)):
            of_ref[0, :, sl], st_ref[0, h] = res[i]
            ob_ref[0, :, sl], st_ref[1, h] = res[GLA_GROUP_HEADS + i]

    @pl.when(n == pl.num_programs(1) - 1)
    def _():
        sf_ref[0] = st_ref[0]
        sb_ref[0] = st_ref[1]


def _gla_call(gates, feats, lbf, lbb, s0f, s0b):
    bsz, t, _ = gates.shape
    c = GLA_CHUNK
    nc = t // c
    w = HG_WIDTH
    msum_f = jnp.asarray(_gla_sum_matrix(False), BF16)
    msum_b = jnp.asarray(_gla_sum_matrix(True), BF16)

    def fwd(col):
        return pl.BlockSpec((1, c, w), lambda b, n: (b, n, col))

    def bwd(col):
        return pl.BlockSpec((1, c, w), lambda b, n: (b, nc - 1 - n, col))

    state_spec = pl.BlockSpec((1, HG_HEADS, HG_DIM, HG_DIM), lambda b, n: (b, 0, 0, 0))
    const = lambda shape: pl.BlockSpec(shape, lambda b, n: (0,) * len(shape))
    return pl.pallas_call(
        _gla_kernel,
        grid=(bsz, nc),
        in_specs=[fwd(FEAT_IV), fwd(GATE_ZF), fwd(FEAT_Q), bwd(FEAT_IV), bwd(GATE_ZB), bwd(FEAT_Q),
                  const((1, w)), const((1, w)), state_spec, state_spec,
                  const(msum_f.shape), const(msum_b.shape)],
        out_specs=[pl.BlockSpec((1, c, w), lambda b, n: (b, n, 0)),
                   pl.BlockSpec((1, c, w), lambda b, n: (b, nc - 1 - n, 0)),
                   state_spec, state_spec],
        out_shape=[jax.ShapeDtypeStruct((bsz, t, w), F32),
                   jax.ShapeDtypeStruct((bsz, t, w), F32),
                   jax.ShapeDtypeStruct((bsz, HG_HEADS, HG_DIM, HG_DIM), F32),
                   jax.ShapeDtypeStruct((bsz, HG_HEADS, HG_DIM, HG_DIM), F32)],
        scratch_shapes=[pltpu.VMEM((2, HG_HEADS, HG_DIM, HG_DIM), F32)],
        compiler_params=_params("parallel", "arbitrary"),
        name="gla",
    )(feats, gates, feats, feats, gates, feats, lbf, lbb, s0f, s0b, msum_f, msum_b)


def _mixout_body(of_ref, ob_ref, g_ref, cg_ref, bg_ref, hv_ref, halo, cw_ref, wo_ref,
                 x_ref, g1_ref, o_ref, grid_conv):
    tm = of_ref.shape[1]
    o = of_ref[0] + ob_ref[0]
    heads = []
    for h in range(HG_HEADS):
        oh = o[:, h * HG_DIM:(h + 1) * HG_DIM]
        heads.append(oh * lax.rsqrt(jnp.mean(oh * oh, axis=-1, keepdims=True) + EPS))
    g = g_ref[0].astype(F32)
    o_rec = jnp.concatenate(heads, axis=-1) * (g * jax.nn.sigmoid(g))

    u = cg_ref[0].astype(F32) * hv_ref[0].astype(F32)
    w0, w1, w2 = cw_ref[0:1, :], cw_ref[1:2, :], cw_ref[2:3, :]
    pos = lax.broadcasted_iota(jnp.int32, (tm, 1), 0)
    if grid_conv:
        cgp_ref, hvp_ref, cgn_ref, hvn_ref = halo
        i = pl.program_id(1)
        last = pl.num_programs(1) - 1
        colpos = pos & (GRID_W - 1)
        uh = u[:, :SC_HALF]
        left = jnp.where(colpos == 0, 0.0, pltpu.roll(uh, 1, 0))
        right = jnp.where(colpos == GRID_W - 1, 0.0, pltpu.roll(uh, tm - 1, 0))
        conv_h = w0[:, :SC_HALF] * left + w1[:, :SC_HALF] * uh + w2[:, :SC_HALF] * right
        uv = u[:, SC_HALF:]
        halo_u = lambda cg, hv: cg[0][:, SC_HALF:].astype(F32) * hv[0][:, SC_HALF:].astype(F32)
        up_halo = jnp.where(i == 0, 0.0, halo_u(cgp_ref, hvp_ref))
        dn_halo = jnp.where(i == last, 0.0, halo_u(cgn_ref, hvn_ref))
        up = jnp.concatenate([up_halo, uv[:tm - GRID_W]], axis=0)
        down = jnp.concatenate([uv[GRID_W:], dn_halo], axis=0)
        conv_v = w0[:, SC_HALF:] * up + w1[:, SC_HALF:] * uv + w2[:, SC_HALF:] * down
        conv = jnp.concatenate([conv_h, conv_v], axis=-1)
    else:
        left = jnp.where(pos == 0, 0.0, pltpu.roll(u, 1, 0))
        right = jnp.where(pos == tm - 1, 0.0, pltpu.roll(u, tm - 1, 0))
        conv = w0 * left + w1 * u + w2 * right
    o_conv = bg_ref[0].astype(F32) * conv
    y = _dot(o_rec, wo_ref[0:HG_WIDTH, :]) + _dot(o_conv, wo_ref[HG_WIDTH:, :])
    o_ref[0] = x_ref[0] + g1_ref[0] * y


def _mixout_grid_kernel(of_ref, ob_ref, g_ref, cg_ref, bg_ref, hv_ref, cgp_ref, hvp_ref,
                        cgn_ref, hvn_ref, cw_ref, wo_ref, x_ref, g1_ref, o_ref):
    _mixout_body(of_ref, ob_ref, g_ref, cg_ref, bg_ref, hv_ref,
                 (cgp_ref, hvp_ref, cgn_ref, hvn_ref), cw_ref, wo_ref, x_ref, g1_ref, o_ref, True)


def _mixout_seq_kernel(of_ref, ob_ref, g_ref, cg_ref, bg_ref, hv_ref, cw_ref, wo_ref,
                       x_ref, g1_ref, o_ref):
    _mixout_body(of_ref, ob_ref, g_ref, cg_ref, bg_ref, hv_ref, None, cw_ref, wo_ref,
                 x_ref, g1_ref, o_ref, False)


def _mixout_call(o_f, o_b, proj, conv_w, w_out, x, g1, grid_conv):
    bsz, t, d = x.shape
    w = HG_WIDTH
    tm = 512 if grid_conv else t
    nt = t // tm
    hb = tm // GRID_W
    nhalo = t // GRID_W

    def col(cidx):
        return pl.BlockSpec((1, tm, w), lambda b, i: (b, i, cidx))

    def prev(cidx):
        return pl.BlockSpec((1, GRID_W, w), lambda b, i: (b, jnp.maximum(i * hb - 1, 0), cidx))

    def nxt(cidx):
        return pl.BlockSpec((1, GRID_W, w),
                            lambda b, i: (b, jnp.minimum((i + 1) * hb, nhalo - 1), cidx))

    row = pl.BlockSpec((1, tm, w), lambda b, i: (b, i, 0))
    in_specs = [row, row, col(FEAT_G), col(FEAT_CG), col(FEAT_BG), col(FEAT_HV)]
    args = [o_f, o_b, proj, proj, proj, proj]
    if grid_conv:
        in_specs += [prev(FEAT_CG), prev(FEAT_HV), nxt(FEAT_CG), nxt(FEAT_HV)]
        args += [proj, proj, proj, proj]
    in_specs += [
        pl.BlockSpec((3, w), lambda b, i: (0, 0)),
        pl.BlockSpec(w_out.shape, lambda b, i: (0, 0)),
        pl.BlockSpec((1, tm, d), lambda b, i: (b, i, 0)),
        pl.BlockSpec((1, 1, d), lambda b, i: (b, 0, 0)),
    ]
    args += [conv_w, w_out, x, g1]
    return pl.pallas_call(
        _mixout_grid_kernel if grid_conv else _mixout_seq_kernel,
        grid=(bsz, nt),
        in_specs=in_specs,
        out_specs=pl.BlockSpec((1, tm, d), lambda b, i: (b, i, 0)),
        out_shape=jax.ShapeDtypeStruct((bsz, t, d), F32),
        compiler_params=_params("parallel", "parallel"),
        name="mixout_grid" if grid_conv else "mixout_seq",
    )(*args)


def _oddeven_merge_sort_pairs(n):
    pairs = []
    p = 1
    while p < n:
        k = p
        while k >= 1:
            for j in range(k % p, n - k, 2 * k):
                for i in range(min(k, n - j - k)):
                    if (i + j) // (2 * p) == (i + j + k) // (2 * p):
                        pairs.append((i + j, i + j + k))
            k //= 2
        p *= 2
    return pairs


_SORT16 = _oddeven_merge_sort_pairs(PEER_TOPK)


def _sort_desc(xs):
    xs = list(xs)
    for i, j in _SORT16:
        a, b = xs[i], xs[j]
        xs[i], xs[j] = jnp.maximum(a, b), jnp.minimum(a, b)
    return xs


def _bitonic_merge_desc(xs):
    xs = list(xs)
    d = len(xs) // 2
    while d >= 1:
        for i in range(len(xs)):
            if i & d == 0:
                a, b = xs[i], xs[i + d]
                xs[i], xs[i + d] = jnp.maximum(a, b), jnp.minimum(a, b)
        d //= 2
    return xs


def _top_of_union(a, b):
    k = len(a)
    return [jnp.maximum(a[r], b[k - 1 - r]) for r in range(k)]


def _topk_rows(s):
    rows = [s[SUBLANES * v:SUBLANES * (v + 1), :] for v in range(N_KEYS // SUBLANES)]
    rows = _sort_desc(rows)
    for shift in (4, 2, 1):
        rolled = [pltpu.roll(r, shift, 0) for r in rows]
        rows = _bitonic_merge_desc(_top_of_union(rows, rolled))
    return rows


def _peer_prep_kernel(x_ref, g_ref, sc_ref, sh_ref, wq_ref, keys_ref,
                      ht_ref, n1_ref, e1_ref, r2_ref, e2_ref, s1_ref, s2_ref):
    tq = x_ref.shape[1]
    hf = _norm_mod(x_ref[0], g_ref[...], sc_ref[0], sh_ref[0])
    ht_ref[...] = hf.T.astype(BF16)
    qf = jnp.dot(hf.astype(BF16), wq_ref[...], preferred_element_type=F32)
    sub = lax.broadcasted_iota(jnp.int32, (SUBLANES, tq), 0)
    neg = jnp.full((SUBLANES, tq), -jnp.inf, F32)
    packed = [[jnp.zeros((SUBLANES, tq), F32)] * PEER_TOPK for _ in range(2)]
    for hd in range(PEER_HEADS):
        for p in range(2):
            lo = hd * PEER_QDIM + p * PEER_HALF
            s = _dot_nt(keys_ref[hd, p], qf[:, lo:lo + PEER_HALF])
            (s1_ref if p == 0 else s2_ref)[hd] = s
            top = _topk_rows(s)
            packed[p] = [jnp.where(sub == hd, top[r], packed[p][r]) for r in range(PEER_TOPK)]
    c1, c2 = packed
    pairs = [(a, b) for a in range(PEER_TOPK) for b in range(PEER_TOPK)
             if (a + 1) * (b + 1) <= PEER_TOPK]
    cand = {ab: c1[ab[0]] + c2[ab[1]] for ab in pairs}
    cands = [cand[ab] for ab in pairs]
    cands += [neg] * (-len(cands) % PEER_TOPK)
    groups = [_sort_desc(cands[i:i + PEER_TOPK]) for i in range(0, len(cands), PEER_TOPK)]
    while len(groups) > 2:
        nxt = [_bitonic_merge_desc(_top_of_union(groups[i], groups[i + 1]))
               for i in range(0, len(groups) - 1, 2)]
        if len(groups) % 2:
            nxt.append(groups[-1])
        groups = nxt
    top = _top_of_union(groups[0], groups[1]) if len(groups) == 2 else groups[0]
    tau = functools.reduce(jnp.minimum, top)
    m1, m2 = c1[0], c2[0]
    mx = m1 + m2
    z = functools.reduce(lambda a, b: a + b, [jnp.exp(t - mx) for t in top])
    inv_z = 1.0 / z
    counts = []
    for a in range(PEER_TOPK):
        n = jnp.zeros((SUBLANES, tq), F32)
        for b in range(PEER_TOPK):
            if (a, b) in cand:
                n = n + jnp.where(cand[(a, b)] >= tau, 1.0, 0.0)
        counts.append(n)
    grouped = (N_KEYS // SUBLANES, SUBLANES, tq)
    for hd in range(PEER_HEADS):
        s1 = s1_ref[hd]
        s2 = s2_ref[hd]
        n1 = jnp.zeros((N_KEYS, tq), F32)
        rank2 = jnp.full((N_KEYS, tq), float(PEER_TOPK), F32)
        for r in reversed(range(PEER_TOPK)):
            n1 = jnp.where(s1 == c1[r][hd:hd + 1, :], counts[r][hd:hd + 1, :], n1)
            rank2 = jnp.where(s2 == c2[r][hd:hd + 1, :], float(r), rank2)
        n1_ref[hd] = n1.reshape(grouped)
        r2_ref[hd] = pltpu.bitcast(rank2.astype(BF16), jnp.uint32)
        e1 = jnp.exp(s1 - m1[hd:hd + 1, :]) * inv_z[hd:hd + 1, :]
        e1_ref[hd] = e1.reshape(grouped)
        e2_ref[hd] = pltpu.bitcast(jnp.exp(s2 - m2[hd:hd + 1, :]).astype(BF16), jnp.uint32)


def _peer_prep_call(x, g, scale, shift, wq, keys, l):
    bsz, t, d = x.shape
    tq = min(256, t)
    nt = t // tq
    ttot = bsz * t
    score_spec = pl.BlockSpec((PEER_HEADS, N_KEYS // 2, tq), lambda b, i: (0, 0, b * nt + i))
    score_shape = jax.ShapeDtypeStruct((PEER_HEADS, N_KEYS // 2, ttot), jnp.uint32)
    n_grp = N_KEYS // SUBLANES
    grouped_spec = pl.BlockSpec((PEER_HEADS, n_grp, SUBLANES, tq), lambda b, i: (0, 0, 0, b * nt + i))
    grouped_shape = jax.ShapeDtypeStruct((PEER_HEADS, n_grp, SUBLANES, ttot), F32)
    return pl.pallas_call(
        _peer_prep_kernel,
        grid=(bsz, nt),
        in_specs=[
            pl.BlockSpec((1, tq, d), lambda b, i: (b, i, 0)),
            pl.BlockSpec((1, d), lambda b, i: (0, 0)),
            pl.BlockSpec((1, 1, d), lambda b, i: (b, 0, 0)),
            pl.BlockSpec((1, 1, d), lambda b, i: (b, 0, 0)),
            pl.BlockSpec((None,) + wq.shape[1:], lambda b, i: (l, 0, 0)),
            pl.BlockSpec((None,) + keys.shape[1:], lambda b, i: (l, 0, 0, 0, 0)),
        ],
        out_specs=[pl.BlockSpec((d, tq), lambda b, i: (0, b * nt + i)),
                   grouped_spec, grouped_spec, score_spec, score_spec],
        out_shape=[jax.ShapeDtypeStruct((d, ttot), BF16),
                   grouped_shape, grouped_shape, score_shape, score_shape],
        scratch_shapes=[pltpu.VMEM((PEER_HEADS, N_KEYS, tq), F32),
                        pltpu.VMEM((PEER_HEADS, N_KEYS, tq), F32)],
        compiler_params=_params("parallel", "parallel"),
        name="peer_prep",
    )(x, g, scale, shift, wq, keys)


PEER_TE = SUBLANES * N_KEYS
SQRT_HALF = math.sqrt(0.5)


def _peer_dense_kernel(ht_ref, u_ref, vt_ref, n1_ref, e1_ref, r2_ref, e2_ref,
                       x_ref, g2_ref, o_ref, acc_ref, a_ref, gt_ref, *, n_e, n_steps):
    s = pl.program_id(0)
    tt = ht_ref.shape[1]
    n_lane_tiles = tt // LANES
    n_blk = N_KEYS // PACKED_ROWS

    @pl.when(s == 0)
    def _():
        a_ref[...] = jnp.zeros_like(a_ref)
        gt_ref[...] = jnp.zeros_like(gt_ref)

    @pl.when(jnp.logical_or(s == 0, (s - 2) % n_e == 0))
    def _():
        acc_ref[...] = jnp.zeros_like(acc_ref)

    acc_ref[...] += jnp.dot(vt_ref[0], gt_ref[...], preferred_element_type=F32)

    grp = jnp.clip(s - 1, 0, n_steps - 1) % n_e
    packed = (PACKED_ROWS, LANES)
    for il in range(SUBLANES):
        for j in range(n_lane_tiles):
            lanes = pl.ds(j * LANES, LANES)
            acc = [None] * n_blk
            for hd in range(PEER_HEADS):
                n1b = jnp.broadcast_to(n1_ref[hd, grp, pl.ds(il, 1), lanes], packed).astype(BF16)
                e1b = jnp.broadcast_to(e1_ref[hd, grp, pl.ds(il, 1), lanes], packed).astype(BF16)
                for k in range(n_blk):
                    rows = pl.ds(k * SUBLANES, SUBLANES)
                    rank2 = pltpu.bitcast(r2_ref[hd, rows, lanes], BF16)
                    e2 = pltpu.bitcast(e2_ref[hd, rows, lanes], BF16)
                    w = jnp.where(rank2 < n1b, e2, 0.0) * e1b
                    acc[k] = w if hd == 0 else acc[k] + w
            for k in range(n_blk):
                rows = pl.ds(il * N_KEYS + k * PACKED_ROWS, PACKED_ROWS)
                a = a_ref[rows, lanes].astype(BF16)
                gelu = (0.5 * a) * (1.0 + lax.erf(a * SQRT_HALF))
                gt_ref[rows, lanes] = acc[k] * gelu

    a_ref[...] = jnp.dot(u_ref[...], ht_ref[...], preferred_element_type=F32)

    @pl.when(jnp.logical_and(s >= 2, (s - 2) % n_e == n_e - 1))
    def _():
        o_ref[...] = x_ref[...] + g2_ref[0] * acc_ref[...].T


def _peer_dense_call(ht, u, vt, n1, e1, r2, e2, x, g2, t_per_batch, l):
    ttot, d = x.shape
    tt = min(512, t_per_batch)
    n_e = N_EXPERTS // PEER_TE
    n_steps = (ttot // tt) * n_e
    per_batch = t_per_batch // tt
    tile = lambda s, lag: jnp.clip(s - lag, 0, n_steps - 1) // n_e
    group = lambda s, lag: jnp.clip(s - lag, 0, n_steps - 1) % n_e
    score_spec = pl.BlockSpec((PEER_HEADS, N_KEYS // 2, tt), lambda s: (0, 0, tile(s, 1)))
    grouped_spec = pl.BlockSpec((PEER_HEADS, N_KEYS // SUBLANES, SUBLANES, tt),
                                lambda s: (0, 0, 0, tile(s, 1)))
    return pl.pallas_call(
        functools.partial(_peer_dense_kernel, n_e=n_e, n_steps=n_steps),
        grid=(n_steps + 2,),
        in_specs=[
            pl.BlockSpec((d, tt), lambda s: (0, tile(s, 0))),
            pl.BlockSpec((None, PEER_TE, d), lambda s: (l, group(s, 0), 0)),
            pl.BlockSpec((None, 1, d, PEER_TE), lambda s: (l, group(s, 2), 0, 0)),
            grouped_spec, grouped_spec, score_spec, score_spec,
            pl.BlockSpec((tt, d), lambda s: (tile(s, 2), 0)),
            pl.BlockSpec((1, 1, d), lambda s: (tile(s, 2) // per_batch, 0, 0)),
        ],
        out_specs=pl.BlockSpec((tt, d), lambda s: (tile(s, 2), 0)),
        out_shape=jax.ShapeDtypeStruct((ttot, d), F32),
        scratch_shapes=[pltpu.VMEM((d, tt), F32),
                        pltpu.VMEM((PEER_TE, tt), F32),
                        pltpu.VMEM((PEER_TE, tt), BF16)],
        compiler_params=_params("arbitrary"),
        name="peer_dense",
    )(ht, u, vt, n1, e1, r2, e2, x, g2)


def _final_norm_kernel(x_ref, g_ref, o_ref):
    x = x_ref[...]
    ms = jnp.mean(x * x, axis=-1, keepdims=True)
    o_ref[...] = x * lax.rsqrt(ms + EPS) * g_ref[...]


def _final_norm_call(x, g):
    n, d = x.shape
    tm = 1024
    return pl.pallas_call(
        _final_norm_kernel,
        grid=(n // tm,),
        in_specs=[pl.BlockSpec((tm, d), lambda i: (i, 0)), pl.BlockSpec((1, d), lambda i: (0, 0))],
        out_specs=pl.BlockSpec((tm, d), lambda i: (i, 0)),
        out_shape=jax.ShapeDtypeStruct((n, d), F32),
        compiler_params=_params("parallel"),
        name="final_norm",
    )(x, g)


def _mixer(x, mod, l, norm1_g, w_in, conv_w, w_out, lbf, lbb, s0f, s0b, grid_conv, full):
    sh1, sc1, g1 = mod[0], mod[1], mod[2]
    gates, feats = _proj_call(x, norm1_g[l][None, :], sc1, sh1, w_in, l)
    o_f, o_b, s_f, s_b = _gla_call(gates, feats, lbf, lbb, s0f, s0b)
    if not full:
        return None, s_f, s_b
    x = _mixout_call(o_f, o_b, feats, conv_w[l], w_out[l], x, g1, grid_conv)
    return x, s_f, s_b


def _peer(x, mod, l, norm2_g, wq, keys, u, vt):
    bsz, t, d = x.shape
    sh2, sc2, g2 = mod[3], mod[4], mod[5]
    ht, n1, e1, r2, e2 = _peer_prep_call(x, norm2_g[l][None, :], sc2, sh2, wq, keys, l)
    out = _peer_dense_call(ht, u, vt, n1, e1, r2, e2, x.reshape(bsz * t, d), g2, t, l)
    return out.reshape(bsz, t, d)


def kernel(x, c, ctx, c_ctx, w_mod, b_mod, norm1_g, norm2_g, w_in, conv_w, w_out, lb_logits,
           peer_wq, peer_subkeys, peer_u, peer_v, final_g):
    bsz, t, d = x.shape
    depth = w_mod.shape[0]

    p_lb = jax.nn.softmax(lb_logits.astype(F32), axis=0)
    lower = jnp.cumsum(p_lb, axis=0) - p_lb[0]

    cond = jnp.zeros((COND_ROWS, d), F32).at[:bsz].set(c).at[bsz].set(c_ctx)
    mod = _mod_call(cond, w_mod, b_mod)
    mod = mod.reshape(depth, COND_ROWS, N_MOD, d)
    mod_x = jnp.transpose(mod[:, :bsz], (0, 2, 1, 3))[:, :, :, None, :]
    mod_c = jnp.broadcast_to(mod[:, bsz][:, :, None, None, :], mod_x.shape)

    w_in_b = w_in.astype(BF16)
    w_out_b = w_out.astype(BF16)
    wq_b = peer_wq.astype(BF16)
    keys_b = peer_subkeys.astype(BF16)
    u_b = peer_u.astype(BF16)
    vt_b = jnp.swapaxes(peer_v.astype(BF16).reshape(depth, N_EXPERTS // PEER_TE, PEER_TE, d), 2, 3)

    zero_state = jnp.zeros((bsz, HG_HEADS, HG_DIM, HG_DIM), F32)
    xc = ctx
    for l in range(depth):
        lbf = lower[l, 0][None, :]
        lbb = lower[l, 1][None, :]
        full = l < depth - 1
        xc_new, s_f, s_b = _mixer(xc, mod_c[l], l, norm1_g, w_in_b, conv_w, w_out_b, lbf, lbb,
                                  zero_state, zero_state, False, full)
        if full:
            xc = _peer(xc_new, mod_c[l], l, norm2_g, wq_b, keys_b, u_b, vt_b)
        x, _, _ = _mixer(x, mod_x[l], l, norm1_g, w_in_b, conv_w, w_out_b, lbf, lbb,
                         s_f, s_b, True, True)
        x = _peer(x, mod_x[l], l, norm2_g, wq_b, keys_b, u_b, vt_b)
    return _final_norm_call(x.reshape(bsz * t, d), final_g[None, :]).reshape(bsz, t, d)
```

```python
import functools
import math

import numpy as np
import jax
import jax.numpy as jnp
from jax import lax
from jax.experimental import pallas as pl
from jax.experimental.pallas import tpu as pltpu

F32 = jnp.float32
BF16 = jnp.bfloat16

D_MODEL = 1024
GRID_W = 64
EPS = 1e-6
F_FLOOR = 1e-20
N_MOD = 6
HG_WIDTH = 512
HG_HEADS = 4
HG_DIM = HG_WIDTH // HG_HEADS
SC_WIDTH = 512
SC_HALF = SC_WIDTH // 2
IN_COLS = 5 * HG_WIDTH + 3 * SC_WIDTH
PEER_HEADS = 8
PEER_QDIM = 256
PEER_HALF = PEER_QDIM // 2
N_KEYS = 128
N_EXPERTS = N_KEYS * N_KEYS
PEER_TOPK = 16

SUBLANES = 8
LANES = 128
PACKED_ROWS = 2 * SUBLANES
VMEM_LIMIT = 48 * 1024 * 1024

GLA_CHUNK = 128
GLA_LEVELS = (64, 32, 16, 8, 4, 2, 1)
GLA_GROUP_HEADS = 4
COND_ROWS = 8

COL_IV, COL_ZF, COL_ZB, COL_Q, COL_G, COL_CG, COL_BG, COL_HV = range(8)
GATE_ZF, GATE_ZB = range(2)
FEAT_IV, FEAT_Q, FEAT_G, FEAT_CG, FEAT_BG, FEAT_HV = range(6)


def _params(*sem):
    return pltpu.CompilerParams(dimension_semantics=sem, vmem_limit_bytes=VMEM_LIMIT)


def _dot(a, b):
    return jnp.dot(a.astype(BF16), b.astype(BF16), preferred_element_type=F32)


def _dot_nt(a, b):
    return lax.dot_general(a.astype(BF16), b.astype(BF16), (((1,), (1,)), ((), ())),
                           preferred_element_type=F32)


def _dot_tn(a, b):
    return lax.dot_general(a.astype(BF16), b.astype(BF16), (((0,), (0,)), ((), ())),
                           preferred_element_type=F32)


def _mod_kernel(cond_ref, w_ref, b_ref, o_ref):
    c = cond_ref[...]
    s = c * jax.nn.sigmoid(c)
    o_ref[0] = _dot(s, w_ref[0]) + b_ref[0]


def _mod_call(cond, w_mod, b_mod):
    depth, d, n = w_mod.shape
    tn = 1536
    return pl.pallas_call(
        _mod_kernel,
        grid=(depth, n // tn),
        in_specs=[
            pl.BlockSpec((COND_ROWS, d), lambda l, j: (0, 0)),
            pl.BlockSpec((1, d, tn), lambda l, j: (l, 0, j)),
            pl.BlockSpec((1, 1, tn), lambda l, j: (l, 0, j)),
        ],
        out_specs=pl.BlockSpec((1, COND_ROWS, tn), lambda l, j: (l, 0, j)),
        out_shape=jax.ShapeDtypeStruct((depth, COND_ROWS, n), F32),
        compiler_params=_params("parallel", "parallel"),
        name="mod",
    )(cond, w_mod, b_mod.reshape(depth, 1, n))


def _norm_mod(x, g, scale, shift):
    ms = jnp.mean(x * x, axis=-1, keepdims=True)
    y = x * lax.rsqrt(ms + EPS) * g
    return y * (1.0 + scale) + shift


def _proj_kernel(x_ref, g_ref, sc_ref, sh_ref, w_ref, gates_ref, feats_ref):
    h = _norm_mod(x_ref[0], g_ref[...], sc_ref[0], sh_ref[0])
    y = _dot(h, w_ref[...])
    w = HG_WIDTH
    gates_ref[0] = y[:, COL_ZF * w:(COL_ZB + 1) * w]
    feats_ref[0, :, 0:w] = y[:, COL_IV * w:(COL_IV + 1) * w].astype(BF16)
    feats_ref[0, :, w:] = y[:, COL_Q * w:].astype(BF16)


def _proj_call(x, g, scale, shift, w, l):
    bsz, t, d = x.shape
    n = w.shape[2]
    n_gate = 2 * HG_WIDTH
    tm = min(256, t)
    return pl.pallas_call(
        _proj_kernel,
        grid=(bsz, t // tm),
        in_specs=[
            pl.BlockSpec((1, tm, d), lambda b, i: (b, i, 0)),
            pl.BlockSpec((1, d), lambda b, i: (0, 0)),
            pl.BlockSpec((1, 1, d), lambda b, i: (b, 0, 0)),
            pl.BlockSpec((1, 1, d), lambda b, i: (b, 0, 0)),
            pl.BlockSpec((None, d, n), lambda b, i: (l, 0, 0)),
        ],
        out_specs=[pl.BlockSpec((1, tm, n_gate), lambda b, i: (b, i, 0)),
                   pl.BlockSpec((1, tm, n - n_gate), lambda b, i: (b, i, 0))],
        out_shape=[jax.ShapeDtypeStruct((bsz, t, n_gate), F32),
                   jax.ShapeDtypeStruct((bsz, t, n - n_gate), BF16)],
        compiler_params=_params("parallel", "parallel"),
        name="proj",
    )(x, g, scale, shift, w)


def _gla_sum_matrix(backward):
    c = GLA_CHUNK
    t = np.arange(c)[:, None]
    s = np.arange(c)[None, :]
    mats = [s >= t if backward else s <= t]
    for n in GLA_LEVELS:
        if n >= SUBLANES:
            continue
        mid = (t // (2 * n)) * (2 * n) + n
        if not backward:
            m = np.where(t >= mid, (s >= mid) & (s <= t), (s > t) & (s < mid))
        else:
            m = np.where(t < mid, (s >= t) & (s < mid), (s >= mid) & (s < t))
        mats.append(m)
    return np.concatenate([m.astype(np.float32) for m in mats], axis=0)


def _gla_masks(backward):
    c = GLA_CHUNK
    row = lax.broadcasted_iota(jnp.int32, (c, c), 0)
    col = lax.broadcasted_iota(jnp.int32, (c, c), 1)
    qrow, lvl = [], []
    for n in GLA_LEVELS:
        r_hi = (row & (2 * n - 1)) >= n
        c_hi = (col & (2 * n - 1)) >= n
        same = (row & ~(2 * n - 1)) == (col & ~(2 * n - 1))
        if not backward:
            qrow.append(r_hi)
            lvl.append(same & r_hi & jnp.logical_not(c_hi))
        else:
            qrow.append(jnp.logical_not(r_hi))
            lvl.append(same & jnp.logical_not(r_hi) & c_hi)
    return qrow, lvl, row == col


def _gla_gates(z, lb, msum):
    sig = jax.nn.sigmoid(z)
    f = lb + (1.0 - lb) * sig
    logf = jnp.log(jnp.maximum(f, F_FLOOR))
    kk = (1.0 - lb) * jax.nn.sigmoid(-z)
    sums = jnp.dot(msum, logf.astype(BF16), preferred_element_type=F32)
    return kk, sums


def _boundary_rows(b, n, backward):
    c, w = b.shape
    parts = []
    for start in range(0, c, 2 * n):
        r = start + n if backward else start + n - 1
        parts.append(jnp.broadcast_to(b[r:r + 1, :], (2 * n, w)))
    return jnp.concatenate(parts, axis=0)


def _gla_chains(chains, ones):
    c = GLA_CHUNK
    atts = [jnp.where(masks[2], jnp.dot((q * kk).astype(BF16), ones, preferred_element_type=F32), 0.0)
            for q, v, kk, sums, st, masks, backward in chains]
    sub_tile = [n for n in GLA_LEVELS if n < SUBLANES]
    for li, n in enumerate(GLA_LEVELS):
        for ci, (q, v, kk, sums, st, masks, backward) in enumerate(chains):
            if n >= SUBLANES:
                exponent = -jnp.abs(sums[0:c] - _boundary_rows(sums[0:c], n, backward))
            else:
                k = 1 + sub_tile.index(n)
                exponent = sums[k * c:(k + 1) * c]
            mixed = (jnp.where(masks[0][li][:, :HG_DIM], q, kk) * jnp.exp(exponent)).astype(BF16)
            a = lax.dot_general(mixed, mixed, (((1,), (1,)), ((), ())), preferred_element_type=F32)
            atts[ci] = jnp.where(masks[1][li], a, atts[ci])
    out = []
    for att, (q, v, kk, sums, st, masks, backward) in zip(atts, chains):
        b = sums[0:c]
        b_end = b[0:1, :] if backward else b[c - 1:c, :]
        q_in = q * jnp.exp(b)
        o = _dot(att, v) + _dot_nt(q_in, st)
        k_out = kk * jnp.exp(b_end - b)
        out.append((o, jnp.exp(b_end) * st + _dot_tn(v, k_out)))
    return out


def _gla_kernel(ivf_ref, zf_ref, qf_ref, ivb_ref, zb_ref, qb_ref, lbf_ref, lbb_ref,
                s0f_ref, s0b_ref, mf_ref, mb_ref,
                of_ref, ob_ref, sf_ref, sb_ref, st_ref):
    n = pl.program_id(1)

    @pl.when(n == 0)
    def _():
        st_ref[0] = s0f_ref[0]
        st_ref[1] = s0b_ref[0]

    ones = jnp.ones((HG_DIM, HG_DIM), BF16)
    masks_f = _gla_masks(False)
    masks_b = _gla_masks(True)
    kk_f, sums_f = _gla_gates(zf_ref[0], lbf_ref[...], mf_ref[...])
    kk_b, sums_b = _gla_gates(zb_ref[0], lbb_ref[...], mb_ref[...])
    for h0 in range(0, HG_HEADS, GLA_GROUP_HEADS):
        heads = range(h0, h0 + GLA_GROUP_HEADS)
        sls = [slice(h * HG_DIM, (h + 1) * HG_DIM) for h in heads]
        chains = [(qf_ref[0, :, sl].astype(F32), ivf_ref[0, :, sl], kk_f[:, sl], sums_f[:, sl],
                   st_ref[0, h], masks_f, False) for h, sl in zip(heads, sls)]
        chains += [(qb_ref[0, :, sl].astype(F32), ivb_ref[0, :, sl], kk_b[:, sl], sums_b[:, sl],
                    st_ref[1, h], masks_b, True) for h, sl in zip(heads, sls)]
        res = _gla_chains(chains, ones)
        for i, (h, sl) in enumerate(zip(heads, sls)):
            of_ref[0, :, sl], st_ref[0, h] = res[i]
            ob_ref[0, :, sl], st_ref[1, h] = res[GLA_GROUP_HEADS + i]

    @pl.when(n == pl.num_programs(1) - 1)
    def _():
        sf_ref[0] = st_ref[0]
        sb_ref[0] = st_ref[1]


def _gla_call(gates, feats, lbf, lbb, s0f, s0b):
    bsz, t, _ = gates.shape
    c = GLA_CHUNK
    nc = t // c
    w = HG_WIDTH
    msum_f = jnp.asarray(_gla_sum_matrix(False), BF16)
    msum_b = jnp.asarray(_gla_sum_matrix(True), BF16)

    def fwd(col):
        return pl.BlockSpec((1, c, w), lambda b, n: (b, n, col))

    def bwd(col):
        return pl.BlockSpec((1, c, w), lambda b, n: (b, nc - 1 - n, col))

    state_spec = pl.BlockSpec((1, HG_HEADS, HG_DIM, HG_DIM), lambda b, n: (b, 0, 0, 0))
    const = lambda shape: pl.BlockSpec(shape, lambda b, n: (0,) * len(shape))
    return pl.pallas_call(
        _gla_kernel,
        grid=(bsz, nc),
        in_specs=[fwd(FEAT_IV), fwd(GATE_ZF), fwd(FEAT_Q), bwd(FEAT_IV), bwd(GATE_ZB), bwd(FEAT_Q),
                  const((1, w)), const((1, w)), state_spec, state_spec,
                  const(msum_f.shape), const(msum_b.shape)],
        out_specs=[pl.BlockSpec((1, c, w), lambda b, n: (b, n, 0)),
                   pl.BlockSpec((1, c, w), lambda b, n: (b, nc - 1 - n, 0)),
                   state_spec, state_spec],
        out_shape=[jax.ShapeDtypeStruct((bsz, t, w), F32),
                   jax.ShapeDtypeStruct((bsz, t, w), F32),
                   jax.ShapeDtypeStruct((bsz, HG_HEADS, HG_DIM, HG_DIM), F32),
                   jax.ShapeDtypeStruct((bsz, HG_HEADS, HG_DIM, HG_DIM), F32)],
        scratch_shapes=[pltpu.VMEM((2, HG_HEADS, HG_DIM, HG_DIM), F32)],
        compiler_params=_params("parallel", "arbitrary"),
        name="gla",
    )(feats, gates, feats, feats, gates, feats, lbf, lbb, s0f, s0b, msum_f, msum_b)


def _mixout_body(of_ref, ob_ref, g_ref, cg_ref, bg_ref, hv_ref, halo, cw_ref, wo_ref,
                 x_ref, g1_ref, o_ref, grid_conv):
    tm = of_ref.shape[1]
    o = of_ref[0] + ob_ref[0]
    heads = []
    for h in range(HG_HEADS):
        oh = o[:, h * HG_DIM:(h + 1) * HG_DIM]
        heads.append(oh * lax.rsqrt(jnp.mean(oh * oh, axis=-1, keepdims=True) + EPS))
    g = g_ref[0].astype(F32)
    o_rec = jnp.concatenate(heads, axis=-1) * (g * jax.nn.sigmoid(g))

    u = cg_ref[0].astype(F32) * hv_ref[0].astype(F32)
    w0, w1, w2 = cw_ref[0:1, :], cw_ref[1:2, :], cw_ref[2:3, :]
    pos = lax.broadcasted_iota(jnp.int32, (tm, 1), 0)
    if grid_conv:
        cgp_ref, hvp_ref, cgn_ref, hvn_ref = halo
        i = pl.program_id(1)
        last = pl.num_programs(1) - 1
        colpos = pos & (GRID_W - 1)
        uh = u[:, :SC_HALF]
        left = jnp.where(colpos == 0, 0.0, pltpu.roll(uh, 1, 0))
        right = jnp.where(colpos == GRID_W - 1, 0.0, pltpu.roll(uh, tm - 1, 0))
        conv_h = w0[:, :SC_HALF] * left + w1[:, :SC_HALF] * uh + w2[:, :SC_HALF] * right
        uv = u[:, SC_HALF:]
        halo_u = lambda cg, hv: cg[0][:, SC_HALF:].astype(F32) * hv[0][:, SC_HALF:].astype(F32)
        up_halo = jnp.where(i == 0, 0.0, halo_u(cgp_ref, hvp_ref))
        dn_halo = jnp.where(i == last, 0.0, halo_u(cgn_ref, hvn_ref))
        up = jnp.concatenate([up_halo, uv[:tm - GRID_W]], axis=0)
        down = jnp.concatenate([uv[GRID_W:], dn_halo], axis=0)
        conv_v = w0[:, SC_HALF:] * up + w1[:, SC_HALF:] * uv + w2[:, SC_HALF:] * down
        conv = jnp.concatenate([conv_h, conv_v], axis=-1)
    else:
        left = jnp.where(pos == 0, 0.0, pltpu.roll(u, 1, 0))
        right = jnp.where(pos == tm - 1, 0.0, pltpu.roll(u, tm - 1, 0))
        conv = w0 * left + w1 * u + w2 * right
    o_conv = bg_ref[0].astype(F32) * conv
    y = _dot(o_rec, wo_ref[0:HG_WIDTH, :]) + _dot(o_conv, wo_ref[HG_WIDTH:, :])
    o_ref[0] = x_ref[0] + g1_ref[0] * y


def _mixout_grid_kernel(of_ref, ob_ref, g_ref, cg_ref, bg_ref, hv_ref, cgp_ref, hvp_ref,
                        cgn_ref, hvn_ref, cw_ref, wo_ref, x_ref, g1_ref, o_ref):
    _mixout_body(of_ref, ob_ref, g_ref, cg_ref, bg_ref, hv_ref,
                 (cgp_ref, hvp_ref, cgn_ref, hvn_ref), cw_ref, wo_ref, x_ref, g1_ref, o_ref, True)


def _mixout_seq_kernel(of_ref, ob_ref, g_ref, cg_ref, bg_ref, hv_ref, cw_ref, wo_ref,
                       x_ref, g1_ref, o_ref):
    _mixout_body(of_ref, ob_ref, g_ref, cg_ref, bg_ref, hv_ref, None, cw_ref, wo_ref,
                 x_ref, g1_ref, o_ref, False)


def _mixout_call(o_f, o_b, proj, conv_w, w_out, x, g1, grid_conv):
    bsz, t, d = x.shape
    w = HG_WIDTH
    tm = 512 if grid_conv else t
    nt = t // tm
    hb = tm // GRID_W
    nhalo = t // GRID_W

    def col(cidx):
        return pl.BlockSpec((1, tm, w), lambda b, i: (b, i, cidx))

    def prev(cidx):
        return pl.BlockSpec((1, GRID_W, w), lambda b, i: (b, jnp.maximum(i * hb - 1, 0), cidx))

    def nxt(cidx):
        return pl.BlockSpec((1, GRID_W, w),
                            lambda b, i: (b, jnp.minimum((i + 1) * hb, nhalo - 1), cidx))

    row = pl.BlockSpec((1, tm, w), lambda b, i: (b, i, 0))
    in_specs = [row, row, col(FEAT_G), col(FEAT_CG), col(FEAT_BG), col(FEAT_HV)]
    args = [o_f, o_b, proj, proj, proj, proj]
    if grid_conv:
        in_specs += [prev(FEAT_CG), prev(FEAT_HV), nxt(FEAT_CG), nxt(FEAT_HV)]
        args += [proj, proj, proj, proj]
    in_specs += [
        pl.BlockSpec((3, w), lambda b, i: (0, 0)),
        pl.BlockSpec(w_out.shape, lambda b, i: (0, 0)),
        pl.BlockSpec((1, tm, d), lambda b, i: (b, i, 0)),
        pl.BlockSpec((1, 1, d), lambda b, i: (b, 0, 0)),
    ]
    args += [conv_w, w_out, x, g1]
    return pl.pallas_call(
        _mixout_grid_kernel if grid_conv else _mixout_seq_kernel,
        grid=(bsz, nt),
        in_specs=in_specs,
        out_specs=pl.BlockSpec((1, tm, d), lambda b, i: (b, i, 0)),
        out_shape=jax.ShapeDtypeStruct((bsz, t, d), F32),
        compiler_params=_params("parallel", "parallel"),
        name="mixout_grid" if grid_conv else "mixout_seq",
    )(*args)


def _oddeven_merge_sort_pairs(n):
    pairs = []
    p = 1
    while p < n:
        k = p
        while k >= 1:
            for j in range(k % p, n - k, 2 * k):
                for i in range(min(k, n - j - k)):
                    if (i + j) // (2 * p) == (i + j + k) // (2 * p):
                        pairs.append((i + j, i + j + k))
            k //= 2
        p *= 2
    return pairs


_SORT16 = _oddeven_merge_sort_pairs(PEER_TOPK)


def _sort_desc(xs):
    xs = list(xs)
    for i, j in _SORT16:
        a, b = xs[i], xs[j]
        xs[i], xs[j] = jnp.maximum(a, b), jnp.minimum(a, b)
    return xs


def _bitonic_merge_desc(xs):
    xs = list(xs)
    d = len(xs) // 2
    while d >= 1:
        for i in range(len(xs)):
            if i & d == 0:
                a, b = xs[i], xs[i + d]
                xs[i], xs[i + d] = jnp.maximum(a, b), jnp.minimum(a, b)
        d //= 2
    return xs


def _top_of_union(a, b):
    k = len(a)
    return [jnp.maximum(a[r], b[k - 1 - r]) for r in range(k)]


def _topk_rows(s):
    rows = [s[SUBLANES * v:SUBLANES * (v + 1), :] for v in range(N_KEYS // SUBLANES)]
    rows = _sort_desc(rows)
    for shift in (4, 2, 1):
        rolled = [pltpu.roll(r, shift, 0) for r in rows]
        rows = _bitonic_merge_desc(_top_of_union(rows, rolled))
    return rows


def _peer_prep_kernel(x_ref, g_ref, sc_ref, sh_ref, wq_ref, keys_ref,
                      ht_ref, n1_ref, e1_ref, r2_ref, e2_ref, s1_ref, s2_ref):
    tq = x_ref.shape[1]
    hf = _norm_mod(x_ref[0], g_ref[...], sc_ref[0], sh_ref[0])
    ht_ref[...] = hf.T.astype(BF16)
    qf = jnp.dot(hf.astype(BF16), wq_ref[...], preferred_element_type=F32)
    sub = lax.broadcasted_iota(jnp.int32, (SUBLANES, tq), 0)
    neg = jnp.full((SUBLANES, tq), -jnp.inf, F32)
    packed = [[jnp.zeros((SUBLANES, tq), F32)] * PEER_TOPK for _ in range(2)]
    for hd in range(PEER_HEADS):
        for p in range(2):
            lo = hd * PEER_QDIM + p * PEER_HALF
            s = _dot_nt(keys_ref[hd, p], qf[:, lo:lo + PEER_HALF])
            (s1_ref if p == 0 else s2_ref)[hd] = s
            top = _topk_rows(s)
            packed[p] = [jnp.where(sub == hd, top[r], packed[p][r]) for r in range(PEER_TOPK)]
    c1, c2 = packed
    pairs = [(a, b) for a in range(PEER_TOPK) for b in range(PEER_TOPK)
             if (a + 1) * (b + 1) <= PEER_TOPK]
    cand = {ab: c1[ab[0]] + c2[ab[1]] for ab in pairs}
    cands = [cand[ab] for ab in pairs]
    cands += [neg] * (-len(cands) % PEER_TOPK)
    groups = [_sort_desc(cands[i:i + PEER_TOPK]) for i in range(0, len(cands), PEER_TOPK)]
    while len(groups) > 2:
        nxt = [_bitonic_merge_desc(_top_of_union(groups[i], groups[i + 1]))
               for i in range(0, len(groups) - 1, 2)]
        if len(groups) % 2:
            nxt.append(groups[-1])
        groups = nxt
    top = _top_of_union(groups[0], groups[1]) if len(groups) == 2 else groups[0]
    tau = functools.reduce(jnp.minimum, top)
    m1, m2 = c1[0], c2[0]
    mx = m1 + m2
    z = functools.reduce(lambda a, b: a + b, [jnp.exp(t - mx) for t in top])
    inv_z = 1.0 / z
    counts = []
    for a in range(PEER_TOPK):
        n = jnp.zeros((SUBLANES, tq), F32)
        for b in range(PEER_TOPK):
            if (a, b) in cand:
                n = n + jnp.where(cand[(a, b)] >= tau, 1.0, 0.0)
        counts.append(n)
    half = PEER_TOPK // 2
    theta = []
    for m in range(1, half + 1):
        th = jnp.full((SUBLANES, tq), jnp.inf, F32)
        for a in range(PEER_TOPK // m):
            th = jnp.where(counts[a] >= m, c1[a], th)
        theta.append(th)
    grouped = (N_KEYS // SUBLANES, SUBLANES, tq)
    for hd in range(PEER_HEADS):
        s1 = s1_ref[hd]
        s2 = s2_ref[hd]
        n1 = jnp.zeros((N_KEYS, tq), F32)
        for m in range(1, half + 1):
            n1 = jnp.where(s1 >= theta[m - 1][hd:hd + 1, :], float(m), n1)
        n1 = jnp.where(s1 >= c1[0][hd:hd + 1, :], counts[0][hd:hd + 1, :], n1)
        rank2 = jnp.full((N_KEYS, tq), float(PEER_TOPK), F32)
        for r in reversed(range(PEER_TOPK)):
            rank2 = jnp.where(s2 == c2[r][hd:hd + 1, :], float(r), rank2)
        n1_ref[hd] = n1.reshape(grouped)
        r2_ref[hd] = pltpu.bitcast(rank2.astype(BF16), jnp.uint32)
        e1 = jnp.exp(s1 - m1[hd:hd + 1, :]) * inv_z[hd:hd + 1, :]
        e1_ref[hd] = e1.reshape(grouped)
        e2_ref[hd] = pltpu.bitcast(jnp.exp(s2 - m2[hd:hd + 1, :]).astype(BF16), jnp.uint32)


def _peer_prep_call(x, g, scale, shift, wq, keys, l):
    bsz, t, d = x.shape
    tq = min(256, t)
    nt = t // tq
    ttot = bsz * t
    score_spec = pl.BlockSpec((PEER_HEADS, N_KEYS // 2, tq), lambda b, i: (0, 0, b * nt + i))
    score_shape = jax.ShapeDtypeStruct((PEER_HEADS, N_KEYS // 2, ttot), jnp.uint32)
    n_grp = N_KEYS // SUBLANES
    grouped_spec = pl.BlockSpec((PEER_HEADS, n_grp, SUBLANES, tq), lambda b, i: (0, 0, 0, b * nt + i))
    grouped_shape = jax.ShapeDtypeStruct((PEER_HEADS, n_grp, SUBLANES, ttot), F32)
    return pl.pallas_call(
        _peer_prep_kernel,
        grid=(bsz, nt),
        in_specs=[
            pl.BlockSpec((1, tq, d), lambda b, i: (b, i, 0)),
            pl.BlockSpec((1, d), lambda b, i: (0, 0)),
            pl.BlockSpec((1, 1, d), lambda b, i: (b, 0, 0)),
            pl.BlockSpec((1, 1, d), lambda b, i: (b, 0, 0)),
            pl.BlockSpec((None,) + wq.shape[1:], lambda b, i: (l, 0, 0)),
            pl.BlockSpec((None,) + keys.shape[1:], lambda b, i: (l, 0, 0, 0, 0)),
        ],
        out_specs=[pl.BlockSpec((d, tq), lambda b, i: (0, b * nt + i)),
                   grouped_spec, grouped_spec, score_spec, score_spec],
        out_shape=[jax.ShapeDtypeStruct((d, ttot), BF16),
                   grouped_shape, grouped_shape, score_shape, score_shape],
        scratch_shapes=[pltpu.VMEM((PEER_HEADS, N_KEYS, tq), F32),
                        pltpu.VMEM((PEER_HEADS, N_KEYS, tq), F32)],
        compiler_params=_params("parallel", "parallel"),
        name="peer_prep",
    )(x, g, scale, shift, wq, keys)


PEER_TE = SUBLANES * N_KEYS
SQRT_HALF = math.sqrt(0.5)


def _peer_dense_kernel(ht_ref, u_ref, vt_ref, n1_ref, e1_ref, r2_ref, e2_ref,
                       x_ref, g2_ref, o_ref, acc_ref, a_ref, gt_ref, *, n_e, n_steps):
    s = pl.program_id(0)
    tt = ht_ref.shape[1]
    n_lane_tiles = tt // LANES
    n_blk = N_KEYS // PACKED_ROWS

    @pl.when(s == 0)
    def _():
        a_ref[...] = jnp.zeros_like(a_ref)
        gt_ref[...] = jnp.zeros_like(gt_ref)

    @pl.when(jnp.logical_or(s == 0, (s - 2) % n_e == 0))
    def _():
        acc_ref[...] = jnp.zeros_like(acc_ref)

    acc_ref[...] += jnp.dot(vt_ref[0], gt_ref[...], preferred_element_type=F32)

    grp = jnp.clip(s - 1, 0, n_steps - 1) % n_e
    packed = (PACKED_ROWS, LANES)
    for il in range(SUBLANES):
        for j in range(n_lane_tiles):
            lanes = pl.ds(j * LANES, LANES)
            acc = [None] * n_blk
            for hd in range(PEER_HEADS):
                n1b = jnp.broadcast_to(n1_ref[hd, grp, pl.ds(il, 1), lanes], packed).astype(BF16)
                e1b = jnp.broadcast_to(e1_ref[hd, grp, pl.ds(il, 1), lanes], packed).astype(BF16)
                for k in range(n_blk):
                    rows = pl.ds(k * SUBLANES, SUBLANES)
                    rank2 = pltpu.bitcast(r2_ref[hd, rows, lanes], BF16)
                    e2 = pltpu.bitcast(e2_ref[hd, rows, lanes], BF16)
                    w = jnp.where(rank2 < n1b, e2, 0.0) * e1b
                    acc[k] = w if hd == 0 else acc[k] + w
            for k in range(n_blk):
                rows = pl.ds(il * N_KEYS + k * PACKED_ROWS, PACKED_ROWS)
                a = a_ref[rows, lanes].astype(BF16)
                gelu = (0.5 * a) * (1.0 + lax.erf(a * SQRT_HALF))
                gt_ref[rows, lanes] = acc[k] * gelu

    a_ref[...] = jnp.dot(u_ref[...], ht_ref[...], preferred_element_type=F32)

    @pl.when(jnp.logical_and(s >= 2, (s - 2) % n_e == n_e - 1))
    def _():
        o_ref[...] = x_ref[...] + g2_ref[0] * acc_ref[...].T


def _peer_dense_call(ht, u, vt, n1, e1, r2, e2, x, g2, t_per_batch, l):
    ttot, d = x.shape
    tt = min(512, t_per_batch)
    n_e = N_EXPERTS // PEER_TE
    n_steps = (ttot // tt) * n_e
    per_batch = t_per_batch // tt
    tile = lambda s, lag: jnp.clip(s - lag, 0, n_steps - 1) // n_e
    group = lambda s, lag: jnp.clip(s - lag, 0, n_steps - 1) % n_e
    score_spec = pl.BlockSpec((PEER_HEADS, N_KEYS // 2, tt), lambda s: (0, 0, tile(s, 1)))
    grouped_spec = pl.BlockSpec((PEER_HEADS, N_KEYS // SUBLANES, SUBLANES, tt),
                                lambda s: (0, 0, 0, tile(s, 1)))
    return pl.pallas_call(
        functools.partial(_peer_dense_kernel, n_e=n_e, n_steps=n_steps),
        grid=(n_steps + 2,),
        in_specs=[
            pl.BlockSpec((d, tt), lambda s: (0, tile(s, 0))),
            pl.BlockSpec((None, PEER_TE, d), lambda s: (l, group(s, 0), 0)),
            pl.BlockSpec((None, 1, d, PEER_TE), lambda s: (l, group(s, 2), 0, 0)),
            grouped_spec, grouped_spec, score_spec, score_spec,
            pl.BlockSpec((tt, d), lambda s: (tile(s, 2), 0)),
            pl.BlockSpec((1, 1, d), lambda s: (tile(s, 2) // per_batch, 0, 0)),
        ],
        out_specs=pl.BlockSpec((tt, d), lambda s: (tile(s, 2), 0)),
        out_shape=jax.ShapeDtypeStruct((ttot, d), F32),
        scratch_shapes=[pltpu.VMEM((d, tt), F32),
                        pltpu.VMEM((PEER_TE, tt), F32),
                        pltpu.VMEM((PEER_TE, tt), BF16)],
        compiler_params=_params("arbitrary"),
        name="peer_dense",
    )(ht, u, vt, n1, e1, r2, e2, x, g2)


def _final_norm_kernel(x_ref, g_ref, o_ref):
    x = x_ref[...]
    ms = jnp.mean(x * x, axis=-1, keepdims=True)
    o_ref[...] = x * lax.rsqrt(ms + EPS) * g_ref[...]


def _final_norm_call(x, g):
    n, d = x.shape
    tm = 1024
    return pl.pallas_call(
        _final_norm_kernel,
        grid=(n // tm,),
        in_specs=[pl.BlockSpec((tm, d), lambda i: (i, 0)), pl.BlockSpec((1, d), lambda i: (0, 0))],
        out_specs=pl.BlockSpec((tm, d), lambda i: (i, 0)),
        out_shape=jax.ShapeDtypeStruct((n, d), F32),
        compiler_params=_params("parallel"),
        name="final_norm",
    )(x, g)


def _mixer(x, mod, l, norm1_g, w_in, conv_w, w_out, lbf, lbb, s0f, s0b, grid_conv, full):
    sh1, sc1, g1 = mod[0], mod[1], mod[2]
    gates, feats = _proj_call(x, norm1_g[l][None, :], sc1, sh1, w_in, l)
    o_f, o_b, s_f, s_b = _gla_call(gates, feats, lbf, lbb, s0f, s0b)
    if not full:
        return None, s_f, s_b
    x = _mixout_call(o_f, o_b, feats, conv_w[l], w_out[l], x, g1, grid_conv)
    return x, s_f, s_b


def _peer(x, mod, l, norm2_g, wq, keys, u, vt):
    bsz, t, d = x.shape
    sh2, sc2, g2 = mod[3], mod[4], mod[5]
    ht, n1, e1, r2, e2 = _peer_prep_call(x, norm2_g[l][None, :], sc2, sh2, wq, keys, l)
    out = _peer_dense_call(ht, u, vt, n1, e1, r2, e2, x.reshape(bsz * t, d), g2, t, l)
    return out.reshape(bsz, t, d)


def kernel(x, c, ctx, c_ctx, w_mod, b_mod, norm1_g, norm2_g, w_in, conv_w, w_out, lb_logits,
           peer_wq, peer_subkeys, peer_u, peer_v, final_g):
    bsz, t, d = x.shape
    depth = w_mod.shape[0]

    p_lb = jax.nn.softmax(lb_logits.astype(F32), axis=0)
    lower = jnp.cumsum(p_lb, axis=0) - p_lb[0]

    cond = jnp.zeros((COND_ROWS, d), F32).at[:bsz].set(c).at[bsz].set(c_ctx)
    mod = _mod_call(cond, w_mod, b_mod)
    mod = mod.reshape(depth, COND_ROWS, N_MOD, d)
    mod_x = jnp.transpose(mod[:, :bsz], (0, 2, 1, 3))[:, :, :, None, :]
    mod_c = jnp.broadcast_to(mod[:, bsz][:, :, None, None, :], mod_x.shape)

    w_in_b = w_in.astype(BF16)
    w_out_b = w_out.astype(BF16)
    wq_b = peer_wq.astype(BF16)
    keys_b = peer_subkeys.astype(BF16)
    u_b = peer_u.astype(BF16)
    vt_b = jnp.swapaxes(peer_v.astype(BF16).reshape(depth, N_EXPERTS // PEER_TE, PEER_TE, d), 2, 3)

    zero_state = jnp.zeros((bsz, HG_HEADS, HG_DIM, HG_DIM), F32)
    xc = ctx
    for l in range(depth):
        lbf = lower[l, 0][None, :]
        lbb = lower[l, 1][None, :]
        full = l < depth - 1
        xc_new, s_f, s_b = _mixer(xc, mod_c[l], l, norm1_g, w_in_b, conv_w, w_out_b, lbf, lbb,
                                  zero_state, zero_state, False, full)
        if full:
            xc = _peer(xc_new, mod_c[l], l, norm2_g, wq_b, keys_b, u_b, vt_b)
        x, _, _ = _mixer(x, mod_x[l], l, norm1_g, w_in_b, conv_w, w_out_b, lbf, lbb,
                         s_f, s_b, True, True)
        x = _peer(x, mod_x[l], l, norm2_g, wq_b, keys_b, u_b, vt_b)
    return _final_norm_call(x.reshape(bsz * t, d), final_g[None, :]).reshape(bsz, t, d)
```

```python
import functools
import math

import numpy as np
import jax
import jax.numpy as jnp
from jax import lax
from jax.experimental import pallas as pl
from jax.experimental.pallas import tpu as pltpu

F32 = jnp.float32
BF16 = jnp.bfloat16

D_MODEL = 1024
GRID_W = 64
EPS = 1e-6
F_FLOOR = 1e-20
N_MOD = 6
HG_WIDTH = 512
HG_HEADS = 4
HG_DIM = HG_WIDTH // HG_HEADS
SC_WIDTH = 512
SC_HALF = SC_WIDTH // 2
IN_COLS = 5 * HG_WIDTH + 3 * SC_WIDTH
PEER_HEADS = 8
PEER_QDIM = 256
PEER_HALF = PEER_QDIM // 2
N_KEYS = 128
N_EXPERTS = N_KEYS * N_KEYS
PEER_TOPK = 16

SUBLANES = 8
LANES = 128
PACKED_ROWS = 2 * SUBLANES
VMEM_LIMIT = 48 * 1024 * 1024

GLA_CHUNK = 128
GLA_LEVELS = (64, 32, 16, 8, 4, 2, 1)
GLA_GROUP_HEADS = 4
COND_ROWS = 8

COL_IV, COL_ZF, COL_ZB, COL_Q, COL_G, COL_CG, COL_BG, COL_HV = range(8)
GATE_ZF, GATE_ZB = range(2)
FEAT_IV, FEAT_Q, FEAT_G, FEAT_CG, FEAT_BG, FEAT_HV = range(6)


def _params(*sem):
    return pltpu.CompilerParams(dimension_semantics=sem, vmem_limit_bytes=VMEM_LIMIT)


def _dot(a, b):
    return jnp.dot(a.astype(BF16), b.astype(BF16), preferred_element_type=F32)


def _dot_nt(a, b):
    return lax.dot_general(a.astype(BF16), b.astype(BF16), (((1,), (1,)), ((), ())),
                           preferred_element_type=F32)


def _dot_tn(a, b):
    return lax.dot_general(a.astype(BF16), b.astype(BF16), (((0,), (0,)), ((), ())),
                           preferred_element_type=F32)


def _mod_kernel(cond_ref, w_ref, b_ref, o_ref):
    c = cond_ref[...]
    s = c * jax.nn.sigmoid(c)
    o_ref[0] = _dot(s, w_ref[0]) + b_ref[0]


def _mod_call(cond, w_mod, b_mod):
    depth, d, n = w_mod.shape
    tn = n // 2
    return pl.pallas_call(
        _mod_kernel,
        grid=(depth, n // tn),
        in_specs=[
            pl.BlockSpec((COND_ROWS, d), lambda l, j: (0, 0)),
            pl.BlockSpec((1, d, tn), lambda l, j: (l, 0, j)),
            pl.BlockSpec((1, 1, tn), lambda l, j: (l, 0, j)),
        ],
        out_specs=pl.BlockSpec((1, COND_ROWS, tn), lambda l, j: (l, 0, j)),
        out_shape=jax.ShapeDtypeStruct((depth, COND_ROWS, n), F32),
        compiler_params=_params("parallel", "parallel"),
        name="mod",
    )(cond, w_mod, b_mod.reshape(depth, 1, n))


def _norm_mod(x, g, scale, shift):
    ms = jnp.mean(x * x, axis=-1, keepdims=True)
    y = x * lax.rsqrt(ms + EPS) * g
    return y * (1.0 + scale) + shift


def _proj_kernel(x_ref, g_ref, sc_ref, sh_ref, w_ref, gates_ref, feats_ref):
    h = _norm_mod(x_ref[0], g_ref[...], sc_ref[0], sh_ref[0])
    y = _dot(h, w_ref[...])
    w = HG_WIDTH
    gates_ref[0] = y[:, COL_ZF * w:(COL_ZB + 1) * w]
    feats_ref[0, :, 0:w] = y[:, COL_IV * w:(COL_IV + 1) * w].astype(BF16)
    feats_ref[0, :, w:] = y[:, COL_Q * w:].astype(BF16)


def _proj_call(x, g, scale, shift, w, l):
    bsz, t, d = x.shape
    n = w.shape[2]
    n_gate = 2 * HG_WIDTH
    tm = min(256, t)
    return pl.pallas_call(
        _proj_kernel,
        grid=(bsz, t // tm),
        in_specs=[
            pl.BlockSpec((1, tm, d), lambda b, i: (b, i, 0)),
            pl.BlockSpec((1, d), lambda b, i: (0, 0)),
            pl.BlockSpec((1, 1, d), lambda b, i: (b, 0, 0)),
            pl.BlockSpec((1, 1, d), lambda b, i: (b, 0, 0)),
            pl.BlockSpec((None, d, n), lambda b, i: (l, 0, 0)),
        ],
        out_specs=[pl.BlockSpec((1, tm, n_gate), lambda b, i: (b, i, 0)),
                   pl.BlockSpec((1, tm, n - n_gate), lambda b, i: (b, i, 0))],
        out_shape=[jax.ShapeDtypeStruct((bsz, t, n_gate), F32),
                   jax.ShapeDtypeStruct((bsz, t, n - n_gate), BF16)],
        compiler_params=_params("parallel", "parallel"),
        name="proj",
    )(x, g, scale, shift, w)


def _gla_sum_matrix(backward):
    c = GLA_CHUNK
    t = np.arange(c)[:, None]
    s = np.arange(c)[None, :]
    mats = [s >= t if backward else s <= t]
    for n in GLA_LEVELS:
        if n >= SUBLANES:
            continue
        mid = (t // (2 * n)) * (2 * n) + n
        if not backward:
            m = np.where(t >= mid, (s >= mid) & (s <= t), (s > t) & (s < mid))
        else:
            m = np.where(t < mid, (s >= t) & (s < mid), (s >= mid) & (s < t))
        mats.append(m)
    return np.concatenate([m.astype(np.float32) for m in mats], axis=0)


def _gla_masks(backward):
    c = GLA_CHUNK
    row = lax.broadcasted_iota(jnp.int32, (c, c), 0)
    col = lax.broadcasted_iota(jnp.int32, (c, c), 1)
    qrow, lvl = [], []
    for n in GLA_LEVELS:
        r_hi = (row & (2 * n - 1)) >= n
        c_hi = (col & (2 * n - 1)) >= n
        same = (row & ~(2 * n - 1)) == (col & ~(2 * n - 1))
        if not backward:
            qrow.append(r_hi)
            lvl.append(same & r_hi & jnp.logical_not(c_hi))
        else:
            qrow.append(jnp.logical_not(r_hi))
            lvl.append(same & jnp.logical_not(r_hi) & c_hi)
    return qrow, lvl, row == col


def _gla_gates(z, lb, msum):
    sig = jax.nn.sigmoid(z)
    f = lb + (1.0 - lb) * sig
    logf = jnp.log(jnp.maximum(f, F_FLOOR))
    kk = (1.0 - lb) * jax.nn.sigmoid(-z)
    sums = jnp.dot(msum, logf.astype(BF16), preferred_element_type=F32)
    return kk, sums


def _boundary_rows(b, n, backward):
    c, w = b.shape
    parts = []
    for start in range(0, c, 2 * n):
        r = start + n if backward else start + n - 1
        parts.append(jnp.broadcast_to(b[r:r + 1, :], (2 * n, w)))
    return jnp.concatenate(parts, axis=0)


def _gla_chains(chains, ones):
    c = GLA_CHUNK
    atts = [jnp.where(masks[2], jnp.dot((q * kk).astype(BF16), ones, preferred_element_type=F32), 0.0)
            for q, v, kk, sums, st, masks, backward in chains]
    sub_tile = [n for n in GLA_LEVELS if n < SUBLANES]
    for li, n in enumerate(GLA_LEVELS):
        for ci, (q, v, kk, sums, st, masks, backward) in enumerate(chains):
            if n >= SUBLANES:
                exponent = -jnp.abs(sums[0:c] - _boundary_rows(sums[0:c], n, backward))
            else:
                k = 1 + sub_tile.index(n)
                exponent = sums[k * c:(k + 1) * c]
            mixed = (jnp.where(masks[0][li][:, :HG_DIM], q, kk) * jnp.exp(exponent)).astype(BF16)
            a = lax.dot_general(mixed, mixed, (((1,), (1,)), ((), ())), preferred_element_type=F32)
            atts[ci] = jnp.where(masks[1][li], a, atts[ci])
    out = []
    for att, (q, v, kk, sums, st, masks, backward) in zip(atts, chains):
        b = sums[0:c]
        b_end = b[0:1, :] if backward else b[c - 1:c, :]
        q_in = q * jnp.exp(b)
        o = _dot(att, v) + _dot_nt(q_in, st)
        k_out = kk * jnp.exp(b_end - b)
        out.append((o, jnp.exp(b_end) * st + _dot_tn(v, k_out)))
    return out


def _gla_kernel(ivf_ref, zf_ref, qf_ref, ivb_ref, zb_ref, qb_ref, lbf_ref, lbb_ref,
                s0f_ref, s0b_ref, mf_ref, mb_ref,
                of_ref, ob_ref, sf_ref, sb_ref, st_ref):
    n = pl.program_id(1)

    @pl.when(n == 0)
    def _():
        st_ref[0] = s0f_ref[0]
        st_ref[1] = s0b_ref[0]

    ones = jnp.ones((HG_DIM, HG_DIM), BF16)
    masks_f = _gla_masks(False)
    masks_b = _gla_masks(True)
    kk_f, sums_f = _gla_gates(zf_ref[0], lbf_ref[...], mf_ref[...])
    kk_b, sums_b = _gla_gates(zb_ref[0], lbb_ref[...], mb_ref[...])
    for h0 in range(0, HG_HEADS, GLA_GROUP_HEADS):
        heads = range(h0, h0 + GLA_GROUP_HEADS)
        sls = [slice(h * HG_DIM, (h + 1) * HG_DIM) for h in heads]
        chains = [(qf_ref[0, :, sl].astype(F32), ivf_ref[0, :, sl], kk_f[:, sl], sums_f[:, sl],
                   st_ref[0, h], masks_f, False) for h, sl in zip(heads, sls)]
        chains += [(qb_ref[0, :, sl].astype(F32), ivb_ref[0, :, sl], kk_b[:, sl], sums_b[:, sl],
                    st_ref[1, h], masks_b, True) for h, sl in zip(heads, sls)]
        res = _gla_chains(chains, ones)
        for i, (h, sl) in enumerate(zip(heads, sls)):
            of_ref[0, :, sl], st_ref[0, h] = res[i]
            ob_ref[0, :, sl], st_ref[1, h] = res[GLA_GROUP_HEADS + i]

    @pl.when(n == pl.num_programs(1) - 1)
    def _():
        sf_ref[0] = st_ref[0]
        sb_ref[0] = st_ref[1]


def _gla_call(gates, feats, lbf, lbb, s0f, s0b):
    bsz, t, _ = gates.shape
    c = GLA_CHUNK
    nc = t // c
    w = HG_WIDTH
    msum_f = jnp.asarray(_gla_sum_matrix(False), BF16)
    msum_b = jnp.asarray(_gla_sum_matrix(True), BF16)

    def fwd(col):
        return pl.BlockSpec((1, c, w), lambda b, n: (b, n, col))

    def bwd(col):
        return pl.BlockSpec((1, c, w), lambda b, n: (b, nc - 1 - n, col))

    state_spec = pl.BlockSpec((1, HG_HEADS, HG_DIM, HG_DIM), lambda b, n: (b, 0, 0, 0))
    const = lambda shape: pl.BlockSpec(shape, lambda b, n: (0,) * len(shape))
    return pl.pallas_call(
        _gla_kernel,
        grid=(bsz, nc),
        in_specs=[fwd(FEAT_IV), fwd(GATE_ZF), fwd(FEAT_Q), bwd(FEAT_IV), bwd(GATE_ZB), bwd(FEAT_Q),
                  const((1, w)), const((1, w)), state_spec, state_spec,
                  const(msum_f.shape), const(msum_b.shape)],
        out_specs=[pl.BlockSpec((1, c, w), lambda b, n: (b, n, 0)),
                   pl.BlockSpec((1, c, w), lambda b, n: (b, nc - 1 - n, 0)),
                   state_spec, state_spec],
        out_shape=[jax.ShapeDtypeStruct((bsz, t, w), F32),
                   jax.ShapeDtypeStruct((bsz, t, w), F32),
                   jax.ShapeDtypeStruct((bsz, HG_HEADS, HG_DIM, HG_DIM), F32),
                   jax.ShapeDtypeStruct((bsz, HG_HEADS, HG_DIM, HG_DIM), F32)],
        scratch_shapes=[pltpu.VMEM((2, HG_HEADS, HG_DIM, HG_DIM), F32)],
        compiler_params=_params("parallel", "arbitrary"),
        name="gla",
    )(feats, gates, feats, feats, gates, feats, lbf, lbb, s0f, s0b, msum_f, msum_b)


def _mixout_body(of_ref, ob_ref, g_ref, cg_ref, bg_ref, hv_ref, halo, cw_ref, wo_ref,
                 x_ref, g1_ref, o_ref, grid_conv):
    tm = of_ref.shape[1]
    o = of_ref[0] + ob_ref[0]
    heads = []
    for h in range(HG_HEADS):
        oh = o[:, h * HG_DIM:(h + 1) * HG_DIM]
        heads.append(oh * lax.rsqrt(jnp.mean(oh * oh, axis=-1, keepdims=True) + EPS))
    g = g_ref[0].astype(F32)
    o_rec = jnp.concatenate(heads, axis=-1) * (g * jax.nn.sigmoid(g))

    u = cg_ref[0].astype(F32) * hv_ref[0].astype(F32)
    w0, w1, w2 = cw_ref[0:1, :], cw_ref[1:2, :], cw_ref[2:3, :]
    pos = lax.broadcasted_iota(jnp.int32, (tm, 1), 0)
    if grid_conv:
        cgp_ref, hvp_ref, cgn_ref, hvn_ref = halo
        i = pl.program_id(1)
        last = pl.num_programs(1) - 1
        colpos = pos & (GRID_W - 1)
        uh = u[:, :SC_HALF]
        left = jnp.where(colpos == 0, 0.0, pltpu.roll(uh, 1, 0))
        right = jnp.where(colpos == GRID_W - 1, 0.0, pltpu.roll(uh, tm - 1, 0))
        conv_h = w0[:, :SC_HALF] * left + w1[:, :SC_HALF] * uh + w2[:, :SC_HALF] * right
        uv = u[:, SC_HALF:]
        halo_u = lambda cg, hv: cg[0][:, SC_HALF:].astype(F32) * hv[0][:, SC_HALF:].astype(F32)
        up_halo = jnp.where(i == 0, 0.0, halo_u(cgp_ref, hvp_ref))
        dn_halo = jnp.where(i == last, 0.0, halo_u(cgn_ref, hvn_ref))
        up = jnp.concatenate([up_halo, uv[:tm - GRID_W]], axis=0)
        down = jnp.concatenate([uv[GRID_W:], dn_halo], axis=0)
        conv_v = w0[:, SC_HALF:] * up + w1[:, SC_HALF:] * uv + w2[:, SC_HALF:] * down
        conv = jnp.concatenate([conv_h, conv_v], axis=-1)
    else:
        left = jnp.where(pos == 0, 0.0, pltpu.roll(u, 1, 0))
        right = jnp.where(pos == tm - 1, 0.0, pltpu.roll(u, tm - 1, 0))
        conv = w0 * left + w1 * u + w2 * right
    o_conv = bg_ref[0].astype(F32) * conv
    y = _dot(o_rec, wo_ref[0:HG_WIDTH, :]) + _dot(o_conv, wo_ref[HG_WIDTH:, :])
    o_ref[0] = x_ref[0] + g1_ref[0] * y


def _mixout_grid_kernel(of_ref, ob_ref, g_ref, cg_ref, bg_ref, hv_ref, cgp_ref, hvp_ref,
                        cgn_ref, hvn_ref, cw_ref, wo_ref, x_ref, g1_ref, o_ref):
    _mixout_body(of_ref, ob_ref, g_ref, cg_ref, bg_ref, hv_ref,
                 (cgp_ref, hvp_ref, cgn_ref, hvn_ref), cw_ref, wo_ref, x_ref, g1_ref, o_ref, True)


def _mixout_seq_kernel(of_ref, ob_ref, g_ref, cg_ref, bg_ref, hv_ref, cw_ref, wo_ref,
                       x_ref, g1_ref, o_ref):
    _mixout_body(of_ref, ob_ref, g_ref, cg_ref, bg_ref, hv_ref, None, cw_ref, wo_ref,
                 x_ref, g1_ref, o_ref, False)


def _mixout_call(o_f, o_b, proj, conv_w, w_out, x, g1, grid_conv):
    bsz, t, d = x.shape
    w = HG_WIDTH
    tm = 512 if grid_conv else t
    nt = t // tm
    hb = tm // GRID_W
    nhalo = t // GRID_W

    def col(cidx):
        return pl.BlockSpec((1, tm, w), lambda b, i: (b, i, cidx))

    def prev(cidx):
        return pl.BlockSpec((1, GRID_W, w), lambda b, i: (b, jnp.maximum(i * hb - 1, 0), cidx))

    def nxt(cidx):
        return pl.BlockSpec((1, GRID_W, w),
                            lambda b, i: (b, jnp.minimum((i + 1) * hb, nhalo - 1), cidx))

    row = pl.BlockSpec((1, tm, w), lambda b, i: (b, i, 0))
    in_specs = [row, row, col(FEAT_G), col(FEAT_CG), col(FEAT_BG), col(FEAT_HV)]
    args = [o_f, o_b, proj, proj, proj, proj]
    if grid_conv:
        in_specs += [prev(FEAT_CG), prev(FEAT_HV), nxt(FEAT_CG), nxt(FEAT_HV)]
        args += [proj, proj, proj, proj]
    in_specs += [
        pl.BlockSpec((3, w), lambda b, i: (0, 0)),
        pl.BlockSpec(w_out.shape, lambda b, i: (0, 0)),
        pl.BlockSpec((1, tm, d), lambda b, i: (b, i, 0)),
        pl.BlockSpec((1, 1, d), lambda b, i: (b, 0, 0)),
    ]
    args += [conv_w, w_out, x, g1]
    return pl.pallas_call(
        _mixout_grid_kernel if grid_conv else _mixout_seq_kernel,
        grid=(bsz, nt),
        in_specs=in_specs,
        out_specs=pl.BlockSpec((1, tm, d), lambda b, i: (b, i, 0)),
        out_shape=jax.ShapeDtypeStruct((bsz, t, d), F32),
        compiler_params=_params("parallel", "parallel"),
        name="mixout_grid" if grid_conv else "mixout_seq",
    )(*args)


def _oddeven_merge_sort_pairs(n):
    pairs = []
    p = 1
    while p < n:
        k = p
        while k >= 1:
            for j in range(k % p, n - k, 2 * k):
                for i in range(min(k, n - j - k)):
                    if (i + j) // (2 * p) == (i + j + k) // (2 * p):
                        pairs.append((i + j, i + j + k))
            k //= 2
        p *= 2
    return pairs


_SORT16 = _oddeven_merge_sort_pairs(PEER_TOPK)


def _sort_desc(xs):
    xs = list(xs)
    for i, j in _SORT16:
        a, b = xs[i], xs[j]
        xs[i], xs[j] = jnp.maximum(a, b), jnp.minimum(a, b)
    return xs


def _bitonic_merge_desc(xs):
    xs = list(xs)
    d = len(xs) // 2
    while d >= 1:
        for i in range(len(xs)):
            if i & d == 0:
                a, b = xs[i], xs[i + d]
                xs[i], xs[i + d] = jnp.maximum(a, b), jnp.minimum(a, b)
        d //= 2
    return xs


def _top_of_union(a, b):
    k = len(a)
    return [jnp.maximum(a[r], b[k - 1 - r]) for r in range(k)]


def _topk_rows(s):
    rows = [s[SUBLANES * v:SUBLANES * (v + 1), :] for v in range(N_KEYS // SUBLANES)]
    rows = _sort_desc(rows)
    for shift in (4, 2, 1):
        rolled = [pltpu.roll(r, shift, 0) for r in rows]
        rows = _bitonic_merge_desc(_top_of_union(rows, rolled))
    return rows


def _peer_prep_kernel(x_ref, g_ref, sc_ref, sh_ref, wq_ref, keys_ref,
                      ht_ref, n1_ref, e1_ref, r2_ref, e2_ref, s1_ref, s2_ref):
    tq = x_ref.shape[1]
    hf = _norm_mod(x_ref[0], g_ref[...], sc_ref[0], sh_ref[0])
    ht_ref[...] = hf.T.astype(BF16)
    qf = jnp.dot(hf.astype(BF16), wq_ref[...], preferred_element_type=F32)
    sub = lax.broadcasted_iota(jnp.int32, (SUBLANES, tq), 0)
    neg = jnp.full((SUBLANES, tq), -jnp.inf, F32)
    packed = [[jnp.zeros((SUBLANES, tq), F32)] * PEER_TOPK for _ in range(2)]
    for hd in range(PEER_HEADS):
        for p in range(2):
            lo = hd * PEER_QDIM + p * PEER_HALF
            s = _dot_nt(keys_ref[hd, p], qf[:, lo:lo + PEER_HALF])
            (s1_ref if p == 0 else s2_ref)[hd] = s
            top = _topk_rows(s)
            packed[p] = [jnp.where(sub == hd, top[r], packed[p][r]) for r in range(PEER_TOPK)]
    c1, c2 = packed
    pairs = [(a, b) for a in range(PEER_TOPK) for b in range(PEER_TOPK)
             if (a + 1) * (b + 1) <= PEER_TOPK]
    cand = {ab: c1[ab[0]] + c2[ab[1]] for ab in pairs}
    cands = [cand[ab] for ab in pairs]
    cands += [neg] * (-len(cands) % PEER_TOPK)
    groups = [_sort_desc(cands[i:i + PEER_TOPK]) for i in range(0, len(cands), PEER_TOPK)]
    while len(groups) > 2:
        nxt = [_bitonic_merge_desc(_top_of_union(groups[i], groups[i + 1]))
               for i in range(0, len(groups) - 1, 2)]
        if len(groups) % 2:
            nxt.append(groups[-1])
        groups = nxt
    top = _top_of_union(groups[0], groups[1]) if len(groups) == 2 else groups[0]
    tau = functools.reduce(jnp.minimum, top)
    m1, m2 = c1[0], c2[0]
    mx = m1 + m2
    z = functools.reduce(lambda a, b: a + b, [jnp.exp(t - mx) for t in top])
    inv_z = 0.5 / z
    counts = []
    for a in range(PEER_TOPK):
        n = jnp.zeros((SUBLANES, tq), F32)
        for b in range(PEER_TOPK):
            if (a, b) in cand:
                n = n + jnp.where(cand[(a, b)] >= tau, 1.0, 0.0)
        counts.append(n)
    half = PEER_TOPK // 2
    theta = []
    for m in range(1, half + 1):
        th = jnp.full((SUBLANES, tq), jnp.inf, F32)
        for a in range(PEER_TOPK // m):
            th = jnp.where(counts[a] >= m, c1[a], th)
        theta.append(th)
    grouped = (N_KEYS // SUBLANES, SUBLANES, tq)
    for hd in range(PEER_HEADS):
        s1 = s1_ref[hd]
        s2 = s2_ref[hd]
        n1 = jnp.zeros((N_KEYS, tq), F32)
        for m in range(1, half + 1):
            n1 = jnp.where(s1 >= theta[m - 1][hd:hd + 1, :], float(m), n1)
        n1 = jnp.where(s1 >= c1[0][hd:hd + 1, :], counts[0][hd:hd + 1, :], n1)
        rank2 = jnp.full((N_KEYS, tq), float(PEER_TOPK), F32)
        for r in reversed(range(PEER_TOPK)):
            rank2 = jnp.where(s2 == c2[r][hd:hd + 1, :], float(r), rank2)
        n1_ref[hd] = n1.reshape(grouped)
        r2_ref[hd] = pltpu.bitcast(rank2.astype(BF16), jnp.uint32)
        e1 = jnp.exp(s1 - m1[hd:hd + 1, :]) * inv_z[hd:hd + 1, :]
        e1_ref[hd] = e1.reshape(grouped)
        e2_ref[hd] = pltpu.bitcast(jnp.exp(s2 - m2[hd:hd + 1, :]).astype(BF16), jnp.uint32)


def _peer_prep_call(x, g, scale, shift, wq, keys, l):
    bsz, t, d = x.shape
    tq = min(256, t)
    nt = t // tq
    ttot = bsz * t
    score_spec = pl.BlockSpec((PEER_HEADS, N_KEYS // 2, tq), lambda b, i: (0, 0, b * nt + i))
    score_shape = jax.ShapeDtypeStruct((PEER_HEADS, N_KEYS // 2, ttot), jnp.uint32)
    n_grp = N_KEYS // SUBLANES
    grouped_spec = pl.BlockSpec((PEER_HEADS, n_grp, SUBLANES, tq), lambda b, i: (0, 0, 0, b * nt + i))
    grouped_shape = jax.ShapeDtypeStruct((PEER_HEADS, n_grp, SUBLANES, ttot), F32)
    return pl.pallas_call(
        _peer_prep_kernel,
        grid=(bsz, nt),
        in_specs=[
            pl.BlockSpec((1, tq, d), lambda b, i: (b, i, 0)),
            pl.BlockSpec((1, d), lambda b, i: (0, 0)),
            pl.BlockSpec((1, 1, d), lambda b, i: (b, 0, 0)),
            pl.BlockSpec((1, 1, d), lambda b, i: (b, 0, 0)),
            pl.BlockSpec((None,) + wq.shape[1:], lambda b, i: (l, 0, 0)),
            pl.BlockSpec((None,) + keys.shape[1:], lambda b, i: (l, 0, 0, 0, 0)),
        ],
        out_specs=[pl.BlockSpec((d, tq), lambda b, i: (0, b * nt + i)),
                   grouped_spec, grouped_spec, score_spec, score_spec],
        out_shape=[jax.ShapeDtypeStruct((d, ttot), BF16),
                   grouped_shape, grouped_shape, score_shape, score_shape],
        scratch_shapes=[pltpu.VMEM((PEER_HEADS, N_KEYS, tq), F32),
                        pltpu.VMEM((PEER_HEADS, N_KEYS, tq), F32)],
        compiler_params=_params("parallel", "parallel"),
        name="peer_prep",
    )(x, g, scale, shift, wq, keys)


PEER_TE = SUBLANES * N_KEYS
SQRT_HALF = math.sqrt(0.5)


def _peer_dense_kernel(ht_ref, u_ref, vt_ref, n1_ref, e1_ref, r2_ref, e2_ref,
                       x_ref, g2_ref, o_ref, acc_ref, a_ref, gt_ref, *, n_e, n_steps):
    s = pl.program_id(0)
    tt = ht_ref.shape[1]
    n_lane_tiles = tt // LANES
    n_blk = N_KEYS // PACKED_ROWS

    @pl.when(s == 0)
    def _():
        a_ref[...] = jnp.zeros_like(a_ref)
        gt_ref[...] = jnp.zeros_like(gt_ref)

    @pl.when(jnp.logical_or(s == 0, (s - 2) % n_e == 0))
    def _():
        acc_ref[...] = jnp.zeros_like(acc_ref)

    acc_ref[...] += jnp.dot(vt_ref[0], gt_ref[...], preferred_element_type=F32)

    grp = jnp.clip(s - 1, 0, n_steps - 1) % n_e
    packed = (PACKED_ROWS, LANES)
    for il in range(SUBLANES):
        for j in range(n_lane_tiles):
            lanes = pl.ds(j * LANES, LANES)
            acc = [None] * n_blk
            for hd in range(PEER_HEADS):
                n1b = jnp.broadcast_to(n1_ref[hd, grp, pl.ds(il, 1), lanes], packed).astype(BF16)
                e1b = jnp.broadcast_to(e1_ref[hd, grp, pl.ds(il, 1), lanes], packed).astype(BF16)
                for k in range(n_blk):
                    rows = pl.ds(k * SUBLANES, SUBLANES)
                    rank2 = pltpu.bitcast(r2_ref[hd, rows, lanes], BF16)
                    e2 = pltpu.bitcast(e2_ref[hd, rows, lanes], BF16)
                    w = jnp.where(rank2 < n1b, e2, 0.0) * e1b
                    acc[k] = w if hd == 0 else acc[k] + w
            for k in range(n_blk):
                rows = pl.ds(il * N_KEYS + k * PACKED_ROWS, PACKED_ROWS)
                a = a_ref[rows, lanes].astype(BF16)
                gelu2 = a * (1.0 + lax.erf(a * SQRT_HALF))
                gt_ref[rows, lanes] = acc[k] * gelu2

    a_ref[...] = jnp.dot(u_ref[...], ht_ref[...], preferred_element_type=F32)

    @pl.when(jnp.logical_and(s >= 2, (s - 2) % n_e == n_e - 1))
    def _():
        o_ref[...] = x_ref[...] + g2_ref[0] * acc_ref[...].T


def _peer_dense_call(ht, u, vt, n1, e1, r2, e2, x, g2, t_per_batch, l):
    ttot, d = x.shape
    tt = min(512, t_per_batch)
    n_e = N_EXPERTS // PEER_TE
    n_steps = (ttot // tt) * n_e
    per_batch = t_per_batch // tt
    tile = lambda s, lag: jnp.clip(s - lag, 0, n_steps - 1) // n_e
    group = lambda s, lag: jnp.clip(s - lag, 0, n_steps - 1) % n_e
    score_spec = pl.BlockSpec((PEER_HEADS, N_KEYS // 2, tt), lambda s: (0, 0, tile(s, 1)))
    grouped_spec = pl.BlockSpec((PEER_HEADS, N_KEYS // SUBLANES, SUBLANES, tt),
                                lambda s: (0, 0, 0, tile(s, 1)))
    return pl.pallas_call(
        functools.partial(_peer_dense_kernel, n_e=n_e, n_steps=n_steps),
        grid=(n_steps + 2,),
        in_specs=[
            pl.BlockSpec((d, tt), lambda s: (0, tile(s, 0))),
            pl.BlockSpec((None, PEER_TE, d), lambda s: (l, group(s, 0), 0)),
            pl.BlockSpec((None, 1, d, PEER_TE), lambda s: (l, group(s, 2), 0, 0)),
            grouped_spec, grouped_spec, score_spec, score_spec,
            pl.BlockSpec((tt, d), lambda s: (tile(s, 2), 0)),
            pl.BlockSpec((1, 1, d), lambda s: (tile(s, 2) // per_batch, 0, 0)),
        ],
        out_specs=pl.BlockSpec((tt, d), lambda s: (tile(s, 2), 0)),
        out_shape=jax.ShapeDtypeStruct((ttot, d), F32),
        scratch_shapes=[pltpu.VMEM((d, tt), F32),
                        pltpu.VMEM((PEER_TE, tt), F32),
                        pltpu.VMEM((PEER_TE, tt), BF16)],
        compiler_params=_params("arbitrary"),
        name="peer_dense",
    )(ht, u, vt, n1, e1, r2, e2, x, g2)


def _final_norm_kernel(x_ref, g_ref, o_ref):
    x = x_ref[...]
    ms = jnp.mean(x * x, axis=-1, keepdims=True)
    o_ref[...] = x * lax.rsqrt(ms + EPS) * g_ref[...]


def _final_norm_call(x, g):
    n, d = x.shape
    tm = 1024
    return pl.pallas_call(
        _final_norm_kernel,
        grid=(n // tm,),
        in_specs=[pl.BlockSpec((tm, d), lambda i: (i, 0)), pl.BlockSpec((1, d), lambda i: (0, 0))],
        out_specs=pl.BlockSpec((tm, d), lambda i: (i, 0)),
        out_shape=jax.ShapeDtypeStruct((n, d), F32),
        compiler_params=_params("parallel"),
        name="final_norm",
    )(x, g)


def _mixer(x, mod, l, norm1_g, w_in, conv_w, w_out, lbf, lbb, s0f, s0b, grid_conv, full):
    sh1, sc1, g1 = mod[0], mod[1], mod[2]
    gates, feats = _proj_call(x, norm1_g[l][None, :], sc1, sh1, w_in, l)
    o_f, o_b, s_f, s_b = _gla_call(gates, feats, lbf, lbb, s0f, s0b)
    if not full:
        return None, s_f, s_b
    x = _mixout_call(o_f, o_b, feats, conv_w[l], w_out[l], x, g1, grid_conv)
    return x, s_f, s_b


def _peer(x, mod, l, norm2_g, wq, keys, u, vt):
    bsz, t, d = x.shape
    sh2, sc2, g2 = mod[3], mod[4], mod[5]
    ht, n1, e1, r2, e2 = _peer_prep_call(x, norm2_g[l][None, :], sc2, sh2, wq, keys, l)
    out = _peer_dense_call(ht, u, vt, n1, e1, r2, e2, x.reshape(bsz * t, d), g2, t, l)
    return out.reshape(bsz, t, d)


def kernel(x, c, ctx, c_ctx, w_mod, b_mod, norm1_g, norm2_g, w_in, conv_w, w_out, lb_logits,
           peer_wq, peer_subkeys, peer_u, peer_v, final_g):
    bsz, t, d = x.shape
    depth = w_mod.shape[0]

    p_lb = jax.nn.softmax(lb_logits.astype(F32), axis=0)
    lower = jnp.cumsum(p_lb, axis=0) - p_lb[0]

    cond = jnp.zeros((COND_ROWS, d), F32).at[:bsz].set(c).at[bsz].set(c_ctx)
    mod = _mod_call(cond, w_mod, b_mod)
    mod = mod.reshape(depth, COND_ROWS, N_MOD, d)
    mod_x = jnp.transpose(mod[:, :bsz], (0, 2, 1, 3))[:, :, :, None, :]
    mod_c = jnp.broadcast_to(mod[:, bsz][:, :, None, None, :], mod_x.shape)

    w_in_b = w_in.astype(BF16)
    w_out_b = w_out.astype(BF16)
    wq_b = peer_wq.astype(BF16)
    keys_b = peer_subkeys.astype(BF16)
    u_b = peer_u.astype(BF16)
    vt_b = jnp.swapaxes(peer_v.astype(BF16).reshape(depth, N_EXPERTS // PEER_TE, PEER_TE, d), 2, 3)

    zero_state = jnp.zeros((bsz, HG_HEADS, HG_DIM, HG_DIM), F32)
    xc = ctx
    for l in range(depth):
        lbf = lower[l, 0][None, :]
        lbb = lower[l, 1][None, :]
        full = l < depth - 1
        xc_new, s_f, s_b = _mixer(xc, mod_c[l], l, norm1_g, w_in_b, conv_w, w_out_b, lbf, lbb,
                                  zero_state, zero_state, False, full)
        if full:
            xc = _peer(xc_new, mod_c[l], l, norm2_g, wq_b, keys_b, u_b, vt_b)
        x, _, _ = _mixer(x, mod_x[l], l, norm1_g, w_in_b, conv_w, w_out_b, lbf, lbb,
                         s_f, s_b, True, True)
        x = _peer(x, mod_x[l], l, norm2_g, wq_b, keys_b, u_b, vt_b)
    return _final_norm_call(x.reshape(bsz * t, d), final_g[None, :]).reshape(bsz, t, d)
```

```python
import functools
import math

import numpy as np
import jax
import jax.numpy as jnp
from jax import lax
from jax.experimental import pallas as pl
from jax.experimental.pallas import tpu as pltpu

F32 = jnp.float32
BF16 = jnp.bfloat16

D_MODEL = 1024
GRID_W = 64
EPS = 1e-6
F_FLOOR = 1e-20
N_MOD = 6
HG_WIDTH = 512
HG_HEADS = 4
HG_DIM = HG_WIDTH // HG_HEADS
SC_WIDTH = 512
SC_HALF = SC_WIDTH // 2
IN_COLS = 5 * HG_WIDTH + 3 * SC_WIDTH
PEER_HEADS = 8
PEER_QDIM = 256
PEER_HALF = PEER_QDIM // 2
N_KEYS = 128
N_EXPERTS = N_KEYS * N_KEYS
PEER_TOPK = 16

SUBLANES = 8
LANES = 128
PACKED_ROWS = 2 * SUBLANES
VMEM_LIMIT = 48 * 1024 * 1024

GLA_CHUNK = 128
GLA_LEVELS = (64, 32, 16, 8, 4, 2, 1)
GLA_GROUP_HEADS = 2
COND_ROWS = 8

COL_IV, COL_ZF, COL_ZB, COL_Q, COL_G, COL_CG, COL_BG, COL_HV = range(8)
GATE_ZF, GATE_ZB = range(2)
FEAT_IV, FEAT_Q, FEAT_G, FEAT_CG, FEAT_BG, FEAT_HV = range(6)


def _params(*sem):
    return pltpu.CompilerParams(dimension_semantics=sem, vmem_limit_bytes=VMEM_LIMIT)


def _dot(a, b):
    return jnp.dot(a.astype(BF16), b.astype(BF16), preferred_element_type=F32)


def _dot_nt(a, b):
    return lax.dot_general(a.astype(BF16), b.astype(BF16), (((1,), (1,)), ((), ())),
                           preferred_element_type=F32)


def _dot_tn(a, b):
    return lax.dot_general(a.astype(BF16), b.astype(BF16), (((0,), (0,)), ((), ())),
                           preferred_element_type=F32)


def _mod_kernel(cond_ref, w_ref, b_ref, o_ref):
    c = cond_ref[...]
    s = c * jax.nn.sigmoid(c)
    o_ref[0] = _dot(s, w_ref[0]) + b_ref[0]


def _mod_call(cond, w_mod, b_mod):
    depth, d, n = w_mod.shape
    tn = n // 2
    return pl.pallas_call(
        _mod_kernel,
        grid=(depth, n // tn),
        in_specs=[
            pl.BlockSpec((COND_ROWS, d), lambda l, j: (0, 0)),
            pl.BlockSpec((1, d, tn), lambda l, j: (l, 0, j)),
            pl.BlockSpec((1, 1, tn), lambda l, j: (l, 0, j)),
        ],
        out_specs=pl.BlockSpec((1, COND_ROWS, tn), lambda l, j: (l, 0, j)),
        out_shape=jax.ShapeDtypeStruct((depth, COND_ROWS, n), F32),
        compiler_params=_params("parallel", "parallel"),
        name="mod",
    )(cond, w_mod, b_mod.reshape(depth, 1, n))


def _norm_mod(x, g, scale, shift):
    ms = jnp.mean(x * x, axis=-1, keepdims=True)
    y = x * lax.rsqrt(ms + EPS) * g
    return y * (1.0 + scale) + shift


def _proj_kernel(x_ref, g_ref, sc_ref, sh_ref, w_ref, gates_ref, feats_ref):
    h = _norm_mod(x_ref[0], g_ref[...], sc_ref[0], sh_ref[0])
    y = _dot(h, w_ref[...])
    w = HG_WIDTH
    gates_ref[0] = y[:, COL_ZF * w:(COL_ZB + 1) * w]
    feats_ref[0, :, 0:w] = y[:, COL_IV * w:(COL_IV + 1) * w].astype(BF16)
    feats_ref[0, :, w:] = y[:, COL_Q * w:].astype(BF16)


def _proj_call(x, g, scale, shift, w, l):
    bsz, t, d = x.shape
    n = w.shape[2]
    n_gate = 2 * HG_WIDTH
    tm = min(256, t)
    return pl.pallas_call(
        _proj_kernel,
        grid=(bsz, t // tm),
        in_specs=[
            pl.BlockSpec((1, tm, d), lambda b, i: (b, i, 0)),
            pl.BlockSpec((1, d), lambda b, i: (0, 0)),
            pl.BlockSpec((1, 1, d), lambda b, i: (b, 0, 0)),
            pl.BlockSpec((1, 1, d), lambda b, i: (b, 0, 0)),
            pl.BlockSpec((None, d, n), lambda b, i: (l, 0, 0)),
        ],
        out_specs=[pl.BlockSpec((1, tm, n_gate), lambda b, i: (b, i, 0)),
                   pl.BlockSpec((1, tm, n - n_gate), lambda b, i: (b, i, 0))],
        out_shape=[jax.ShapeDtypeStruct((bsz, t, n_gate), F32),
                   jax.ShapeDtypeStruct((bsz, t, n - n_gate), BF16)],
        compiler_params=_params("parallel", "parallel"),
        name="proj",
    )(x, g, scale, shift, w)


def _gla_sum_matrix(backward):
    c = GLA_CHUNK
    t = np.arange(c)[:, None]
    s = np.arange(c)[None, :]
    mats = [s >= t if backward else s <= t]
    for n in GLA_LEVELS:
        if n >= SUBLANES:
            continue
        mid = (t // (2 * n)) * (2 * n) + n
        if not backward:
            m = np.where(t >= mid, (s >= mid) & (s <= t), (s > t) & (s < mid))
        else:
            m = np.where(t < mid, (s >= t) & (s < mid), (s >= mid) & (s < t))
        mats.append(m)
    return np.concatenate([m.astype(np.float32) for m in mats], axis=0)


def _gla_masks(backward):
    c = GLA_CHUNK
    row = lax.broadcasted_iota(jnp.int32, (c, c), 0)
    col = lax.broadcasted_iota(jnp.int32, (c, c), 1)
    qrow, lvl = [], []
    for n in GLA_LEVELS:
        r_hi = (row & (2 * n - 1)) >= n
        c_hi = (col & (2 * n - 1)) >= n
        same = (row & ~(2 * n - 1)) == (col & ~(2 * n - 1))
        if not backward:
            qrow.append(r_hi)
            lvl.append(same & r_hi & jnp.logical_not(c_hi))
        else:
            qrow.append(jnp.logical_not(r_hi))
            lvl.append(same & jnp.logical_not(r_hi) & c_hi)
    return qrow, lvl, row == col


def _gla_gates(z, lb, msum):
    sig = jax.nn.sigmoid(z)
    f = lb + (1.0 - lb) * sig
    logf = jnp.log(jnp.maximum(f, F_FLOOR))
    kk = (1.0 - lb) * jax.nn.sigmoid(-z)
    sums = jnp.dot(msum, logf.astype(BF16), preferred_element_type=F32)
    return kk, sums


def _boundary_rows(b, n, backward):
    c, w = b.shape
    parts = []
    for start in range(0, c, 2 * n):
        r = start + n if backward else start + n - 1
        parts.append(jnp.broadcast_to(b[r:r + 1, :], (2 * n, w)))
    return jnp.concatenate(parts, axis=0)


def _gla_chains(chains, ones):
    c = GLA_CHUNK
    atts = [jnp.where(masks[2], jnp.dot((q * kk).astype(BF16), ones, preferred_element_type=F32), 0.0)
            for q, v, kk, sums, st, masks, backward in chains]
    sub_tile = [n for n in GLA_LEVELS if n < SUBLANES]
    for li, n in enumerate(GLA_LEVELS):
        for ci, (q, v, kk, sums, st, masks, backward) in enumerate(chains):
            if n >= SUBLANES:
                exponent = -jnp.abs(sums[0:c] - _boundary_rows(sums[0:c], n, backward))
            else:
                k = 1 + sub_tile.index(n)
                exponent = sums[k * c:(k + 1) * c]
            mixed = (jnp.where(masks[0][li][:, :HG_DIM], q, kk) * jnp.exp(exponent)).astype(BF16)
            a = lax.dot_general(mixed, mixed, (((1,), (1,)), ((), ())), preferred_element_type=F32)
            atts[ci] = jnp.where(masks[1][li], a, atts[ci])
    out = []
    for att, (q, v, kk, sums, st, masks, backward) in zip(atts, chains):
        b = sums[0:c]
        b_end = b[0:1, :] if backward else b[c - 1:c, :]
        q_in = q * jnp.exp(b)
        o = _dot(att, v) + _dot_nt(q_in, st)
        k_out = kk * jnp.exp(b_end - b)
        out.append((o, jnp.exp(b_end) * st + _dot_tn(v, k_out)))
    return out


def _gla_kernel(ivf_ref, zf_ref, qf_ref, ivb_ref, zb_ref, qb_ref, lbf_ref, lbb_ref,
                s0f_ref, s0b_ref, mf_ref, mb_ref,
                of_ref, ob_ref, sf_ref, sb_ref, st_ref):
    n = pl.program_id(1)

    @pl.when(n == 0)
    def _():
        st_ref[0] = s0f_ref[0]
        st_ref[1] = s0b_ref[0]

    ones = jnp.ones((HG_DIM, HG_DIM), BF16)
    masks_f = _gla_masks(False)
    masks_b = _gla_masks(True)
    kk_f, sums_f = _gla_gates(zf_ref[0], lbf_ref[...], mf_ref[...])
    kk_b, sums_b = _gla_gates(zb_ref[0], lbb_ref[...], mb_ref[...])
    for h0 in range(0, HG_HEADS, GLA_GROUP_HEADS):
        heads = range(h0, h0 + GLA_GROUP_HEADS)
        sls = [slice(h * HG_DIM, (h + 1) * HG_DIM) for h in heads]
        chains = [(qf_ref[0, :, sl].astype(F32), ivf_ref[0, :, sl], kk_f[:, sl], sums_f[:, sl],
                   st_ref[0, h], masks_f, False) for h, sl in zip(heads, sls)]
        chains += [(qb_ref[0, :, sl].astype(F32), ivb_ref[0, :, sl], kk_b[:, sl], sums_b[:, sl],
                    st_ref[1, h], masks_b, True) for h, sl in zip(heads, sls)]
        res = _gla_chains(chains, ones)
        for i, (h, sl) in enumerate(zip(heads, sls)):
            of_ref[0, :, sl], st_ref[0, h] = res[i]
            ob_ref[0, :, sl], st_ref[1, h] = res[GLA_GROUP_HEADS + i]

    @pl.when(n == pl.num_programs(1) - 1)
    def _():
        sf_ref[0] = st_ref[0]
        sb_ref[0] = st_ref[1]


def _gla_call(gates, feats, lbf, lbb, s0f, s0b):
    bsz, t, _ = gates.shape
    c = GLA_CHUNK
    nc = t // c
    w = HG_WIDTH
    msum_f = jnp.asarray(_gla_sum_matrix(False), BF16)
    msum_b = jnp.asarray(_gla_sum_matrix(True), BF16)

    def fwd(col):
        return pl.BlockSpec((1, c, w), lambda b, n: (b, n, col))

    def bwd(col):
        return pl.BlockSpec((1, c, w), lambda b, n: (b, nc - 1 - n, col))

    state_spec = pl.BlockSpec((1, HG_HEADS, HG_DIM, HG_DIM), lambda b, n: (b, 0, 0, 0))
    const = lambda shape: pl.BlockSpec(shape, lambda b, n: (0,) * len(shape))
    return pl.pallas_call(
        _gla_kernel,
        grid=(bsz, nc),
        in_specs=[fwd(FEAT_IV), fwd(GATE_ZF), fwd(FEAT_Q), bwd(FEAT_IV), bwd(GATE_ZB), bwd(FEAT_Q),
                  const((1, w)), const((1, w)), state_spec, state_spec,
                  const(msum_f.shape), const(msum_b.shape)],
        out_specs=[pl.BlockSpec((1, c, w), lambda b, n: (b, n, 0)),
                   pl.BlockSpec((1, c, w), lambda b, n: (b, nc - 1 - n, 0)),
                   state_spec, state_spec],
        out_shape=[jax.ShapeDtypeStruct((bsz, t, w), F32),
                   jax.ShapeDtypeStruct((bsz, t, w), F32),
                   jax.ShapeDtypeStruct((bsz, HG_HEADS, HG_DIM, HG_DIM), F32),
                   jax.ShapeDtypeStruct((bsz, HG_HEADS, HG_DIM, HG_DIM), F32)],
        scratch_shapes=[pltpu.VMEM((2, HG_HEADS, HG_DIM, HG_DIM), F32)],
        compiler_params=_params("parallel", "arbitrary"),
        name="gla",
    )(feats, gates, feats, feats, gates, feats, lbf, lbb, s0f, s0b, msum_f, msum_b)


def _mixout_body(of_ref, ob_ref, g_ref, cg_ref, bg_ref, hv_ref, halo, cw_ref, wo_ref,
                 x_ref, g1_ref, o_ref, grid_conv):
    tm = of_ref.shape[1]
    o = of_ref[0] + ob_ref[0]
    heads = []
    for h in range(HG_HEADS):
        oh = o[:, h * HG_DIM:(h + 1) * HG_DIM]
        heads.append(oh * lax.rsqrt(jnp.mean(oh * oh, axis=-1, keepdims=True) + EPS))
    g = g_ref[0].astype(F32)
    o_rec = jnp.concatenate(heads, axis=-1) * (g * jax.nn.sigmoid(g))

    u = cg_ref[0].astype(F32) * hv_ref[0].astype(F32)
    w0, w1, w2 = cw_ref[0:1, :], cw_ref[1:2, :], cw_ref[2:3, :]
    pos = lax.broadcasted_iota(jnp.int32, (tm, 1), 0)
    if grid_conv:
        cgp_ref, hvp_ref, cgn_ref, hvn_ref = halo
        i = pl.program_id(1)
        last = pl.num_programs(1) - 1
        colpos = pos & (GRID_W - 1)
        uh = u[:, :SC_HALF]
        left = jnp.where(colpos == 0, 0.0, pltpu.roll(uh, 1, 0))
        right = jnp.where(colpos == GRID_W - 1, 0.0, pltpu.roll(uh, tm - 1, 0))
        conv_h = w0[:, :SC_HALF] * left + w1[:, :SC_HALF] * uh + w2[:, :SC_HALF] * right
        uv = u[:, SC_HALF:]
        halo_u = lambda cg, hv: cg[0][:, SC_HALF:].astype(F32) * hv[0][:, SC_HALF:].astype(F32)
        up_halo = jnp.where(i == 0, 0.0, halo_u(cgp_ref, hvp_ref))
        dn_halo = jnp.where(i == last, 0.0, halo_u(cgn_ref, hvn_ref))
        up = jnp.concatenate([up_halo, uv[:tm - GRID_W]], axis=0)
        down = jnp.concatenate([uv[GRID_W:], dn_halo], axis=0)
        conv_v = w0[:, SC_HALF:] * up + w1[:, SC_HALF:] * uv + w2[:, SC_HALF:] * down
        conv = jnp.concatenate([conv_h, conv_v], axis=-1)
    else:
        left = jnp.where(pos == 0, 0.0, pltpu.roll(u, 1, 0))
        right = jnp.where(pos == tm - 1, 0.0, pltpu.roll(u, tm - 1, 0))
        conv = w0 * left + w1 * u + w2 * right
    o_conv = bg_ref[0].astype(F32) * conv
    y = _dot(o_rec, wo_ref[0:HG_WIDTH, :]) + _dot(o_conv, wo_ref[HG_WIDTH:, :])
    o_ref[0] = x_ref[0] + g1_ref[0] * y


def _mixout_grid_kernel(of_ref, ob_ref, g_ref, cg_ref, bg_ref, hv_ref, cgp_ref, hvp_ref,
                        cgn_ref, hvn_ref, cw_ref, wo_ref, x_ref, g1_ref, o_ref):
    _mixout_body(of_ref, ob_ref, g_ref, cg_ref, bg_ref, hv_ref,
                 (cgp_ref, hvp_ref, cgn_ref, hvn_ref), cw_ref, wo_ref, x_ref, g1_ref, o_ref, True)


def _mixout_seq_kernel(of_ref, ob_ref, g_ref, cg_ref, bg_ref, hv_ref, cw_ref, wo_ref,
                       x_ref, g1_ref, o_ref):
    _mixout_body(of_ref, ob_ref, g_ref, cg_ref, bg_ref, hv_ref, None, cw_ref, wo_ref,
                 x_ref, g1_ref, o_ref, False)


def _mixout_call(o_f, o_b, proj, conv_w, w_out, x, g1, grid_conv):
    bsz, t, d = x.shape
    w = HG_WIDTH
    tm = 512 if grid_conv else t
    nt = t // tm
    hb = tm // GRID_W
    nhalo = t // GRID_W

    def col(cidx):
        return pl.BlockSpec((1, tm, w), lambda b, i: (b, i, cidx))

    def prev(cidx):
        return pl.BlockSpec((1, GRID_W, w), lambda b, i: (b, jnp.maximum(i * hb - 1, 0), cidx))

    def nxt(cidx):
        return pl.BlockSpec((1, GRID_W, w),
                            lambda b, i: (b, jnp.minimum((i + 1) * hb, nhalo - 1), cidx))

    row = pl.BlockSpec((1, tm, w), lambda b, i: (b, i, 0))
    in_specs = [row, row, col(FEAT_G), col(FEAT_CG), col(FEAT_BG), col(FEAT_HV)]
    args = [o_f, o_b, proj, proj, proj, proj]
    if grid_conv:
        in_specs += [prev(FEAT_CG), prev(FEAT_HV), nxt(FEAT_CG), nxt(FEAT_HV)]
        args += [proj, proj, proj, proj]
    in_specs += [
        pl.BlockSpec((3, w), lambda b, i: (0, 0)),
        pl.BlockSpec(w_out.shape, lambda b, i: (0, 0)),
        pl.BlockSpec((1, tm, d), lambda b, i: (b, i, 0)),
        pl.BlockSpec((1, 1, d), lambda b, i: (b, 0, 0)),
    ]
    args += [conv_w, w_out, x, g1]
    return pl.pallas_call(
        _mixout_grid_kernel if grid_conv else _mixout_seq_kernel,
        grid=(bsz, nt),
        in_specs=in_specs,
        out_specs=pl.BlockSpec((1, tm, d), lambda b, i: (b, i, 0)),
        out_shape=jax.ShapeDtypeStruct((bsz, t, d), F32),
        compiler_params=_params("parallel", "parallel"),
        name="mixout_grid" if grid_conv else "mixout_seq",
    )(*args)


def _oddeven_merge_sort_pairs(n):
    pairs = []
    p = 1
    while p < n:
        k = p
        while k >= 1:
            for j in range(k % p, n - k, 2 * k):
                for i in range(min(k, n - j - k)):
                    if (i + j) // (2 * p) == (i + j + k) // (2 * p):
                        pairs.append((i + j, i + j + k))
            k //= 2
        p *= 2
    return pairs


_SORT16 = _oddeven_merge_sort_pairs(PEER_TOPK)


def _sort_desc(xs):
    xs = list(xs)
    for i, j in _SORT16:
        a, b = xs[i], xs[j]
        xs[i], xs[j] = jnp.maximum(a, b), jnp.minimum(a, b)
    return xs


def _bitonic_merge_desc(xs):
    xs = list(xs)
    d = len(xs) // 2
    while d >= 1:
        for i in range(len(xs)):
            if i & d == 0:
                a, b = xs[i], xs[i + d]
                xs[i], xs[i + d] = jnp.maximum(a, b), jnp.minimum(a, b)
        d //= 2
    return xs


def _top_of_union(a, b):
    k = len(a)
    return [jnp.maximum(a[r], b[k - 1 - r]) for r in range(k)]


def _topk_rows(s):
    rows = [s[SUBLANES * v:SUBLANES * (v + 1), :] for v in range(N_KEYS // SUBLANES)]
    rows = _sort_desc(rows)
    for shift in (4, 2, 1):
        rolled = [pltpu.roll(r, shift, 0) for r in rows]
        rows = _bitonic_merge_desc(_top_of_union(rows, rolled))
    return rows


def _peer_prep_kernel(x_ref, g_ref, sc_ref, sh_ref, wq_ref, keys_ref,
                      ht_ref, n1_ref, e1_ref, r2_ref, e2_ref, s1_ref, s2_ref):
    tq = x_ref.shape[1]
    hf = _norm_mod(x_ref[0], g_ref[...], sc_ref[0], sh_ref[0])
    ht_ref[...] = hf.T.astype(BF16)
    qf = jnp.dot(hf.astype(BF16), wq_ref[...], preferred_element_type=F32)
    sub = lax.broadcasted_iota(jnp.int32, (SUBLANES, tq), 0)
    neg = jnp.full((SUBLANES, tq), -jnp.inf, F32)
    packed = [[jnp.zeros((SUBLANES, tq), F32)] * PEER_TOPK for _ in range(2)]
    for hd in range(PEER_HEADS):
        for p in range(2):
            lo = hd * PEER_QDIM + p * PEER_HALF
            s = _dot_nt(keys_ref[hd, p], qf[:, lo:lo + PEER_HALF])
            (s1_ref if p == 0 else s2_ref)[hd] = s
            top = _topk_rows(s)
            packed[p] = [jnp.where(sub == hd, top[r], packed[p][r]) for r in range(PEER_TOPK)]
    c1, c2 = packed
    pairs = [(a, b) for a in range(PEER_TOPK) for b in range(PEER_TOPK)
             if (a + 1) * (b + 1) <= PEER_TOPK]
    cand = {ab: c1[ab[0]] + c2[ab[1]] for ab in pairs}
    cands = [cand[ab] for ab in pairs]
    cands += [neg] * (-len(cands) % PEER_TOPK)
    groups = [_sort_desc(cands[i:i + PEER_TOPK]) for i in range(0, len(cands), PEER_TOPK)]
    while len(groups) > 2:
        nxt = [_bitonic_merge_desc(_top_of_union(groups[i], groups[i + 1]))
               for i in range(0, len(groups) - 1, 2)]
        if len(groups) % 2:
            nxt.append(groups[-1])
        groups = nxt
    top = _top_of_union(groups[0], groups[1]) if len(groups) == 2 else groups[0]
    tau = functools.reduce(jnp.minimum, top)
    m1, m2 = c1[0], c2[0]
    mx = m1 + m2
    z = functools.reduce(lambda a, b: a + b, [jnp.exp(t - mx) for t in top])
    inv_z = 0.5 / z
    counts = []
    for a in range(PEER_TOPK):
        n = jnp.zeros((SUBLANES, tq), F32)
        for b in range(PEER_TOPK):
            if (a, b) in cand:
                n = n + jnp.where(cand[(a, b)] >= tau, 1.0, 0.0)
        counts.append(n)
    half = PEER_TOPK // 2
    theta = []
    for m in range(1, half + 1):
        th = jnp.full((SUBLANES, tq), jnp.inf, F32)
        for a in range(PEER_TOPK // m):
            th = jnp.where(counts[a] >= m, c1[a], th)
        theta.append(th)
    grouped = (N_KEYS // SUBLANES, SUBLANES, tq)
    for hd in range(PEER_HEADS):
        s1 = s1_ref[hd]
        s2 = s2_ref[hd]
        n1 = jnp.zeros((N_KEYS, tq), F32)
        for m in range(1, half + 1):
            n1 = jnp.where(s1 >= theta[m - 1][hd:hd + 1, :], float(m), n1)
        n1 = jnp.where(s1 >= c1[0][hd:hd + 1, :], counts[0][hd:hd + 1, :], n1)
        rank2 = jnp.full((N_KEYS, tq), float(PEER_TOPK), F32)
        for r in reversed(range(PEER_TOPK)):
            rank2 = jnp.where(s2 == c2[r][hd:hd + 1, :], float(r), rank2)
        n1_ref[hd] = n1.reshape(grouped)
        r2_ref[hd] = pltpu.bitcast(rank2.astype(BF16), jnp.uint32)
        e1 = jnp.exp(s1 - m1[hd:hd + 1, :]) * inv_z[hd:hd + 1, :]
        e1_ref[hd] = e1.reshape(grouped)
        e2_ref[hd] = pltpu.bitcast(jnp.exp(s2 - m2[hd:hd + 1, :]).astype(BF16), jnp.uint32)


def _peer_prep_call(x, g, scale, shift, wq, keys, l):
    bsz, t, d = x.shape
    tq = min(256, t)
    nt = t // tq
    ttot = bsz * t
    score_spec = pl.BlockSpec((PEER_HEADS, N_KEYS // 2, tq), lambda b, i: (0, 0, b * nt + i))
    score_shape = jax.ShapeDtypeStruct((PEER_HEADS, N_KEYS // 2, ttot), jnp.uint32)
    n_grp = N_KEYS // SUBLANES
    grouped_spec = pl.BlockSpec((PEER_HEADS, n_grp, SUBLANES, tq), lambda b, i: (0, 0, 0, b * nt + i))
    grouped_shape = jax.ShapeDtypeStruct((PEER_HEADS, n_grp, SUBLANES, ttot), F32)
    return pl.pallas_call(
        _peer_prep_kernel,
        grid=(bsz, nt),
        in_specs=[
            pl.BlockSpec((1, tq, d), lambda b, i: (b, i, 0)),
            pl.BlockSpec((1, d), lambda b, i: (0, 0)),
            pl.BlockSpec((1, 1, d), lambda b, i: (b, 0, 0)),
            pl.BlockSpec((1, 1, d), lambda b, i: (b, 0, 0)),
            pl.BlockSpec((None,) + wq.shape[1:], lambda b, i: (l, 0, 0)),
            pl.BlockSpec((None,) + keys.shape[1:], lambda b, i: (l, 0, 0, 0, 0)),
        ],
        out_specs=[pl.BlockSpec((d, tq), lambda b, i: (0, b * nt + i)),
                   grouped_spec, grouped_spec, score_spec, score_spec],
        out_shape=[jax.ShapeDtypeStruct((d, ttot), BF16),
                   grouped_shape, grouped_shape, score_shape, score_shape],
        scratch_shapes=[pltpu.VMEM((PEER_HEADS, N_KEYS, tq), F32),
                        pltpu.VMEM((PEER_HEADS, N_KEYS, tq), F32)],
        compiler_params=_params("parallel", "parallel"),
        name="peer_prep",
    )(x, g, scale, shift, wq, keys)


PEER_TE = SUBLANES * N_KEYS
SQRT_HALF = math.sqrt(0.5)


def _peer_dense_kernel(ht_ref, u_ref, vt_ref, n1_ref, e1_ref, r2_ref, e2_ref,
                       x_ref, g2_ref, o_ref, acc_ref, a_ref, gt_ref, *, n_e, n_steps):
    s = pl.program_id(0)
    tt = ht_ref.shape[1]
    n_lane_tiles = tt // LANES
    n_blk = N_KEYS // PACKED_ROWS

    @pl.when(s == 0)
    def _():
        a_ref[...] = jnp.zeros_like(a_ref)
        gt_ref[...] = jnp.zeros_like(gt_ref)

    @pl.when(jnp.logical_or(s == 0, (s - 2) % n_e == 0))
    def _():
        acc_ref[...] = jnp.zeros_like(acc_ref)

    acc_ref[...] += jnp.dot(vt_ref[0], gt_ref[...], preferred_element_type=F32)

    grp = jnp.clip(s - 1, 0, n_steps - 1) % n_e
    packed = (PACKED_ROWS, LANES)
    for il in range(SUBLANES):
        for j in range(n_lane_tiles):
            lanes = pl.ds(j * LANES, LANES)
            acc = [None] * n_blk
            for hd in range(PEER_HEADS):
                n1b = jnp.broadcast_to(n1_ref[hd, grp, pl.ds(il, 1), lanes], packed).astype(BF16)
                e1b = jnp.broadcast_to(e1_ref[hd, grp, pl.ds(il, 1), lanes], packed).astype(BF16)
                for k in range(n_blk):
                    rows = pl.ds(k * SUBLANES, SUBLANES)
                    rank2 = pltpu.bitcast(r2_ref[hd, rows, lanes], BF16)
                    e2 = pltpu.bitcast(e2_ref[hd, rows, lanes], BF16)
                    w = jnp.where(rank2 < n1b, e2, 0.0) * e1b
                    acc[k] = w if hd == 0 else acc[k] + w
            for k in range(n_blk):
                rows = pl.ds(il * N_KEYS + k * PACKED_ROWS, PACKED_ROWS)
                a = a_ref[rows, lanes].astype(BF16)
                gelu2 = a * (1.0 + lax.erf(a * SQRT_HALF))
                gt_ref[rows, lanes] = acc[k] * gelu2

    a_ref[...] = jnp.dot(u_ref[...], ht_ref[...], preferred_element_type=F32)

    @pl.when(jnp.logical_and(s >= 2, (s - 2) % n_e == n_e - 1))
    def _():
        o_ref[...] = x_ref[...] + g2_ref[0] * acc_ref[...].T


def _peer_dense_call(ht, u, vt, n1, e1, r2, e2, x, g2, t_per_batch, l):
    ttot, d = x.shape
    tt = min(512, t_per_batch)
    n_e = N_EXPERTS // PEER_TE
    n_steps = (ttot // tt) * n_e
    per_batch = t_per_batch // tt
    tile = lambda s, lag: jnp.clip(s - lag, 0, n_steps - 1) // n_e
    group = lambda s, lag: jnp.clip(s - lag, 0, n_steps - 1) % n_e
    score_spec = pl.BlockSpec((PEER_HEADS, N_KEYS // 2, tt), lambda s: (0, 0, tile(s, 1)))
    grouped_spec = pl.BlockSpec((PEER_HEADS, N_KEYS // SUBLANES, SUBLANES, tt),
                                lambda s: (0, 0, 0, tile(s, 1)))
    return pl.pallas_call(
        functools.partial(_peer_dense_kernel, n_e=n_e, n_steps=n_steps),
        grid=(n_steps + 2,),
        in_specs=[
            pl.BlockSpec((d, tt), lambda s: (0, tile(s, 0))),
            pl.BlockSpec((None, PEER_TE, d), lambda s: (l, group(s, 0), 0)),
            pl.BlockSpec((None, 1, d, PEER_TE), lambda s: (l, group(s, 2), 0, 0)),
            grouped_spec, grouped_spec, score_spec, score_spec,
            pl.BlockSpec((tt, d), lambda s: (tile(s, 2), 0)),
            pl.BlockSpec((1, 1, d), lambda s: (tile(s, 2) // per_batch, 0, 0)),
        ],
        out_specs=pl.BlockSpec((tt, d), lambda s: (tile(s, 2), 0)),
        out_shape=jax.ShapeDtypeStruct((ttot, d), F32),
        scratch_shapes=[pltpu.VMEM((d, tt), F32),
                        pltpu.VMEM((PEER_TE, tt), F32),
                        pltpu.VMEM((PEER_TE, tt), BF16)],
        compiler_params=_params("arbitrary"),
        name="peer_dense",
    )(ht, u, vt, n1, e1, r2, e2, x, g2)


def _final_norm_kernel(x_ref, g_ref, o_ref):
    x = x_ref[...]
    ms = jnp.mean(x * x, axis=-1, keepdims=True)
    o_ref[...] = x * lax.rsqrt(ms + EPS) * g_ref[...]


def _final_norm_call(x, g):
    n, d = x.shape
    tm = 1024
    return pl.pallas_call(
        _final_norm_kernel,
        grid=(n // tm,),
        in_specs=[pl.BlockSpec((tm, d), lambda i: (i, 0)), pl.BlockSpec((1, d), lambda i: (0, 0))],
        out_specs=pl.BlockSpec((tm, d), lambda i: (i, 0)),
        out_shape=jax.ShapeDtypeStruct((n, d), F32),
        compiler_params=_params("parallel"),
        name="final_norm",
    )(x, g)


def _mixer(x, mod, l, norm1_g, w_in, conv_w, w_out, lbf, lbb, s0f, s0b, grid_conv, full):
    sh1, sc1, g1 = mod[0], mod[1], mod[2]
    gates, feats = _proj_call(x, norm1_g[l][None, :], sc1, sh1, w_in, l)
    o_f, o_b, s_f, s_b = _gla_call(gates, feats, lbf, lbb, s0f, s0b)
    if not full:
        return None, s_f, s_b
    x = _mixout_call(o_f, o_b, feats, conv_w[l], w_out[l], x, g1, grid_conv)
    return x, s_f, s_b


def _peer(x, mod, l, norm2_g, wq, keys, u, vt):
    bsz, t, d = x.shape
    sh2, sc2, g2 = mod[3], mod[4], mod[5]
    ht, n1, e1, r2, e2 = _peer_prep_call(x, norm2_g[l][None, :], sc2, sh2, wq, keys, l)
    out = _peer_dense_call(ht, u, vt, n1, e1, r2, e2, x.reshape(bsz * t, d), g2, t, l)
    return out.reshape(bsz, t, d)


def kernel(x, c, ctx, c_ctx, w_mod, b_mod, norm1_g, norm2_g, w_in, conv_w, w_out, lb_logits,
           peer_wq, peer_subkeys, peer_u, peer_v, final_g):
    bsz, t, d = x.shape
    depth = w_mod.shape[0]

    p_lb = jax.nn.softmax(lb_logits.astype(F32), axis=0)
    lower = jnp.cumsum(p_lb, axis=0) - p_lb[0]

    cond = jnp.zeros((COND_ROWS, d), F32).at[:bsz].set(c).at[bsz].set(c_ctx)
    mod = _mod_call(cond, w_mod, b_mod)
    mod = mod.reshape(depth, COND_ROWS, N_MOD, d)
    mod_x = jnp.transpose(mod[:, :bsz], (0, 2, 1, 3))[:, :, :, None, :]
    mod_c = jnp.broadcast_to(mod[:, bsz][:, :, None, None, :], mod_x.shape)

    w_in_b = w_in.astype(BF16)
    w_out_b = w_out.astype(BF16)
    wq_b = peer_wq.astype(BF16)
    keys_b = peer_subkeys.astype(BF16)
    u_b = peer_u.astype(BF16)
    vt_b = jnp.swapaxes(peer_v.astype(BF16).reshape(depth, N_EXPERTS // PEER_TE, PEER_TE, d), 2, 3)

    zero_state = jnp.zeros((bsz, HG_HEADS, HG_DIM, HG_DIM), F32)
    xc = ctx
    for l in range(depth):
        lbf = lower[l, 0][None, :]
        lbb = lower[l, 1][None, :]
        full = l < depth - 1
        xc_new, s_f, s_b = _mixer(xc, mod_c[l], l, norm1_g, w_in_b, conv_w, w_out_b, lbf, lbb,
                                  zero_state, zero_state, False, full)
        if full:
            xc = _peer(xc_new, mod_c[l], l, norm2_g, wq_b, keys_b, u_b, vt_b)
        x, _, _ = _mixer(x, mod_x[l], l, norm1_g, w_in_b, conv_w, w_out_b, lbf, lbb,
                         s_f, s_b, True, True)
        x = _peer(x, mod_x[l], l, norm2_g, wq_b, keys_b, u_b, vt_b)
    return _final_norm_call(x.reshape(bsz * t, d), final_g[None, :]).reshape(bsz, t, d)
```

```python
import functools
import math

import numpy as np
import jax
import jax.numpy as jnp
from jax import lax
from jax.experimental import pallas as pl
from jax.experimental.pallas import tpu as pltpu

F32 = jnp.float32
BF16 = jnp.bfloat16

D_MODEL = 1024
GRID_W = 64
EPS = 1e-6
F_FLOOR = 1e-20
N_MOD = 6
HG_WIDTH = 512
HG_HEADS = 4
HG_DIM = HG_WIDTH // HG_HEADS
SC_WIDTH = 512
SC_HALF = SC_WIDTH // 2
IN_COLS = 5 * HG_WIDTH + 3 * SC_WIDTH
PEER_HEADS = 8
PEER_QDIM = 256
PEER_HALF = PEER_QDIM // 2
N_KEYS = 128
N_EXPERTS = N_KEYS * N_KEYS
PEER_TOPK = 16

SUBLANES = 8
LANES = 128
PACKED_ROWS = 2 * SUBLANES
VMEM_LIMIT = 48 * 1024 * 1024

GLA_CHUNK = 128
GLA_LEVELS = (64, 32, 16, 8, 4, 2, 1)
GLA_GROUP_HEADS = 2
COND_ROWS = 8

COL_IV, COL_ZF, COL_ZB, COL_Q, COL_G, COL_CG, COL_BG, COL_HV = range(8)
GATE_ZF, GATE_ZB = range(2)
FEAT_IV, FEAT_Q, FEAT_G, FEAT_CG, FEAT_BG, FEAT_HV = range(6)


def _params(*sem):
    return pltpu.CompilerParams(dimension_semantics=sem, vmem_limit_bytes=VMEM_LIMIT)


def _dot(a, b):
    return jnp.dot(a.astype(BF16), b.astype(BF16), preferred_element_type=F32)


def _dot_nt(a, b):
    return lax.dot_general(a.astype(BF16), b.astype(BF16), (((1,), (1,)), ((), ())),
                           preferred_element_type=F32)


def _dot_tn(a, b):
    return lax.dot_general(a.astype(BF16), b.astype(BF16), (((0,), (0,)), ((), ())),
                           preferred_element_type=F32)


def _mod_kernel(cond_ref, w_ref, b_ref, o_ref):
    c = cond_ref[...]
    s = c * jax.nn.sigmoid(c)
    o_ref[0] = _dot(s, w_ref[0]) + b_ref[0]


def _mod_call(cond, w_mod, b_mod):
    depth, d, n = w_mod.shape
    tn = n // 2
    return pl.pallas_call(
        _mod_kernel,
        grid=(depth, n // tn),
        in_specs=[
            pl.BlockSpec((COND_ROWS, d), lambda l, j: (0, 0)),
            pl.BlockSpec((1, d, tn), lambda l, j: (l, 0, j)),
            pl.BlockSpec((1, 1, tn), lambda l, j: (l, 0, j)),
        ],
        out_specs=pl.BlockSpec((1, COND_ROWS, tn), lambda l, j: (l, 0, j)),
        out_shape=jax.ShapeDtypeStruct((depth, COND_ROWS, n), F32),
        compiler_params=_params("parallel", "parallel"),
        name="mod",
    )(cond, w_mod, b_mod.reshape(depth, 1, n))


def _norm_mod(x, g, scale, shift):
    ms = jnp.mean(x * x, axis=-1, keepdims=True)
    y = x * lax.rsqrt(ms + EPS) * g
    return y * (1.0 + scale) + shift


def _proj_kernel(x_ref, g_ref, sc_ref, sh_ref, w_ref, gates_ref, feats_ref):
    h = _norm_mod(x_ref[0], g_ref[...], sc_ref[0], sh_ref[0])
    y = _dot(h, w_ref[...])
    w = HG_WIDTH
    gates_ref[0] = y[:, COL_ZF * w:(COL_ZB + 1) * w]
    feats_ref[0, :, 0:w] = y[:, COL_IV * w:(COL_IV + 1) * w].astype(BF16)
    feats_ref[0, :, w:] = y[:, COL_Q * w:].astype(BF16)


def _proj_call(x, g, scale, shift, w, l):
    bsz, t, d = x.shape
    n = w.shape[2]
    n_gate = 2 * HG_WIDTH
    tm = min(256, t)
    return pl.pallas_call(
        _proj_kernel,
        grid=(bsz, t // tm),
        in_specs=[
            pl.BlockSpec((1, tm, d), lambda b, i: (b, i, 0)),
            pl.BlockSpec((1, d), lambda b, i: (0, 0)),
            pl.BlockSpec((1, 1, d), lambda b, i: (b, 0, 0)),
            pl.BlockSpec((1, 1, d), lambda b, i: (b, 0, 0)),
            pl.BlockSpec((None, d, n), lambda b, i: (l, 0, 0)),
        ],
        out_specs=[pl.BlockSpec((1, tm, n_gate), lambda b, i: (b, i, 0)),
                   pl.BlockSpec((1, tm, n - n_gate), lambda b, i: (b, i, 0))],
        out_shape=[jax.ShapeDtypeStruct((bsz, t, n_gate), F32),
                   jax.ShapeDtypeStruct((bsz, t, n - n_gate), BF16)],
        compiler_params=_params("parallel", "parallel"),
        name="proj",
    )(x, g, scale, shift, w)


def _gla_sum_matrix(backward):
    c = GLA_CHUNK
    t = np.arange(c)[:, None]
    s = np.arange(c)[None, :]
    mats = [s >= t if backward else s <= t]
    for n in GLA_LEVELS:
        if n >= SUBLANES:
            continue
        mid = (t // (2 * n)) * (2 * n) + n
        if not backward:
            m = np.where(t >= mid, (s >= mid) & (s <= t), (s > t) & (s < mid))
        else:
            m = np.where(t < mid, (s >= t) & (s < mid), (s >= mid) & (s < t))
        mats.append(m)
    return np.concatenate([m.astype(np.float32) for m in mats], axis=0)


def _gla_masks(backward):
    c = GLA_CHUNK
    row = lax.broadcasted_iota(jnp.int32, (c, c), 0)
    col = lax.broadcasted_iota(jnp.int32, (c, c), 1)
    qrow, lvl = [], []
    for n in GLA_LEVELS:
        r_hi = (row & (2 * n - 1)) >= n
        c_hi = (col & (2 * n - 1)) >= n
        same = (row & ~(2 * n - 1)) == (col & ~(2 * n - 1))
        if not backward:
            qrow.append(r_hi)
            lvl.append(same & r_hi & jnp.logical_not(c_hi))
        else:
            qrow.append(jnp.logical_not(r_hi))
            lvl.append(same & jnp.logical_not(r_hi) & c_hi)
    return qrow, lvl, row == col


def _gla_gates(z, lb, msum):
    sig = jax.nn.sigmoid(z)
    f = lb + (1.0 - lb) * sig
    logf = jnp.log(jnp.maximum(f, F_FLOOR))
    kk = (1.0 - lb) * jax.nn.sigmoid(-z)
    sums = jnp.dot(msum, logf.astype(BF16), preferred_element_type=F32)
    return kk, sums


def _boundary_rows(b, n, backward):
    c, w = b.shape
    parts = []
    for start in range(0, c, 2 * n):
        r = start + n if backward else start + n - 1
        parts.append(jnp.broadcast_to(b[r:r + 1, :], (2 * n, w)))
    return jnp.concatenate(parts, axis=0)


def _gla_chains(chains, ones):
    c = GLA_CHUNK
    atts = [jnp.where(masks[2], jnp.dot((q * kk).astype(BF16), ones, preferred_element_type=F32), 0.0)
            for q, v, kk, sums, st, masks, backward in chains]
    sub_tile = [n for n in GLA_LEVELS if n < SUBLANES]
    for li, n in enumerate(GLA_LEVELS):
        for ci, (q, v, kk, sums, st, masks, backward) in enumerate(chains):
            if n >= SUBLANES:
                exponent = -jnp.abs(sums[0:c] - _boundary_rows(sums[0:c], n, backward))
            else:
                k = 1 + sub_tile.index(n)
                exponent = sums[k * c:(k + 1) * c]
            mixed = (jnp.where(masks[0][li][:, :HG_DIM], q, kk) * jnp.exp(exponent)).astype(BF16)
            a = lax.dot_general(mixed, mixed, (((1,), (1,)), ((), ())), preferred_element_type=F32)
            atts[ci] = jnp.where(masks[1][li], a, atts[ci])
    out = []
    for att, (q, v, kk, sums, st, masks, backward) in zip(atts, chains):
        b = sums[0:c]
        b_end = b[0:1, :] if backward else b[c - 1:c, :]
        q_in = q * jnp.exp(b)
        o = _dot(att, v) + _dot_nt(q_in, st)
        k_out = kk * jnp.exp(b_end - b)
        out.append((o, jnp.exp(b_end) * st + _dot_tn(v, k_out)))
    return out


def _gla_kernel(ivf_ref, zf_ref, qf_ref, ivb_ref, zb_ref, qb_ref, lbf_ref, lbb_ref,
                s0f_ref, s0b_ref, mf_ref, mb_ref,
                of_ref, ob_ref, sf_ref, sb_ref, st_ref):
    n = pl.program_id(1)

    @pl.when(n == 0)
    def _():
        st_ref[0] = s0f_ref[0]
        st_ref[1] = s0b_ref[0]

    ones = jnp.ones((HG_DIM, HG_DIM), BF16)
    masks_f = _gla_masks(False)
    masks_b = _gla_masks(True)
    kk_f, sums_f = _gla_gates(zf_ref[0], lbf_ref[...], mf_ref[...])
    kk_b, sums_b = _gla_gates(zb_ref[0], lbb_ref[...], mb_ref[...])
    for h0 in range(0, HG_HEADS, GLA_GROUP_HEADS):
        heads = range(h0, h0 + GLA_GROUP_HEADS)
        sls = [slice(h * HG_DIM, (h + 1) * HG_DIM) for h in heads]
        chains = [(qf_ref[0, :, sl].astype(F32), ivf_ref[0, :, sl], kk_f[:, sl], sums_f[:, sl],
                   st_ref[0, h], masks_f, False) for h, sl in zip(heads, sls)]
        chains += [(qb_ref[0, :, sl].astype(F32), ivb_ref[0, :, sl], kk_b[:, sl], sums_b[:, sl],
                    st_ref[1, h], masks_b, True) for h, sl in zip(heads, sls)]
        res = _gla_chains(chains, ones)
        for i, (h, sl) in enumerate(zip(heads, sls)):
            of_ref[0, :, sl], st_ref[0, h] = res[i]
            ob_ref[0, :, sl], st_ref[1, h] = res[GLA_GROUP_HEADS + i]

    @pl.when(n == pl.num_programs(1) - 1)
    def _():
        sf_ref[0] = st_ref[0]
        sb_ref[0] = st_ref[1]


def _gla_call(gates, feats, lbf, lbb, s0f, s0b):
    bsz, t, _ = gates.shape
    c = GLA_CHUNK
    nc = t // c
    w = HG_WIDTH
    msum_f = jnp.asarray(_gla_sum_matrix(False), BF16)
    msum_b = jnp.asarray(_gla_sum_matrix(True), BF16)

    def fwd(col):
        return pl.BlockSpec((1, c, w), lambda b, n: (b, n, col))

    def bwd(col):
        return pl.BlockSpec((1, c, w), lambda b, n: (b, nc - 1 - n, col))

    state_spec = pl.BlockSpec((1, HG_HEADS, HG_DIM, HG_DIM), lambda b, n: (b, 0, 0, 0))
    const = lambda shape: pl.BlockSpec(shape, lambda b, n: (0,) * len(shape))
    return pl.pallas_call(
        _gla_kernel,
        grid=(bsz, nc),
        in_specs=[fwd(FEAT_IV), fwd(GATE_ZF), fwd(FEAT_Q), bwd(FEAT_IV), bwd(GATE_ZB), bwd(FEAT_Q),
                  const((1, w)), const((1, w)), state_spec, state_spec,
                  const(msum_f.shape), const(msum_b.shape)],
        out_specs=[pl.BlockSpec((1, c, w), lambda b, n: (b, n, 0)),
                   pl.BlockSpec((1, c, w), lambda b, n: (b, nc - 1 - n, 0)),
                   state_spec, state_spec],
        out_shape=[jax.ShapeDtypeStruct((bsz, t, w), F32),
                   jax.ShapeDtypeStruct((bsz, t, w), F32),
                   jax.ShapeDtypeStruct((bsz, HG_HEADS, HG_DIM, HG_DIM), F32),
                   jax.ShapeDtypeStruct((bsz, HG_HEADS, HG_DIM, HG_DIM), F32)],
        scratch_shapes=[pltpu.VMEM((2, HG_HEADS, HG_DIM, HG_DIM), F32)],
        compiler_params=_params("parallel", "arbitrary"),
        name="gla",
    )(feats, gates, feats, feats, gates, feats, lbf, lbb, s0f, s0b, msum_f, msum_b)


def _mixout_body(of_ref, ob_ref, g_ref, cg_ref, bg_ref, hv_ref, halo, cw_ref, wo_ref,
                 x_ref, g1_ref, o_ref, grid_conv):
    tm = of_ref.shape[1]
    o = of_ref[0] + ob_ref[0]
    heads = []
    for h in range(HG_HEADS):
        oh = o[:, h * HG_DIM:(h + 1) * HG_DIM]
        heads.append(oh * lax.rsqrt(jnp.mean(oh * oh, axis=-1, keepdims=True) + EPS))
    g = g_ref[0].astype(F32)
    o_rec = jnp.concatenate(heads, axis=-1) * (g * jax.nn.sigmoid(g))

    u = cg_ref[0].astype(F32) * hv_ref[0].astype(F32)
    w0, w1, w2 = cw_ref[0:1, :], cw_ref[1:2, :], cw_ref[2:3, :]
    pos = lax.broadcasted_iota(jnp.int32, (tm, 1), 0)
    if grid_conv:
        cgp_ref, hvp_ref, cgn_ref, hvn_ref = halo
        i = pl.program_id(1)
        last = pl.num_programs(1) - 1
        colpos = pos & (GRID_W - 1)
        uh = u[:, :SC_HALF]
        left = jnp.where(colpos == 0, 0.0, pltpu.roll(uh, 1, 0))
        right = jnp.where(colpos == GRID_W - 1, 0.0, pltpu.roll(uh, tm - 1, 0))
        conv_h = w0[:, :SC_HALF] * left + w1[:, :SC_HALF] * uh + w2[:, :SC_HALF] * right
        uv = u[:, SC_HALF:]
        halo_u = lambda cg, hv: cg[0][:, SC_HALF:].astype(F32) * hv[0][:, SC_HALF:].astype(F32)
        up_halo = jnp.where(i == 0, 0.0, halo_u(cgp_ref, hvp_ref))
        dn_halo = jnp.where(i == last, 0.0, halo_u(cgn_ref, hvn_ref))
        up = jnp.concatenate([up_halo, uv[:tm - GRID_W]], axis=0)
        down = jnp.concatenate([uv[GRID_W:], dn_halo], axis=0)
        conv_v = w0[:, SC_HALF:] * up + w1[:, SC_HALF:] * uv + w2[:, SC_HALF:] * down
        conv = jnp.concatenate([conv_h, conv_v], axis=-1)
    else:
        left = jnp.where(pos == 0, 0.0, pltpu.roll(u, 1, 0))
        right = jnp.where(pos == tm - 1, 0.0, pltpu.roll(u, tm - 1, 0))
        conv = w0 * left + w1 * u + w2 * right
    o_conv = bg_ref[0].astype(F32) * conv
    y = _dot(o_rec, wo_ref[0:HG_WIDTH, :]) + _dot(o_conv, wo_ref[HG_WIDTH:, :])
    o_ref[0] = x_ref[0] + g1_ref[0] * y


def _mixout_grid_kernel(of_ref, ob_ref, g_ref, cg_ref, bg_ref, hv_ref, cgp_ref, hvp_ref,
                        cgn_ref, hvn_ref, cw_ref, wo_ref, x_ref, g1_ref, o_ref):
    _mixout_body(of_ref, ob_ref, g_ref, cg_ref, bg_ref, hv_ref,
                 (cgp_ref, hvp_ref, cgn_ref, hvn_ref), cw_ref, wo_ref, x_ref, g1_ref, o_ref, True)


def _mixout_seq_kernel(of_ref, ob_ref, g_ref, cg_ref, bg_ref, hv_ref, cw_ref, wo_ref,
                       x_ref, g1_ref, o_ref):
    _mixout_body(of_ref, ob_ref, g_ref, cg_ref, bg_ref, hv_ref, None, cw_ref, wo_ref,
                 x_ref, g1_ref, o_ref, False)


def _mixout_call(o_f, o_b, proj, conv_w, w_out, x, g1, grid_conv):
    bsz, t, d = x.shape
    w = HG_WIDTH
    tm = 512 if grid_conv else t
    nt = t // tm
    hb = tm // GRID_W
    nhalo = t // GRID_W

    def col(cidx):
        return pl.BlockSpec((1, tm, w), lambda b, i: (b, i, cidx))

    def prev(cidx):
        return pl.BlockSpec((1, GRID_W, w), lambda b, i: (b, jnp.maximum(i * hb - 1, 0), cidx))

    def nxt(cidx):
        return pl.BlockSpec((1, GRID_W, w),
                            lambda b, i: (b, jnp.minimum((i + 1) * hb, nhalo - 1), cidx))

    row = pl.BlockSpec((1, tm, w), lambda b, i: (b, i, 0))
    in_specs = [row, row, col(FEAT_G), col(FEAT_CG), col(FEAT_BG), col(FEAT_HV)]
    args = [o_f, o_b, proj, proj, proj, proj]
    if grid_conv:
        in_specs += [prev(FEAT_CG), prev(FEAT_HV), nxt(FEAT_CG), nxt(FEAT_HV)]
        args += [proj, proj, proj, proj]
    in_specs += [
        pl.BlockSpec((3, w), lambda b, i: (0, 0)),
        pl.BlockSpec(w_out.shape, lambda b, i: (0, 0)),
        pl.BlockSpec((1, tm, d), lambda b, i: (b, i, 0)),
        pl.BlockSpec((1, 1, d), lambda b, i: (b, 0, 0)),
    ]
    args += [conv_w, w_out, x, g1]
    return pl.pallas_call(
        _mixout_grid_kernel if grid_conv else _mixout_seq_kernel,
        grid=(bsz, nt),
        in_specs=in_specs,
        out_specs=pl.BlockSpec((1, tm, d), lambda b, i: (b, i, 0)),
        out_shape=jax.ShapeDtypeStruct((bsz, t, d), F32),
        compiler_params=_params("parallel", "parallel"),
        name="mixout_grid" if grid_conv else "mixout_seq",
    )(*args)


def _oddeven_merge_sort_pairs(n):
    pairs = []
    p = 1
    while p < n:
        k = p
        while k >= 1:
            for j in range(k % p, n - k, 2 * k):
                for i in range(min(k, n - j - k)):
                    if (i + j) // (2 * p) == (i + j + k) // (2 * p):
                        pairs.append((i + j, i + j + k))
            k //= 2
        p *= 2
    return pairs


_SORT16 = _oddeven_merge_sort_pairs(PEER_TOPK)


def _sort_desc(xs):
    xs = list(xs)
    for i, j in _SORT16:
        a, b = xs[i], xs[j]
        xs[i], xs[j] = jnp.maximum(a, b), jnp.minimum(a, b)
    return xs


def _bitonic_merge_desc(xs):
    xs = list(xs)
    d = len(xs) // 2
    while d >= 1:
        for i in range(len(xs)):
            if i & d == 0:
                a, b = xs[i], xs[i + d]
                xs[i], xs[i + d] = jnp.maximum(a, b), jnp.minimum(a, b)
        d //= 2
    return xs


def _top_of_union(a, b):
    k = len(a)
    return [jnp.maximum(a[r], b[k - 1 - r]) for r in range(k)]


def _topk_rows(s):
    rows = [s[SUBLANES * v:SUBLANES * (v + 1), :] for v in range(N_KEYS // SUBLANES)]
    rows = _sort_desc(rows)
    for shift in (4, 2, 1):
        rolled = [pltpu.roll(r, shift, 0) for r in rows]
        rows = _bitonic_merge_desc(_top_of_union(rows, rolled))
    return rows


def _peer_prep_kernel(x_ref, g_ref, sc_ref, sh_ref, wq_ref, keys_ref,
                      ht_ref, n1_ref, e1_ref, r2_ref, e2_ref, s1_ref, s2_ref):
    tq = x_ref.shape[1]
    hf = _norm_mod(x_ref[0], g_ref[...], sc_ref[0], sh_ref[0])
    ht_ref[...] = hf.T.astype(BF16)
    qf = jnp.dot(hf.astype(BF16), wq_ref[...], preferred_element_type=F32)
    sub = lax.broadcasted_iota(jnp.int32, (SUBLANES, tq), 0)
    neg = jnp.full((SUBLANES, tq), -jnp.inf, F32)
    packed = [[jnp.zeros((SUBLANES, tq), F32)] * PEER_TOPK for _ in range(2)]
    for hd in range(PEER_HEADS):
        for p in range(2):
            lo = hd * PEER_QDIM + p * PEER_HALF
            s = _dot_nt(keys_ref[hd, p], qf[:, lo:lo + PEER_HALF])
            (s1_ref if p == 0 else s2_ref)[hd] = s
            top = _topk_rows(s)
            packed[p] = [jnp.where(sub == hd, top[r], packed[p][r]) for r in range(PEER_TOPK)]
    c1, c2 = packed
    pairs = [(a, b) for a in range(PEER_TOPK) for b in range(PEER_TOPK)
             if (a + 1) * (b + 1) <= PEER_TOPK]
    cand = {ab: c1[ab[0]] + c2[ab[1]] for ab in pairs}
    cands = [cand[ab] for ab in pairs]
    cands += [neg] * (-len(cands) % PEER_TOPK)
    groups = [_sort_desc(cands[i:i + PEER_TOPK]) for i in range(0, len(cands), PEER_TOPK)]
    while len(groups) > 2:
        nxt = [_bitonic_merge_desc(_top_of_union(groups[i], groups[i + 1]))
               for i in range(0, len(groups) - 1, 2)]
        if len(groups) % 2:
            nxt.append(groups[-1])
        groups = nxt
    top = _top_of_union(groups[0], groups[1]) if len(groups) == 2 else groups[0]
    tau = functools.reduce(jnp.minimum, top)
    m1, m2 = c1[0], c2[0]
    mx = m1 + m2
    z = functools.reduce(lambda a, b: a + b, [jnp.exp(t - mx) for t in top])
    inv_z = 0.5 / z
    counts = []
    for a in range(PEER_TOPK):
        n = jnp.zeros((SUBLANES, tq), F32)
        for b in range(PEER_TOPK):
            if (a, b) in cand:
                n = n + jnp.where(cand[(a, b)] >= tau, 1.0, 0.0)
        counts.append(n)
    half = PEER_TOPK // 2
    theta = []
    for m in range(1, half + 1):
        th = jnp.full((SUBLANES, tq), jnp.inf, F32)
        for a in range(PEER_TOPK // m):
            th = jnp.where(counts[a] >= m, c1[a], th)
        theta.append(th)
    grouped = (N_KEYS // SUBLANES, SUBLANES, tq)
    for hd in range(PEER_HEADS):
        s1 = s1_ref[hd]
        s2 = s2_ref[hd]
        n1 = jnp.zeros((N_KEYS, tq), F32)
        for m in range(1, half + 1):
            n1 = jnp.where(s1 >= theta[m - 1][hd:hd + 1, :], float(m), n1)
        n1 = jnp.where(s1 >= c1[0][hd:hd + 1, :], counts[0][hd:hd + 1, :], n1)
        rank2 = jnp.full((N_KEYS, tq), float(PEER_TOPK), F32)
        for r in reversed(range(PEER_TOPK)):
            rank2 = jnp.where(s2 == c2[r][hd:hd + 1, :], float(r), rank2)
        n1_ref[hd] = n1.reshape(grouped)
        r2_ref[hd] = pltpu.bitcast(rank2.astype(BF16), jnp.uint32)
        e1 = jnp.exp(s1 - m1[hd:hd + 1, :]) * inv_z[hd:hd + 1, :]
        e1_ref[hd] = e1.reshape(grouped)
        e2_ref[hd] = pltpu.bitcast(jnp.exp(s2 - m2[hd:hd + 1, :]).astype(BF16), jnp.uint32)


def _peer_prep_call(x, g, scale, shift, wq, keys, l):
    bsz, t, d = x.shape
    tq = min(256, t)
    nt = t // tq
    ttot = bsz * t
    score_spec = pl.BlockSpec((PEER_HEADS, N_KEYS // 2, tq), lambda b, i: (0, 0, b * nt + i))
    score_shape = jax.ShapeDtypeStruct((PEER_HEADS, N_KEYS // 2, ttot), jnp.uint32)
    n_grp = N_KEYS // SUBLANES
    grouped_spec = pl.BlockSpec((PEER_HEADS, n_grp, SUBLANES, tq), lambda b, i: (0, 0, 0, b * nt + i))
    grouped_shape = jax.ShapeDtypeStruct((PEER_HEADS, n_grp, SUBLANES, ttot), F32)
    return pl.pallas_call(
        _peer_prep_kernel,
        grid=(bsz, nt),
        in_specs=[
            pl.BlockSpec((1, tq, d), lambda b, i: (b, i, 0)),
            pl.BlockSpec((1, d), lambda b, i: (0, 0)),
            pl.BlockSpec((1, 1, d), lambda b, i: (b, 0, 0)),
            pl.BlockSpec((1, 1, d), lambda b, i: (b, 0, 0)),
            pl.BlockSpec((None,) + wq.shape[1:], lambda b, i: (l, 0, 0)),
            pl.BlockSpec((None,) + keys.shape[1:], lambda b, i: (l, 0, 0, 0, 0)),
        ],
        out_specs=[pl.BlockSpec((d, tq), lambda b, i: (0, b * nt + i)),
                   grouped_spec, grouped_spec, score_spec, score_spec],
        out_shape=[jax.ShapeDtypeStruct((d, ttot), BF16),
                   grouped_shape, grouped_shape, score_shape, score_shape],
        scratch_shapes=[pltpu.VMEM((PEER_HEADS, N_KEYS, tq), F32),
                        pltpu.VMEM((PEER_HEADS, N_KEYS, tq), F32)],
        compiler_params=_params("parallel", "parallel"),
        name="peer_prep",
    )(x, g, scale, shift, wq, keys)


PEER_TE = SUBLANES * N_KEYS
SQRT_HALF = math.sqrt(0.5)


def _peer_dense_kernel(ht_ref, u_ref, vt_ref, n1_ref, e1_ref, r2_ref, e2_ref,
                       x_ref, g2_ref, fg_ref, o_ref, acc_ref, a_ref, gt_ref,
                       *, n_e, n_steps, final_norm):
    s = pl.program_id(0)
    tt = ht_ref.shape[1]
    n_lane_tiles = tt // LANES
    n_blk = N_KEYS // PACKED_ROWS

    @pl.when(s == 0)
    def _():
        a_ref[...] = jnp.zeros_like(a_ref)
        gt_ref[...] = jnp.zeros_like(gt_ref)

    @pl.when(jnp.logical_or(s == 0, (s - 2) % n_e == 0))
    def _():
        acc_ref[...] = jnp.zeros_like(acc_ref)

    acc_ref[...] += jnp.dot(vt_ref[0], gt_ref[...], preferred_element_type=F32)

    grp = jnp.clip(s - 1, 0, n_steps - 1) % n_e
    packed = (PACKED_ROWS, LANES)
    for il in range(SUBLANES):
        for j in range(n_lane_tiles):
            lanes = pl.ds(j * LANES, LANES)
            acc = [None] * n_blk
            for hd in range(PEER_HEADS):
                n1b = jnp.broadcast_to(n1_ref[hd, grp, pl.ds(il, 1), lanes], packed).astype(BF16)
                e1b = jnp.broadcast_to(e1_ref[hd, grp, pl.ds(il, 1), lanes], packed).astype(BF16)
                for k in range(n_blk):
                    rows = pl.ds(k * SUBLANES, SUBLANES)
                    rank2 = pltpu.bitcast(r2_ref[hd, rows, lanes], BF16)
                    e2 = pltpu.bitcast(e2_ref[hd, rows, lanes], BF16)
                    w = jnp.where(rank2 < n1b, e2, 0.0) * e1b
                    acc[k] = w if hd == 0 else acc[k] + w
            for k in range(n_blk):
                rows = pl.ds(il * N_KEYS + k * PACKED_ROWS, PACKED_ROWS)
                a = a_ref[rows, lanes].astype(BF16)
                gelu2 = a * (1.0 + lax.erf(a * SQRT_HALF))
                gt_ref[rows, lanes] = acc[k] * gelu2

    a_ref[...] = jnp.dot(u_ref[...], ht_ref[...], preferred_element_type=F32)

    @pl.when(jnp.logical_and(s >= 2, (s - 2) % n_e == n_e - 1))
    def _():
        y = x_ref[...] + g2_ref[0] * acc_ref[...].T
        if final_norm:
            y = y * lax.rsqrt(jnp.mean(y * y, axis=-1, keepdims=True) + EPS) * fg_ref[...]
        o_ref[...] = y


def _peer_dense_call(ht, u, vt, n1, e1, r2, e2, x, g2, final_g, t_per_batch, l, final_norm):
    ttot, d = x.shape
    tt = min(512, t_per_batch)
    n_e = N_EXPERTS // PEER_TE
    n_steps = (ttot // tt) * n_e
    per_batch = t_per_batch // tt
    tile = lambda s, lag: jnp.clip(s - lag, 0, n_steps - 1) // n_e
    group = lambda s, lag: jnp.clip(s - lag, 0, n_steps - 1) % n_e
    score_spec = pl.BlockSpec((PEER_HEADS, N_KEYS // 2, tt), lambda s: (0, 0, tile(s, 1)))
    grouped_spec = pl.BlockSpec((PEER_HEADS, N_KEYS // SUBLANES, SUBLANES, tt),
                                lambda s: (0, 0, 0, tile(s, 1)))
    return pl.pallas_call(
        functools.partial(_peer_dense_kernel, n_e=n_e, n_steps=n_steps, final_norm=final_norm),
        grid=(n_steps + 2,),
        in_specs=[
            pl.BlockSpec((d, tt), lambda s: (0, tile(s, 0))),
            pl.BlockSpec((None, PEER_TE, d), lambda s: (l, group(s, 0), 0)),
            pl.BlockSpec((None, 1, d, PEER_TE), lambda s: (l, group(s, 2), 0, 0)),
            grouped_spec, grouped_spec, score_spec, score_spec,
            pl.BlockSpec((tt, d), lambda s: (tile(s, 2), 0)),
            pl.BlockSpec((1, 1, d), lambda s: (tile(s, 2) // per_batch, 0, 0)),
            pl.BlockSpec((1, d), lambda s: (0, 0)),
        ],
        out_specs=pl.BlockSpec((tt, d), lambda s: (tile(s, 2), 0)),
        out_shape=jax.ShapeDtypeStruct((ttot, d), F32),
        scratch_shapes=[pltpu.VMEM((d, tt), F32),
                        pltpu.VMEM((PEER_TE, tt), F32),
                        pltpu.VMEM((PEER_TE, tt), BF16)],
        compiler_params=_params("arbitrary"),
        name="peer_dense",
    )(ht, u, vt, n1, e1, r2, e2, x, g2, final_g)


def _final_norm_kernel(x_ref, g_ref, o_ref):
    x = x_ref[...]
    ms = jnp.mean(x * x, axis=-1, keepdims=True)
    o_ref[...] = x * lax.rsqrt(ms + EPS) * g_ref[...]


def _final_norm_call(x, g):
    n, d = x.shape
    tm = 1024
    return pl.pallas_call(
        _final_norm_kernel,
        grid=(n // tm,),
        in_specs=[pl.BlockSpec((tm, d), lambda i: (i, 0)), pl.BlockSpec((1, d), lambda i: (0, 0))],
        out_specs=pl.BlockSpec((tm, d), lambda i: (i, 0)),
        out_shape=jax.ShapeDtypeStruct((n, d), F32),
        compiler_params=_params("parallel"),
        name="final_norm",
    )(x, g)


def _mixer(x, mod, l, norm1_g, w_in, conv_w, w_out, lbf, lbb, s0f, s0b, grid_conv, full):
    sh1, sc1, g1 = mod[0], mod[1], mod[2]
    gates, feats = _proj_call(x, norm1_g[l][None, :], sc1, sh1, w_in, l)
    o_f, o_b, s_f, s_b = _gla_call(gates, feats, lbf, lbb, s0f, s0b)
    if not full:
        return None, s_f, s_b
    x = _mixout_call(o_f, o_b, feats, conv_w[l], w_out[l], x, g1, grid_conv)
    return x, s_f, s_b


def _peer(x, mod, l, norm2_g, wq, keys, u, vt, final_g, final_norm=False):
    bsz, t, d = x.shape
    sh2, sc2, g2 = mod[3], mod[4], mod[5]
    ht, n1, e1, r2, e2 = _peer_prep_call(x, norm2_g[l][None, :], sc2, sh2, wq, keys, l)
    out = _peer_dense_call(ht, u, vt, n1, e1, r2, e2, x.reshape(bsz * t, d), g2, final_g, t, l,
                           final_norm)
    return out.reshape(bsz, t, d)


def kernel(x, c, ctx, c_ctx, w_mod, b_mod, norm1_g, norm2_g, w_in, conv_w, w_out, lb_logits,
           peer_wq, peer_subkeys, peer_u, peer_v, final_g):
    bsz, t, d = x.shape
    depth = w_mod.shape[0]

    p_lb = jax.nn.softmax(lb_logits.astype(F32), axis=0)
    lower = jnp.cumsum(p_lb, axis=0) - p_lb[0]

    cond = jnp.zeros((COND_ROWS, d), F32).at[:bsz].set(c).at[bsz].set(c_ctx)
    mod = _mod_call(cond, w_mod, b_mod)
    mod = mod.reshape(depth, COND_ROWS, N_MOD, d)
    mod_x = jnp.transpose(mod[:, :bsz], (0, 2, 1, 3))[:, :, :, None, :]
    mod_c = jnp.broadcast_to(mod[:, bsz][:, :, None, None, :], mod_x.shape)

    w_in_b = w_in.astype(BF16)
    w_out_b = w_out.astype(BF16)
    wq_b = peer_wq.astype(BF16)
    keys_b = peer_subkeys.astype(BF16)
    u_b = peer_u.astype(BF16)
    vt_b = jnp.swapaxes(peer_v.astype(BF16).reshape(depth, N_EXPERTS // PEER_TE, PEER_TE, d), 2, 3)

    zero_state = jnp.zeros((bsz, HG_HEADS, HG_DIM, HG_DIM), F32)
    fg = final_g[None, :]
    xc = ctx
    for l in range(depth):
        lbf = lower[l, 0][None, :]
        lbb = lower[l, 1][None, :]
        full = l < depth - 1
        xc_new, s_f, s_b = _mixer(xc, mod_c[l], l, norm1_g, w_in_b, conv_w, w_out_b, lbf, lbb,
                                  zero_state, zero_state, False, full)
        if full:
            xc = _peer(xc_new, mod_c[l], l, norm2_g, wq_b, keys_b, u_b, vt_b, fg)
        x, _, _ = _mixer(x, mod_x[l], l, norm1_g, w_in_b, conv_w, w_out_b, lbf, lbb,
                         s_f, s_b, True, True)
        x = _peer(x, mod_x[l], l, norm2_g, wq_b, keys_b, u_b, vt_b, fg, final_norm=not full)
    return x
```
